```python
import math
import jax, jax.numpy as jnp
from jax import lax
import numpy as np

D_MODEL = 1024
BATCH = 4
SEQ = 4096
DEPTH = 2

HEAD_DIM = 64
N_EVEN = (DEPTH + 1) // 2
N_ODD = DEPTH // 2
A_HEADS = D_MODEL // (4 * HEAD_DIM)
B_HEADS = D_MODEL // (2 * HEAD_DIM)
IDX_HEADS = 8
IDX_DIM = 64
TOPK_MAX = 256
C_HEADS = D_MODEL // HEAD_DIM
C_KV_HEADS = C_HEADS // 4
WINDOW = 128
FFN_HIDDEN = ((8 * D_MODEL // 3 + 255) // 256) * 256
Q_BLOCK = 128
NORM_EPS = 1e-6

EVEN_SIZES = (A_HEADS * 2 * HEAD_DIM,
              A_HEADS * 2 * HEAD_DIM,
              A_HEADS * 2 * HEAD_DIM,
              B_HEADS * HEAD_DIM,
              HEAD_DIM,
              HEAD_DIM,
              IDX_HEADS * IDX_DIM,
              IDX_DIM,
              IDX_HEADS)
EVEN_COLS = sum(EVEN_SIZES)
ODD_SIZES = (C_HEADS * HEAD_DIM, C_KV_HEADS * HEAD_DIM, C_KV_HEADS * HEAD_DIM)
ODD_COLS = sum(ODD_SIZES)

kernel_name = 'hybrid_diffattn_dsa_swa_sink_adaln'


def _rms_norm(x, g):
    xf = x.astype(jnp.float32)
    y = xf * lax.rsqrt(jnp.mean(xf * xf, axis=-1, keepdims=True) + NORM_EPS)
    return (y * g.astype(jnp.float32)).astype(x.dtype)


def _split_cols(t, sizes):
    offs = np.cumsum(np.asarray(sizes))[:-1].tolist()
    return jnp.split(t, offs, axis=-1)


def _alibi_slopes(n):
    return jnp.asarray(2.0 ** (-8.0 * np.arange(1, n + 1) / n), dtype=jnp.float32)


def _diff_attention(q, k, v, lam, slopes):
    b, s, h = q.shape[:3]
    nb = s // Q_BLOCK
    qb = jnp.moveaxis(q.reshape(b, nb, Q_BLOCK, h, 2, HEAD_DIM), 1, 0)
    key_pos = jnp.arange(s)
    scale = HEAD_DIM ** -0.5

    def block(args):
        qi, i = args
        dist = (i * Q_BLOCK + jnp.arange(Q_BLOCK))[:, None] - key_pos[None, :]
        logits = jnp.einsum('bqhmd,bshmd->bhmqs', qi, k).astype(jnp.float32) * scale
        logits = logits - slopes[None, :, None, None, None] * dist.astype(jnp.float32)
        logits = jnp.where(dist >= 0, logits, -jnp.inf)
        p = jax.nn.softmax(logits, axis=-1)
        a = p[:, :, 0] - lam * p[:, :, 1]
        return jnp.einsum('bhqs,bshe->bqhe', a.astype(v.dtype), v)

    out = lax.map(block, (qb, jnp.arange(nb)))
    return jnp.moveaxis(out, 0, 1).reshape(b, s, h, 2 * HEAD_DIM)


def _dsa_attention(q, k, v, iq, ik, iw, slopes, topk):
    b, s, h = q.shape[:3]
    nb = s // Q_BLOCK
    key_pos = jnp.arange(s)
    scale = HEAD_DIM ** -0.5
    gather = jax.vmap(lambda t, idx: t[idx])

    def blocks(t):
        return jnp.moveaxis(t.reshape((b, nb, Q_BLOCK) + t.shape[2:]), 1, 0)

    def block(args):
        qi, iqi, iwi, i = args
        q_pos = i * Q_BLOCK + jnp.arange(Q_BLOCK)
        causal = key_pos[None, :] <= q_pos[:, None]
        idx_logits = jnp.einsum('bqhd,bsd->bqhs', iqi, ik).astype(jnp.float32) * IDX_DIM ** -0.5
        score = jnp.einsum('bqhs,bqh->bqs', jax.nn.relu(idx_logits), iwi.astype(jnp.float32))
        score = jnp.where(causal[None], score, -jnp.inf)
        _, sel = lax.top_k(score, topk)
        k_sel = gather(k, sel)
        v_sel = gather(v, sel)
        dist = (q_pos[None, :, None] - sel)[:, None]
        logits = jnp.einsum('bqhd,bqkd->bhqk', qi, k_sel).astype(jnp.float32) * scale
        logits = logits - slopes[None, :, None, None] * dist.astype(jnp.float32)
        logits = jnp.where(dist >= 0, logits, -jnp.inf)
        p = jax.nn.softmax(logits, axis=-1)
        return jnp.einsum('bhqk,bqkd->bqhd', p.astype(v.dtype), v_sel)

    out = lax.map(block, (blocks(q), blocks(iq), blocks(iw), jnp.arange(nb)))
    return jnp.moveaxis(out, 0, 1).reshape(b, s, h, HEAD_DIM)


def _swa_sink_attention(q, k, v, sinks, slopes):
    b, s = q.shape[:2]
    nb = s // WINDOW
    g = C_HEADS // C_KV_HEADS
    qb = q.reshape(b, nb, WINDOW, C_KV_HEADS, g, HEAD_DIM)

    def band(t):
        tb = t.reshape(b, nb, WINDOW, C_KV_HEADS, HEAD_DIM)
        prev = jnp.pad(tb, ((0, 0), (1, 0), (0, 0), (0, 0), (0, 0)))[:, :-1]
        return jnp.concatenate([prev, tb], axis=2)

    kk, vv = band(k), band(v)
    i = jnp.arange(WINDOW)
    j = jnp.arange(2 * WINDOW)
    dist = WINDOW + i[:, None] - j[None, :]
    key_pos = jnp.arange(nb)[:, None] * WINDOW - WINDOW + j[None, :]
    valid = ((dist >= 0) & (dist < WINDOW))[None] & (key_pos >= 0)[:, None, :]
    logits = jnp.einsum('bnqhgd,bnkhd->bnhgqk', qb, kk).astype(jnp.float32) * HEAD_DIM ** -0.5
    sl = slopes.reshape(C_KV_HEADS, g)
    logits = logits - sl[None, None, :, :, None, None] * dist.astype(jnp.float32)
    logits = jnp.where(valid[None, :, None, None], logits, -jnp.inf)
    sink = jnp.broadcast_to(sinks.astype(jnp.float32).reshape(1, 1, C_KV_HEADS, g, 1, 1),
                            logits.shape[:-1] + (1,))
    p = jax.nn.softmax(jnp.concatenate([logits, sink], axis=-1), axis=-1)[..., :-1]
    out = jnp.einsum('bnhgqk,bnkhd->bnqhgd', p.astype(v.dtype), vv)
    return out.reshape(b, s, C_HEADS * HEAD_DIM)


def _even_mixer(h, w_in, qn_a, kn_a, lam_q1, lam_k1, lam_q2, lam_k2, subln_a, qn_b, kn_b, layer_idx):
    b, s, _ = h.shape
    aq, ak, av, bq, bk, bv, iq, ik, iw = _split_cols(h @ w_in, EVEN_SIZES)
    aq = _rms_norm(aq.reshape(b, s, A_HEADS, 2, HEAD_DIM), qn_a)
    ak = _rms_norm(ak.reshape(b, s, A_HEADS, 2, HEAD_DIM), kn_a)
    av = av.reshape(b, s, A_HEADS, 2 * HEAD_DIM)
    lam_init = 0.8 - 0.6 * math.exp(-0.3 * layer_idx)
    f32 = jnp.float32
    lam = (jnp.exp(jnp.sum(lam_q1.astype(f32) * lam_k1.astype(f32)))
           - jnp.exp(jnp.sum(lam_q2.astype(f32) * lam_k2.astype(f32))) + lam_init)
    ya = _diff_attention(aq, ak, av, lam, _alibi_slopes(A_HEADS))
    ya = _rms_norm(ya, subln_a) * (1.0 - lam_init)
    bq = _rms_norm(bq.reshape(b, s, B_HEADS, HEAD_DIM), qn_b)
    bk = _rms_norm(bk, kn_b)
    iq = iq.reshape(b, s, IDX_HEADS, IDX_DIM)
    iw = iw * IDX_HEADS ** -0.5
    topk = min(TOPK_MAX, s // 4)
    yb = _dsa_attention(bq, bk, bv, iq, ik, iw, _alibi_slopes(B_HEADS), topk)
    return jnp.concatenate([ya.reshape(b, s, -1), yb.reshape(b, s, -1)], axis=-1)


def _odd_mixer(h, w_in, qn_c, kn_c, sinks):
    b, s, _ = h.shape
    q, k, v = _split_cols(h @ w_in, ODD_SIZES)
    q = _rms_norm(q.reshape(b, s, C_HEADS, HEAD_DIM), qn_c)
    k = _rms_norm(k.reshape(b, s, C_KV_HEADS, HEAD_DIM), kn_c)
    v = v.reshape(b, s, C_KV_HEADS, HEAD_DIM)
    return _swa_sink_attention(q, k, v, sinks, _alibi_slopes(C_HEADS))


def _swiglu(h, wg, wu, wd):
    return (jax.nn.silu(h @ wg) * (h @ wu)) @ wd


def setup_inputs(seed: int = 0) -> dict:
    key = jax.random.key(seed)
    ks = iter(list(jax.random.split(key, 32)))
    d, f = D_MODEL, FFN_HIDDEN

    def nrm(shape, scale):
        return jax.random.normal(next(ks), shape, jnp.float32) * scale

    def gain(shape):
        return 1.0 + nrm(shape, 0.02)

    return {
        'x': nrm((BATCH, SEQ, d), 1.0),
        'c': nrm((BATCH, d), 1.0),
        'ada_w': nrm((DEPTH, d, 6 * d), 0.5 * d ** -0.5),
        'ada_b': nrm((DEPTH, 6 * d), 0.02),
        'norm_mix': gain((DEPTH, d)),
        'norm_ffn': gain((DEPTH, d)),
        'w_out': nrm((DEPTH, d, d), d ** -0.5),
        'ffn_gate': nrm((DEPTH, d, f), d ** -0.5),
        'ffn_up': nrm((DEPTH, d, f), d ** -0.5),
        'ffn_down': nrm((DEPTH, f, d), f ** -0.5),
        'w_in_even': nrm((N_EVEN, d, EVEN_COLS), d ** -0.5),
        'qn_a': gain((N_EVEN, HEAD_DIM)),
        'kn_a': gain((N_EVEN, HEAD_DIM)),
        'lam_q1': nrm((N_EVEN, HEAD_DIM), 0.1),
        'lam_k1': nrm((N_EVEN, HEAD_DIM), 0.1),
        'lam_q2': nrm((N_EVEN, HEAD_DIM), 0.1),
        'lam_k2': nrm((N_EVEN, HEAD_DIM), 0.1),
        'subln_a': gain((N_EVEN, 2 * HEAD_DIM)),
        'qn_b': gain((N_EVEN, HEAD_DIM)),
        'kn_b': gain((N_EVEN, HEAD_DIM)),
        'w_in_odd': nrm((N_ODD, d, ODD_COLS), d ** -0.5),
        'qn_c': gain((N_ODD, HEAD_DIM)),
        'kn_c': gain((N_ODD, HEAD_DIM)),
        'sinks_c': nrm((N_ODD, C_HEADS), 0.5),
    }


def reference(x, c, ada_w, ada_b, norm_mix, norm_ffn, w_out, ffn_gate, ffn_up, ffn_down,
              w_in_even, qn_a, kn_a, lam_q1, lam_k1, lam_q2, lam_k2, subln_a, qn_b, kn_b,
              w_in_odd, qn_c, kn_c, sinks_c):
    cond = jax.nn.silu(c)
    for l in range(DEPTH):
        mod = cond @ ada_w[l] + ada_b[l]
        sh1, sc1, g1, sh2, sc2, g2 = [m[:, None, :] for m in jnp.split(mod, 6, axis=-1)]
        h = _rms_norm(x, norm_mix[l]) * (1.0 + sc1) + sh1
        if l % 2 == 0:
            e = l // 2
            y = _even_mixer(h, w_in_even[e], qn_a[e], kn_a[e], lam_q1[e], lam_k1[e],
                            lam_q2[e], lam_k2[e], subln_a[e], qn_b[e], kn_b[e], l)
        else:
            o = l // 2
            y = _odd_mixer(h, w_in_odd[o], qn_c[o], kn_c[o], sinks_c[o])
        x = x + g1 * (y @ w_out[l])
        h = _rms_norm(x, norm_ffn[l]) * (1.0 + sc2) + sh2
        x = x + g2 * _swiglu(h, ffn_gate[l], ffn_up[l], ffn_down[l])
    return x
```

```python
import functools
import math

import jax
import jax.numpy as jnp
from jax import lax
from jax.experimental import pallas as pl
from jax.experimental.pallas import tpu as pltpu

HEAD_DIM = 64
LANES = 128
NORM_EPS = 1e-6
TOPK_MAX = 256
WINDOW = 128
NEG_BIG = -1e30
VMEM_LIMIT = 56 * 1024 * 1024

F32 = jnp.float32
BF16 = jnp.bfloat16
_NT = (((1,), (1,)), ((), ()))


def _alibi_slopes(n):
    return [2.0 ** (-8.0 * (i + 1) / n) for i in range(n)]


def _tile_lanes(x, n):
    return x if n == 1 else jnp.concatenate([x] * n, axis=1)


def _ada_kernel(c_ref, w_ref, b_ref, o_ref):
    c = c_ref[...]
    cond = c * jax.nn.sigmoid(c)
    o_ref[0] = jnp.dot(cond, w_ref[0], preferred_element_type=F32,
                       precision=lax.Precision.HIGHEST) + b_ref[0]


def _ada_mod(c, ada_w, ada_b):
    depth, d, n = ada_w.shape
    b = c.shape[0]
    rows = 8
    tn = 1536
    c_pad = jnp.zeros((rows, d), F32).at[:b].set(c)
    out = pl.pallas_call(
        _ada_kernel,
        grid=(depth, n // tn),
        in_specs=[pl.BlockSpec((rows, d), lambda l, j: (0, 0)),
                  pl.BlockSpec((1, d, tn), lambda l, j: (l, 0, j)),
                  pl.BlockSpec((1, 1, tn), lambda l, j: (l, 0, j))],
        out_specs=pl.BlockSpec((1, rows, tn), lambda l, j: (l, 0, j)),
        out_shape=jax.ShapeDtypeStruct((depth, rows, n), F32),
        compiler_params=pltpu.CompilerParams(
            dimension_semantics=("arbitrary", "arbitrary"), vmem_limit_bytes=VMEM_LIMIT),
        name="ada_mod",
    )(c_pad, ada_w, ada_b.reshape(depth, 1, n))
    return out[:, :b]


def _modulated_norm(x, g, sc, sh):
    ms = jnp.mean(x * x, axis=-1, keepdims=True)
    return (x * lax.rsqrt(ms + NORM_EPS) * g) * (1.0 + sc) + sh


def _proj_kernel(x_ref, g_ref, sc_ref, sh_ref, w_ref, bd_ref, gain_ref, *out_refs, groups):
    h = _modulated_norm(x_ref[0], g_ref[...], sc_ref[0], sh_ref[0]).astype(BF16)
    bd = bd_ref[...]
    for (c0, width, parts) in groups:
        y = jnp.dot(h, w_ref[:, c0:c0 + width], preferred_element_type=F32)
        for (p0, pw, oi, o0, mode) in parts:
            for s in range(0, pw, LANES):
                ys = y[:, p0 + s:p0 + s + LANES]
                col = c0 + p0 + s
                if mode == "norm":
                    sq = ys * ys
                    hi = sq.astype(BF16)
                    lo = (sq - hi.astype(F32)).astype(BF16)
                    ss = (jnp.dot(hi, bd, preferred_element_type=F32)
                          + jnp.dot(lo, bd, preferred_element_type=F32))
                    ys = ys * lax.rsqrt(ss * (1.0 / HEAD_DIM) + NORM_EPS)
                if mode != "plain":
                    ys = ys * gain_ref[:, col:col + LANES]
                out_refs[oi][0, :, o0 + s:o0 + s + LANES] = ys.astype(out_refs[oi].dtype)


def _proj(x, g, sc, sh, w, gain, groups, out_defs, tm):
    bsz, s, d = x.shape
    c = w.shape[1]
    r = lax.broadcasted_iota(jnp.int32, (LANES, LANES), 0) // HEAD_DIM
    cc = lax.broadcasted_iota(jnp.int32, (LANES, LANES), 1) // HEAD_DIM
    bd = (r == cc).astype(BF16)
    const = lambda b, i: (0, 0)
    return pl.pallas_call(
        functools.partial(_proj_kernel, groups=groups),
        grid=(bsz, s // tm),
        in_specs=[pl.BlockSpec((1, tm, d), lambda b, i: (b, i, 0)),
                  pl.BlockSpec((1, d), const),
                  pl.BlockSpec((1, 1, d), lambda b, i: (b, 0, 0)),
                  pl.BlockSpec((1, 1, d), lambda b, i: (b, 0, 0)),
                  pl.BlockSpec((d, c), const),
                  pl.BlockSpec((LANES, LANES), const),
                  pl.BlockSpec((1, c), const)],
        out_specs=[pl.BlockSpec((1, tm, wd), lambda b, i: (b, i, 0)) for wd, _ in out_defs],
        out_shape=[jax.ShapeDtypeStruct((bsz, s, wd), dt) for wd, dt in out_defs],
        compiler_params=pltpu.CompilerParams(
            dimension_semantics=("arbitrary", "arbitrary"), vmem_limit_bytes=VMEM_LIMIT),
        name="in_proj",
    )(x, g.reshape(1, d), sc, sh, w, bd, gain)


def _diff_attn_kernel(slope_ref, q_ref, k_ref, v_ref, lam_ref, subln_ref, o_ref,
                      m_ref, l_ref, acc_ref, *, t, lam_init):
    h = pl.program_id(1)
    i = pl.program_id(2)
    slope = slope_ref[h]
    q = q_ref[0]
    lane = lax.broadcasted_iota(jnp.int32, q.shape, 1)
    zero = jnp.zeros_like(q)
    qs = jnp.concatenate([jnp.where(lane < HEAD_DIM, q, zero),
                          jnp.where(lane >= HEAD_DIM, q, zero)], axis=0)
    m_ref[...] = jnp.full(m_ref.shape, NEG_BIG, F32)
    l_ref[...] = jnp.zeros(l_ref.shape, F32)
    acc_ref[...] = jnp.zeros(acc_ref.shape, F32)
    col = lax.broadcasted_iota(jnp.int32, (1, t), 1)

    def step(j, masked):
        start = pl.multiple_of(j * t, t)
        kc = k_ref[0, pl.ds(start, t), :]
        vc = v_ref[0, pl.ds(start, t), :]
        s = lax.dot_general(qs, kc, _NT, preferred_element_type=F32)
        s = s + slope * ((j - i) * t + col).astype(F32)
        if masked:
            r = lax.broadcasted_iota(jnp.int32, (t, t), 0)
            c = lax.broadcasted_iota(jnp.int32, (t, t), 1)
            keep = r >= c
            keep = jnp.concatenate([keep, keep], axis=0)
            s = jnp.where(keep, s, NEG_BIG)
        m_prev = m_ref[...]
        m_next = jnp.maximum(m_prev, jnp.max(s, axis=1, keepdims=True))
        alpha = jnp.exp(m_prev - m_next)
        p = jnp.exp(s - _tile_lanes(m_next, t // LANES))
        l_ref[...] = alpha * l_ref[...] + jnp.sum(p, axis=1, keepdims=True)
        acc_ref[...] = alpha * acc_ref[...] + jnp.dot(p.astype(BF16), vc,
                                                      preferred_element_type=F32)
        m_ref[...] = m_next

    def body(j, carry):
        step(j, False)
        return carry

    lax.fori_loop(0, i, body, 0)
    step(i, True)

    o = acc_ref[...] / l_ref[...]
    lam_v = lam_ref[...]
    s1 = jnp.sum(lam_v[0:1] * lam_v[1:2], axis=-1, keepdims=True)
    s2 = jnp.sum(lam_v[2:3] * lam_v[3:4], axis=-1, keepdims=True)
    lam = jnp.exp(s1) - jnp.exp(s2) + lam_init
    y = o[:t] - lam * o[t:]
    ms = jnp.mean(y * y, axis=-1, keepdims=True)
    y = (y * lax.rsqrt(ms + NORM_EPS) * subln_ref[...]) * (1.0 - lam_init)
    o_ref[0] = y.astype(o_ref.dtype)


def _diff_attention(q, k, v, lam_vecs, subln, layer_idx, t):
    bsz, s, width = q.shape
    heads = width // LANES
    lam_init = 0.8 - 0.6 * math.exp(-0.3 * layer_idx)
    slopes = jnp.asarray(_alibi_slopes(heads), F32)
    return pl.pallas_call(
        functools.partial(_diff_attn_kernel, t=t, lam_init=lam_init),
        grid=(bsz, heads, s // t),
        in_specs=[pl.BlockSpec(memory_space=pltpu.SMEM),
                  pl.BlockSpec((1, t, LANES), lambda b, h, i: (b, i, h)),
                  pl.BlockSpec((1, s, LANES), lambda b, h, i: (b, 0, h)),
                  pl.BlockSpec((1, s, LANES), lambda b, h, i: (b, 0, h)),
                  pl.BlockSpec((4, HEAD_DIM), lambda b, h, i: (0, 0)),
                  pl.BlockSpec((1, LANES), lambda b, h, i: (0, 0))],
        out_specs=pl.BlockSpec((1, t, LANES), lambda b, h, i: (b, i, h)),
        out_shape=jax.ShapeDtypeStruct((bsz, s, width), BF16),
        scratch_shapes=[pltpu.VMEM((2 * t, LANES), F32),
                        pltpu.VMEM((2 * t, LANES), F32),
                        pltpu.VMEM((2 * t, LANES), F32)],
        compiler_params=pltpu.CompilerParams(
            dimension_semantics=("arbitrary", "arbitrary", "arbitrary"),
            vmem_limit_bytes=VMEM_LIMIT),
        name="diff_attn",
    )(slopes, q, k, v, lam_vecs, subln.reshape(1, LANES))


def _stack_heads(x, heads):
    lane = lax.broadcasted_iota(jnp.int32, (x.shape[0], LANES), 1)
    lo = lane < HEAD_DIM
    parts = []
    for h in range(heads):
        blk = x[:, (h // 2) * LANES:(h // 2 + 1) * LANES]
        keep = lo if h % 2 == 0 else jnp.logical_not(lo)
        parts.append(jnp.where(keep, blk, jnp.zeros_like(blk)))
    return jnp.concatenate(parts, axis=0)


def _dsa_kernel(bq_ref, iq_ref, iw_ref, ik_ref, bk_ref, bv_ref, o_ref,
                key_ref, bias_ref, m_ref, l_ref, acc_ref, *, tq, tk, heads, topk, slopes):
    i = pl.program_id(1)
    q0 = i * tq
    nch = q0 // tk + 1
    nsl = tk // LANES
    row = lax.broadcasted_iota(jnp.int32, (tq, tk), 0)
    col = lax.broadcasted_iota(jnp.int32, (tq, tk), 1)

    iqs = _stack_heads(iq_ref[0], heads)
    iw = iw_ref[0]
    wb = [jnp.broadcast_to(iw[:, h:h + 1], (tq, LANES)) for h in range(heads)]

    def score_body(j, carry):
        start = pl.multiple_of(j * tk, tk)
        logit = lax.dot_general(iqs, ik_ref[0, pl.ds(start, tk), :], _NT,
                                preferred_element_type=F32)
        sc = jnp.zeros((tq, tk), F32)
        for h in range(heads):
            sc = sc + jnp.maximum(logit[h * tq:(h + 1) * tq], 0.0) * _tile_lanes(wb[h], nsl)
        sc = sc + 0.0
        causal = (start + col) <= (q0 + row)
        sc = jnp.where(causal, sc, -jnp.inf)
        bits = lax.bitcast_convert_type(sc, jnp.int32)
        key_ref[j] = bits ^ ((bits >> 31) & jnp.int32(0x7FFFFFFF))
        return carry

    lax.fori_loop(0, nch, score_body, 0)

    def count(pred_fn, thr):
        def body(j, cnt):
            kc = key_ref[j]
            for s in range(nsl):
                cnt = cnt + jnp.where(pred_fn(kc[:, s * LANES:(s + 1) * LANES], thr), 1.0, 0.0)
            return cnt
        cnt = lax.fori_loop(0, nch, body, jnp.zeros((tq, LANES), F32))
        return jnp.broadcast_to(jnp.sum(cnt, axis=1, keepdims=True), (tq, LANES))

    ge = lambda a, b: a >= b
    gt = lambda a, b: a > b

    def bit_body(b, thr):
        cand = thr + lax.shift_left(jnp.int32(1), 31 - b)
        return jnp.where(count(ge, cand) >= float(topk), cand, thr)

    int_min = jnp.int32(-2 ** 31)
    thr = lax.fori_loop(0, 32, bit_body, jnp.full((tq, LANES), int_min, jnp.int32))
    n_ge = count(ge, thr)
    n_gt = count(gt, thr)
    need = float(topk) - n_gt
    has_excess_ties = jnp.max(jnp.where(n_ge > float(topk), 1.0, 0.0)) > 0.0
    thr_t = _tile_lanes(thr, nsl)

    @pl.when(jnp.logical_not(has_excess_ties))
    def _():
        def body(j, carry):
            start = pl.multiple_of(j * tk, tk)
            keep = (key_ref[j] >= thr_t) & ((start + col) <= (q0 + row))
            bias_ref[j] = jnp.where(keep, 0.0, NEG_BIG)
            return carry
        lax.fori_loop(0, nch, body, 0)

    @pl.when(has_excess_ties)
    def _():
        r2 = lax.broadcasted_iota(jnp.int32, (tk, tk), 0)
        c2 = lax.broadcasted_iota(jnp.int32, (tk, tk), 1)
        upper = jnp.where(r2 <= c2, 1.0, 0.0).astype(BF16)
        need_t = _tile_lanes(need, nsl)

        def body(j, seen):
            start = pl.multiple_of(j * tk, tk)
            kc = key_ref[j]
            eq = kc == thr_t
            eqf = jnp.where(eq, 1.0, 0.0)
            rank = _tile_lanes(seen, nsl) + jnp.dot(eqf.astype(BF16), upper,
                                                    preferred_element_type=F32)
            keep = ((kc > thr_t) | (eq & (rank <= need_t))) & ((start + col) <= (q0 + row))
            bias_ref[j] = jnp.where(keep, 0.0, NEG_BIG)
            return seen + jnp.broadcast_to(jnp.sum(eqf, axis=1, keepdims=True), (tq, LANES))

        lax.fori_loop(0, nch, body, jnp.zeros((tq, LANES), F32))

    bqs = _stack_heads(bq_ref[0], heads)
    m_ref[...] = jnp.full(m_ref.shape, NEG_BIG, F32)
    l_ref[...] = jnp.zeros(l_ref.shape, F32)
    acc_ref[...] = jnp.zeros(acc_ref.shape, F32)
    kcol = lax.broadcasted_iota(jnp.int32, (1, tk), 1)

    def attn_body(j, carry):
        start = pl.multiple_of(j * tk, tk)
        logit = lax.dot_general(bqs, bk_ref[0, pl.ds(start, tk), :], _NT,
                                preferred_element_type=F32)
        vc = bv_ref[0, pl.ds(start, tk), :]
        bias = bias_ref[j]
        rel = (start - q0 + kcol).astype(F32)
        for h in range(heads):
            rs = slice(h * tq, (h + 1) * tq)
            s = logit[rs] + bias + slopes[h] * rel
            m_prev = m_ref[rs]
            m_next = jnp.maximum(m_prev, jnp.max(s, axis=1, keepdims=True))
            alpha = jnp.exp(m_prev - m_next)
            p = jnp.exp(s - _tile_lanes(m_next, nsl))
            l_ref[rs] = alpha * l_ref[rs] + jnp.sum(p, axis=1, keepdims=True)
            acc_ref[rs] = alpha * acc_ref[rs] + jnp.dot(p.astype(BF16), vc,
                                                        preferred_element_type=F32)
            m_ref[rs] = m_next
        return carry

    lax.fori_loop(0, nch, attn_body, 0)

    lane = lax.broadcasted_iota(jnp.int32, (tq, LANES), 1)
    for g in range(heads // 2):
        ra = slice((2 * g) * tq, (2 * g + 1) * tq)
        rb = slice((2 * g + 1) * tq, (2 * g + 2) * tq)
        oa = acc_ref[ra] / l_ref[ra]
        ob = acc_ref[rb] / l_ref[rb]
        o_ref[0, :, g * LANES:(g + 1) * LANES] = jnp.where(lane < HEAD_DIM, oa, ob).astype(o_ref.dtype)


def _dsa_attention(bq, iq, iwp, ik2, bk2, bv2, tq, tk):
    bsz, s, width = bq.shape
    heads = width // HEAD_DIM
    topk = min(TOPK_MAX, s // 4)
    nck = s // tk
    kern = functools.partial(_dsa_kernel, tq=tq, tk=tk, heads=heads, topk=topk,
                             slopes=_alibi_slopes(heads))
    qspec = lambda w: pl.BlockSpec((1, tq, w), lambda b, i: (b, i, 0))
    kspec = pl.BlockSpec((1, s, LANES), lambda b, i: (b, 0, 0))
    return pl.pallas_call(
        kern,
        grid=(bsz, s // tq),
        in_specs=[qspec(width), qspec(width), qspec(LANES), kspec, kspec, kspec],
        out_specs=pl.BlockSpec((1, tq, width), lambda b, i: (b, i, 0)),
        out_shape=jax.ShapeDtypeStruct((bsz, s, width), BF16),
        scratch_shapes=[pltpu.VMEM((nck, tq, tk), jnp.int32),
                        pltpu.VMEM((nck, tq, tk), F32),
                        pltpu.VMEM((heads * tq, LANES), F32),
                        pltpu.VMEM((heads * tq, LANES), F32),
                        pltpu.VMEM((heads * tq, LANES), F32)],
        compiler_params=pltpu.CompilerParams(
            dimension_semantics=("arbitrary", "arbitrary"), vmem_limit_bytes=VMEM_LIMIT),
        name="dsa_attn",
    )(bq, iq, iwp, ik2, bk2, bv2)


def _swa_kernel(slope_ref, sink_ref, q_ref, kp_ref, kc_ref, vp_ref, vc_ref, o_ref, *, tq, group):
    kv = pl.program_id(1)
    i = pl.program_id(2)
    nk = WINDOW + tq
    qs = _stack_heads(q_ref[0], group)
    kk = jnp.concatenate([kp_ref[0], kc_ref[0]], axis=0)
    vv = jnp.concatenate([vp_ref[0], vc_ref[0]], axis=0)
    logit = lax.dot_general(qs, kk, _NT, preferred_element_type=F32)
    r = lax.broadcasted_iota(jnp.int32, (tq, nk), 0)
    c = lax.broadcasted_iota(jnp.int32, (tq, nk), 1)
    dist = WINDOW + r - c
    valid = (dist >= 0) & (dist < WINDOW) & ((i * tq - WINDOW + c) >= 0)
    distf = dist.astype(F32)
    lane = lax.broadcasted_iota(jnp.int32, (tq, LANES), 1)
    outs = []
    for g in range(group):
        hq = kv * group + g
        s = logit[g * tq:(g + 1) * tq] - slope_ref[hq] * distf
        s = jnp.where(valid, s, NEG_BIG)
        sink = sink_ref[hq]
        m = jnp.maximum(jnp.max(s, axis=1, keepdims=True), sink)
        p = jnp.exp(s - m)
        denom = jnp.sum(p, axis=1, keepdims=True) + jnp.exp(sink - m)
        o = jnp.dot(p.astype(BF16), vv, preferred_element_type=F32) / denom
        outs.append(o)
    for g in range(group // 2):
        o_ref[0, :, g * LANES:(g + 1) * LANES] = jnp.where(
            lane < HEAD_DIM, outs[2 * g], outs[2 * g + 1]).astype(o_ref.dtype)


def _swa_attention(q, k2, v2, sinks, tq):
    bsz, s, width = q.shape
    heads = width // HEAD_DIM
    kvh = k2.shape[2] // LANES
    group = heads // kvh
    slopes = jnp.asarray(_alibi_slopes(heads), F32)
    r = tq // WINDOW
    prev = lambda b, kv, i: (b, jnp.maximum(i * r - 1, 0), kv)
    cur = lambda b, kv, i: (b, i, kv)
    smem = pl.BlockSpec(memory_space=pltpu.SMEM)
    return pl.pallas_call(
        functools.partial(_swa_kernel, tq=tq, group=group),
        grid=(bsz, kvh, s // tq),
        in_specs=[smem, smem,
                  pl.BlockSpec((1, tq, group * HEAD_DIM), cur),
                  pl.BlockSpec((1, WINDOW, LANES), prev),
                  pl.BlockSpec((1, tq, LANES), cur),
                  pl.BlockSpec((1, WINDOW, LANES), prev),
                  pl.BlockSpec((1, tq, LANES), cur)],
        out_specs=pl.BlockSpec((1, tq, group * HEAD_DIM), cur),
        out_shape=jax.ShapeDtypeStruct((bsz, s, width), BF16),
        compiler_params=pltpu.CompilerParams(
            dimension_semantics=("arbitrary", "arbitrary", "arbitrary"),
            vmem_limit_bytes=VMEM_LIMIT),
        name="swa_attn",
    )(slopes, sinks.astype(F32), q, k2, k2, v2, v2)


def _out_ffn_kernel(*refs, ny):
    x_ref = refs[0]
    y_refs = refs[1:1 + ny]
    wo_ref, g1_ref, n_ref, sc_ref, sh_ref, g2_ref, wg_ref, wu_ref, wd_ref, o_ref = refs[1 + ny:]
    mix = None
    r0 = 0
    for y_ref in y_refs:
        w = y_ref.shape[2]
        part = jnp.dot(y_ref[0], wo_ref[r0:r0 + w, :], preferred_element_type=F32)
        mix = part if mix is None else mix + part
        r0 += w
    x = x_ref[0] + g1_ref[0] * mix
    h = _modulated_norm(x, n_ref[...], sc_ref[0], sh_ref[0]).astype(BF16)
    gate = jnp.dot(h, wg_ref[...], preferred_element_type=F32)
    up = jnp.dot(h, wu_ref[...], preferred_element_type=F32)
    act = (gate * jax.nn.sigmoid(gate) * up).astype(BF16)
    o_ref[0] = x + g2_ref[0] * jnp.dot(act, wd_ref[...], preferred_element_type=F32)


def _out_ffn(x, ys, wo, g1, n, sc, sh, g2, wg, wu, wd, tm):
    bsz, s, d = x.shape
    f = wg.shape[1]
    const = lambda b, i: (0, 0)
    tok = lambda b, i: (b, i, 0)
    mod = lambda b, i: (b, 0, 0)
    once = pl.Buffered(1)
    return pl.pallas_call(
        functools.partial(_out_ffn_kernel, ny=len(ys)),
        grid=(bsz, s // tm),
        in_specs=[pl.BlockSpec((1, tm, d), tok)]
                 + [pl.BlockSpec((1, tm, y.shape[2]), tok) for y in ys]
                 + [pl.BlockSpec((d, d), const, pipeline_mode=once),
                  pl.BlockSpec((1, 1, d), mod),
                  pl.BlockSpec((1, d), const),
                  pl.BlockSpec((1, 1, d), mod),
                  pl.BlockSpec((1, 1, d), mod),
                  pl.BlockSpec((1, 1, d), mod),
                  pl.BlockSpec((d, f), const, pipeline_mode=once),
                  pl.BlockSpec((d, f), const, pipeline_mode=once),
                  pl.BlockSpec((f, d), const, pipeline_mode=once)],
        out_specs=pl.BlockSpec((1, tm, d), tok),
        out_shape=jax.ShapeDtypeStruct((bsz, s, d), F32),
        compiler_params=pltpu.CompilerParams(
            dimension_semantics=("arbitrary", "arbitrary"), vmem_limit_bytes=VMEM_LIMIT),
        name="out_ffn",
    )(x, *ys, wo, g1, n.reshape(1, d), sc, sh, g2, wg, wu, wd)


def _dup(w):
    d, c = w.shape
    w = w.reshape(d, c // HEAD_DIM, 1, HEAD_DIM)
    return jnp.broadcast_to(w, (d, c // HEAD_DIM, 2, HEAD_DIM)).reshape(d, 2 * c)


def _even_layout(w_in, qn_a, kn_a, qn_b, kn_b, d):
    a = d // 2
    sizes = (a, a, a, a, HEAD_DIM, HEAD_DIM, a, HEAD_DIM, a // HEAD_DIM)
    offs = [0]
    for sz in sizes:
        offs.append(offs[-1] + sz)
    aq, ak, av, bq, bk, bv, iq, ik, iw = [w_in[:, offs[n]:offs[n + 1]] for n in range(9)]
    nh = a // HEAD_DIM
    iw_pad = jnp.zeros((d, LANES), w_in.dtype).at[:, :nh].set(iw)
    w = jnp.concatenate([aq, ak, av, bq, iq, _dup(bk), _dup(bv), _dup(ik), iw_pad], axis=1)
    scale = HEAD_DIM ** -0.5
    ones = lambda n: jnp.ones((n,), F32)
    gain = jnp.concatenate([
        jnp.tile(qn_a, nh) * scale, jnp.tile(kn_a, nh), ones(a),
        jnp.tile(qn_b, nh) * scale, ones(a) * scale,
        jnp.tile(kn_b, 2), ones(LANES), ones(LANES), ones(LANES) * nh ** -0.5])
    groups = (
        (0, a, ((0, a, 0, 0, "norm"),)),
        (a, a, ((0, a, 1, 0, "norm"),)),
        (2 * a, a, ((0, a, 2, 0, "plain"),)),
        (3 * a, a, ((0, a, 3, 0, "norm"),)),
        (4 * a, a, ((0, a, 4, 0, "scale"),)),
        (5 * a, 4 * LANES, ((0, LANES, 5, 0, "norm"), (LANES, LANES, 6, 0, "plain"),
                            (2 * LANES, LANES, 7, 0, "plain"), (3 * LANES, LANES, 8, 0, "scale"))),
    )
    out_defs = ((a, BF16), (a, BF16), (a, BF16), (a, BF16), (a, BF16),
                (LANES, BF16), (LANES, BF16), (LANES, BF16), (LANES, F32))
    return w.astype(BF16), gain.reshape(1, -1), groups, out_defs


def _odd_layout(w_in, qn_c, kn_c, d):
    kvw = d // 4
    q, k, v = w_in[:, :d], w_in[:, d:d + kvw], w_in[:, d + kvw:]
    w = jnp.concatenate([q, _dup(k), _dup(v)], axis=1)
    nh = d // HEAD_DIM
    gain = jnp.concatenate([jnp.tile(qn_c, nh) * HEAD_DIM ** -0.5,
                            jnp.tile(kn_c, 2 * kvw // HEAD_DIM), jnp.ones((2 * kvw,), F32)])
    half = d // 2
    groups = (
        (0, half, ((0, half, 0, 0, "norm"),)),
        (half, half, ((0, half, 0, half, "norm"),)),
        (d, 2 * kvw, ((0, 2 * kvw, 1, 0, "norm"),)),
        (d + 2 * kvw, 2 * kvw, ((0, 2 * kvw, 2, 0, "plain"),)),
    )
    out_defs = ((d, BF16), (2 * kvw, BF16), (2 * kvw, BF16))
    return w.astype(BF16), gain.reshape(1, -1), groups, out_defs


def kernel(x, c, ada_w, ada_b, norm_mix, norm_ffn, w_out, ffn_gate, ffn_up, ffn_down,
           w_in_even, qn_a, kn_a, lam_q1, lam_k1, lam_q2, lam_k2, subln_a, qn_b, kn_b,
           w_in_odd, qn_c, kn_c, sinks_c):
    depth, d = norm_mix.shape
    mod = _ada_mod(c, ada_w, ada_b)
    for l in range(depth):
        sh1, sc1, g1, sh2, sc2, g2 = [mod[l, :, None, n * d:(n + 1) * d] for n in range(6)]
        if l % 2 == 0:
            e = l // 2
            w, gain, groups, out_defs = _even_layout(w_in_even[e], qn_a[e], kn_a[e],
                                                     qn_b[e], kn_b[e], d)
            aq, ak, av, bq, iq, bk2, bv2, ik2, iwp = _proj(
                x, norm_mix[l], sc1, sh1, w, gain, groups, out_defs, tm=256)
            lam_vecs = jnp.stack([lam_q1[e], lam_k1[e], lam_q2[e], lam_k2[e]]).astype(F32)
            ya = _diff_attention(aq, ak, av, lam_vecs, subln_a[e], l, t=256)
            yb = _dsa_attention(bq, iq, iwp, ik2, bk2, bv2, tq=128, tk=512)
            ys = (ya, yb)
        else:
            o = l // 2
            w, gain, groups, out_defs = _odd_layout(w_in_odd[o], qn_c[o], kn_c[o], d)
            q, k2, v2 = _proj(x, norm_mix[l], sc1, sh1, w, gain, groups, out_defs, tm=256)
            ys = (_swa_attention(q, k2, v2, sinks_c[o], tq=256),)
        x = _out_ffn(x, ys, w_out[l].astype(BF16), g1, norm_ffn[l], sc2, sh2, g2,
                     ffn_gate[l].astype(BF16), ffn_up[l].astype(BF16),
                     ffn_down[l].astype(BF16), tm=256)
    return x
```

```python
import functools
import math

import jax
import jax.numpy as jnp
from jax import lax
from jax.experimental import pallas as pl
from jax.experimental.pallas import tpu as pltpu

HEAD_DIM = 64
LANES = 128
NORM_EPS = 1e-6
TOPK_MAX = 256
WINDOW = 128
NEG_BIG = -1e30
LOG2E = 1.4426950408889634
KEY_NEG_INF = -2139095041
VMEM_LIMIT = 56 * 1024 * 1024

F32 = jnp.float32
BF16 = jnp.bfloat16
_NT = (((1,), (1,)), ((), ()))


def _alibi_slopes(n):
    return [2.0 ** (-8.0 * (i + 1) / n) for i in range(n)]


def _tile_lanes(x, n):
    return x if n == 1 else jnp.concatenate([x] * n, axis=1)


def _ada_kernel(c_ref, w_ref, b_ref, o_ref):
    c = c_ref[...]
    cond = c * jax.nn.sigmoid(c)
    o_ref[0] = jnp.dot(cond, w_ref[0], preferred_element_type=F32,
                       precision=lax.Precision.HIGHEST) + b_ref[0]


def _ada_mod(c, ada_w, ada_b):
    depth, d, n = ada_w.shape
    b = c.shape[0]
    rows = 8
    tn = 1536
    c_pad = jnp.zeros((rows, d), F32).at[:b].set(c)
    out = pl.pallas_call(
        _ada_kernel,
        grid=(depth, n // tn),
        in_specs=[pl.BlockSpec((rows, d), lambda l, j: (0, 0)),
                  pl.BlockSpec((1, d, tn), lambda l, j: (l, 0, j)),
                  pl.BlockSpec((1, 1, tn), lambda l, j: (l, 0, j))],
        out_specs=pl.BlockSpec((1, rows, tn), lambda l, j: (l, 0, j)),
        out_shape=jax.ShapeDtypeStruct((depth, rows, n), F32),
        compiler_params=pltpu.CompilerParams(
            dimension_semantics=("arbitrary", "arbitrary"), vmem_limit_bytes=VMEM_LIMIT),
        name="ada_mod",
    )(c_pad, ada_w, ada_b.reshape(depth, 1, n))
    return out[:, :b]


def _modulated_norm(x, g, sc, sh):
    ms = jnp.mean(x * x, axis=-1, keepdims=True)
    return (x * lax.rsqrt(ms + NORM_EPS) * g) * (1.0 + sc) + sh


def _proj_kernel(x_ref, g_ref, sc_ref, sh_ref, w_ref, bd_ref, gain_ref, *out_refs, groups):
    h = _modulated_norm(x_ref[0], g_ref[...], sc_ref[0], sh_ref[0]).astype(BF16)
    bd = bd_ref[...]
    for (c0, width, parts) in groups:
        y = jnp.dot(h, w_ref[:, c0:c0 + width], preferred_element_type=F32)
        for (p0, pw, oi, o0, mode) in parts:
            for s in range(0, pw, LANES):
                ys = y[:, p0 + s:p0 + s + LANES]
                col = c0 + p0 + s
                if mode == "norm":
                    sq = ys * ys
                    hi = sq.astype(BF16)
                    lo = (sq - hi.astype(F32)).astype(BF16)
                    ss = (jnp.dot(hi, bd, preferred_element_type=F32)
                          + jnp.dot(lo, bd, preferred_element_type=F32))
                    ys = ys * lax.rsqrt(ss * (1.0 / HEAD_DIM) + NORM_EPS)
                if mode != "plain":
                    ys = ys * gain_ref[:, col:col + LANES]
                out_refs[oi][0, :, o0 + s:o0 + s + LANES] = ys.astype(out_refs[oi].dtype)


def _proj(x, g, sc, sh, w, gain, groups, out_defs, tm):
    bsz, s, d = x.shape
    c = w.shape[1]
    r = lax.broadcasted_iota(jnp.int32, (LANES, LANES), 0) // HEAD_DIM
    cc = lax.broadcasted_iota(jnp.int32, (LANES, LANES), 1) // HEAD_DIM
    bd = (r == cc).astype(BF16)
    const = lambda b, i: (0, 0)
    return pl.pallas_call(
        functools.partial(_proj_kernel, groups=groups),
        grid=(bsz, s // tm),
        in_specs=[pl.BlockSpec((1, tm, d), lambda b, i: (b, i, 0)),
                  pl.BlockSpec((1, d), const),
                  pl.BlockSpec((1, 1, d), lambda b, i: (b, 0, 0)),
                  pl.BlockSpec((1, 1, d), lambda b, i: (b, 0, 0)),
                  pl.BlockSpec((d, c), const),
                  pl.BlockSpec((LANES, LANES), const),
                  pl.BlockSpec((1, c), const)],
        out_specs=[pl.BlockSpec((1, tm, wd), lambda b, i: (b, i, 0)) for wd, _ in out_defs],
        out_shape=[jax.ShapeDtypeStruct((bsz, s, wd), dt) for wd, dt in out_defs],
        compiler_params=pltpu.CompilerParams(
            dimension_semantics=("arbitrary", "arbitrary"), vmem_limit_bytes=VMEM_LIMIT),
        name="in_proj",
    )(x, g.reshape(1, d), sc, sh, w, bd, gain)


def _diff_attn_kernel(q_ref, k_ref, v_ref, lam_ref, subln_ref, o_ref,
                      qs_ref, m_ref, l_ref, acc_ref, *, t, heads, slopes, lam_init):
    i = pl.program_id(1)
    lane = lax.broadcasted_iota(jnp.int32, (t, LANES), 1)
    for h in range(heads):
        q = q_ref[0, :, h * LANES:(h + 1) * LANES]
        zero = jnp.zeros_like(q)
        qs_ref[h, :t] = jnp.where(lane < HEAD_DIM, q, zero)
        qs_ref[h, t:] = jnp.where(lane >= HEAD_DIM, q, zero)
    m_ref[...] = jnp.full(m_ref.shape, NEG_BIG, F32)
    l_ref[...] = jnp.zeros(l_ref.shape, F32)
    acc_ref[...] = jnp.zeros(acc_ref.shape, F32)

    def step(start, width, masked):
        col = lax.broadcasted_iota(jnp.int32, (1, width), 1)
        rel = (start - i * t + col).astype(F32)
        if masked:
            r = lax.broadcasted_iota(jnp.int32, (t, width), 0)
            c = lax.broadcasted_iota(jnp.int32, (t, width), 1)
            keep = r >= c
            keep = jnp.concatenate([keep, keep], axis=0)
        for h in range(heads):
            kc = k_ref[0, pl.ds(start, width), h * LANES:(h + 1) * LANES]
            vc = v_ref[0, pl.ds(start, width), h * LANES:(h + 1) * LANES]
            s = lax.dot_general(qs_ref[h], kc, _NT, preferred_element_type=F32)
            s = s + (slopes[h] * LOG2E) * rel
            if masked:
                s = jnp.where(keep, s, NEG_BIG)
            m_prev = m_ref[h]
            m_next = jnp.maximum(m_prev, jnp.max(s, axis=1, keepdims=True))
            alpha = jnp.exp2(m_prev - m_next)
            p = jnp.exp2(s - _tile_lanes(m_next, width // LANES))
            l_ref[h] = alpha * l_ref[h] + jnp.sum(p, axis=1, keepdims=True)
            acc_ref[h] = alpha * acc_ref[h] + jnp.dot(p.astype(BF16), vc,
                                                      preferred_element_type=F32)
            m_ref[h] = m_next

    def body(j, carry):
        step(pl.multiple_of(j * (2 * t), 2 * t), 2 * t, False)
        return carry

    lax.fori_loop(0, i // 2, body, 0)

    @pl.when(i % 2 == 1)
    def _():
        step(pl.multiple_of((i - 1) * t, t), t, False)

    step(pl.multiple_of(i * t, t), t, True)

    lam_v = lam_ref[...]
    s1 = jnp.sum(lam_v[0:1] * lam_v[1:2], axis=-1, keepdims=True)
    s2 = jnp.sum(lam_v[2:3] * lam_v[3:4], axis=-1, keepdims=True)
    lam = jnp.exp(s1) - jnp.exp(s2) + lam_init
    for h in range(heads):
        o = acc_ref[h] / l_ref[h]
        y = o[:t] - lam * o[t:]
        ms = jnp.mean(y * y, axis=-1, keepdims=True)
        y = (y * lax.rsqrt(ms + NORM_EPS) * subln_ref[...]) * (1.0 - lam_init)
        o_ref[0, :, h * LANES:(h + 1) * LANES] = y.astype(o_ref.dtype)


def _diff_attention(q, k, v, lam_vecs, subln, layer_idx, t):
    bsz, s, width = q.shape
    heads = width // LANES
    lam_init = 0.8 - 0.6 * math.exp(-0.3 * layer_idx)
    kern = functools.partial(_diff_attn_kernel, t=t, heads=heads,
                             slopes=_alibi_slopes(heads), lam_init=lam_init)
    return pl.pallas_call(
        kern,
        grid=(bsz, s // t),
        in_specs=[pl.BlockSpec((1, t, width), lambda b, i: (b, i, 0)),
                  pl.BlockSpec((1, s, width), lambda b, i: (b, 0, 0)),
                  pl.BlockSpec((1, s, width), lambda b, i: (b, 0, 0)),
                  pl.BlockSpec((4, HEAD_DIM), lambda b, i: (0, 0)),
                  pl.BlockSpec((1, LANES), lambda b, i: (0, 0))],
        out_specs=pl.BlockSpec((1, t, width), lambda b, i: (b, i, 0)),
        out_shape=jax.ShapeDtypeStruct((bsz, s, width), BF16),
        scratch_shapes=[pltpu.VMEM((heads, 2 * t, LANES), BF16),
                        pltpu.VMEM((heads, 2 * t, LANES), F32),
                        pltpu.VMEM((heads, 2 * t, LANES), F32),
                        pltpu.VMEM((heads, 2 * t, LANES), F32)],
        compiler_params=pltpu.CompilerParams(
            dimension_semantics=("arbitrary", "arbitrary"), vmem_limit_bytes=VMEM_LIMIT),
        name="diff_attn",
    )(q, k, v, lam_vecs, subln.reshape(1, LANES))


def _stack_heads(x, heads):
    lane = lax.broadcasted_iota(jnp.int32, (x.shape[0], LANES), 1)
    lo = lane < HEAD_DIM
    parts = []
    for h in range(heads):
        blk = x[:, (h // 2) * LANES:(h // 2 + 1) * LANES]
        keep = lo if h % 2 == 0 else jnp.logical_not(lo)
        parts.append(jnp.where(keep, blk, jnp.zeros_like(blk)))
    return jnp.concatenate(parts, axis=0)


def _key_to_float(k):
    return lax.bitcast_convert_type(k ^ ((k >> 31) & jnp.int32(0x7FFFFFFF)), F32)


def _dsa_kernel(bq_ref, iq_ref, iw_ref, ik_ref, bk_ref, bv_ref, o_ref,
                sc_ref, bias_ref, m_ref, l_ref, acc_ref, *, tq, tk, heads, topk, slopes):
    i = pl.program_id(1)
    q0 = i * tq
    nch = q0 // tk + 1
    nsl = tk // LANES
    row = lax.broadcasted_iota(jnp.int32, (tq, tk), 0)
    col = lax.broadcasted_iota(jnp.int32, (tq, tk), 1)

    iqs = _stack_heads(iq_ref[0], heads)
    iw = iw_ref[0]
    wb = [jnp.broadcast_to(iw[:, h:h + 1], (tq, LANES)) for h in range(heads)]

    def score_body(j, carry):
        start = pl.multiple_of(j * tk, tk)
        logit = lax.dot_general(iqs, ik_ref[0, pl.ds(start, tk), :], _NT,
                                preferred_element_type=F32)
        sc = jnp.zeros((tq, tk), F32)
        for h in range(heads):
            sc = sc + jnp.maximum(logit[h * tq:(h + 1) * tq], 0.0) * _tile_lanes(wb[h], nsl)
        sc = sc + 0.0
        causal = (start + col) <= (q0 + row)
        sc_ref[j] = jnp.where(causal, sc, -jnp.inf)
        return carry

    lax.fori_loop(0, nch, score_body, 0)

    def count(pred_fn, thr):
        def body(j, cnt):
            sc = sc_ref[j]
            for s in range(nsl):
                cnt = cnt + jnp.where(pred_fn(sc[:, s * LANES:(s + 1) * LANES], thr), 1.0, 0.0)
            return cnt
        cnt = lax.fori_loop(0, nch, body, jnp.zeros((tq, LANES), F32))
        return jnp.broadcast_to(jnp.sum(cnt, axis=1, keepdims=True), (tq, LANES))

    ge = lambda a, b: a >= b
    gt = lambda a, b: a > b

    def bit_body(b, key):
        cand = key + lax.shift_left(jnp.int32(1), 31 - b)
        return jnp.where(count(ge, _key_to_float(cand)) >= float(topk), cand, key)

    int_min = jnp.int32(-2 ** 31)
    key = lax.fori_loop(0, 32, bit_body, jnp.full((tq, LANES), int_min, jnp.int32))
    thr = jnp.where(key <= KEY_NEG_INF, -jnp.inf, _key_to_float(key))
    n_ge = count(ge, thr)
    n_gt = count(gt, thr)
    need = float(topk) - n_gt
    has_excess_ties = jnp.max(jnp.where(n_ge > float(topk), 1.0, 0.0)) > 0.0
    thr_t = _tile_lanes(thr, nsl)

    @pl.when(jnp.logical_not(has_excess_ties))
    def _():
        def body(j, carry):
            start = pl.multiple_of(j * tk, tk)
            keep = (sc_ref[j] >= thr_t) & ((start + col) <= (q0 + row))
            bias_ref[j] = jnp.where(keep, 0.0, NEG_BIG)
            return carry
        lax.fori_loop(0, nch, body, 0)

    @pl.when(has_excess_ties)
    def _():
        r2 = lax.broadcasted_iota(jnp.int32, (tk, tk), 0)
        c2 = lax.broadcasted_iota(jnp.int32, (tk, tk), 1)
        upper = jnp.where(r2 <= c2, 1.0, 0.0).astype(BF16)
        need_t = _tile_lanes(need, nsl)

        def body(j, seen):
            start = pl.multiple_of(j * tk, tk)
            sc = sc_ref[j]
            eq = sc == thr_t
            eqf = jnp.where(eq, 1.0, 0.0)
            rank = _tile_lanes(seen, nsl) + jnp.dot(eqf.astype(BF16), upper,
                                                    preferred_element_type=F32)
            keep = ((sc > thr_t) | (eq & (rank <= need_t))) & ((start + col) <= (q0 + row))
            bias_ref[j] = jnp.where(keep, 0.0, NEG_BIG)
            return seen + jnp.broadcast_to(jnp.sum(eqf, axis=1, keepdims=True), (tq, LANES))

        lax.fori_loop(0, nch, body, jnp.zeros((tq, LANES), F32))

    bqs = _stack_heads(bq_ref[0], heads)
    m_ref[...] = jnp.full(m_ref.shape, NEG_BIG, F32)
    l_ref[...] = jnp.zeros(l_ref.shape, F32)
    acc_ref[...] = jnp.zeros(acc_ref.shape, F32)
    kcol = lax.broadcasted_iota(jnp.int32, (1, tk), 1)

    def attn_body(j, carry):
        start = pl.multiple_of(j * tk, tk)
        logit = lax.dot_general(bqs, bk_ref[0, pl.ds(start, tk), :], _NT,
                                preferred_element_type=F32)
        vc = bv_ref[0, pl.ds(start, tk), :]
        bias = bias_ref[j]
        rel = (start - q0 + kcol).astype(F32)
        for h in range(heads):
            rs = slice(h * tq, (h + 1) * tq)
            s = logit[rs] + bias + (slopes[h] * LOG2E) * rel
            m_prev = m_ref[rs]
            m_next = jnp.maximum(m_prev, jnp.max(s, axis=1, keepdims=True))
            alpha = jnp.exp2(m_prev - m_next)
            p = jnp.exp2(s - _tile_lanes(m_next, nsl))
            l_ref[rs] = alpha * l_ref[rs] + jnp.sum(p, axis=1, keepdims=True)
            acc_ref[rs] = alpha * acc_ref[rs] + jnp.dot(p.astype(BF16), vc,
                                                        preferred_element_type=F32)
            m_ref[rs] = m_next
        return carry

    lax.fori_loop(0, nch, attn_body, 0)

    lane = lax.broadcasted_iota(jnp.int32, (tq, LANES), 1)
    for g in range(heads // 2):
        ra = slice((2 * g) * tq, (2 * g + 1) * tq)
        rb = slice((2 * g + 1) * tq, (2 * g + 2) * tq)
        oa = acc_ref[ra] / l_ref[ra]
        ob = acc_ref[rb] / l_ref[rb]
        o_ref[0, :, g * LANES:(g + 1) * LANES] = jnp.where(lane < HEAD_DIM, oa, ob).astype(o_ref.dtype)


def _dsa_attention(bq, iq, iwp, ik2, bk2, bv2, tq, tk):
    bsz, s, width = bq.shape
    heads = width // HEAD_DIM
    topk = min(TOPK_MAX, s // 4)
    nck = s // tk
    kern = functools.partial(_dsa_kernel, tq=tq, tk=tk, heads=heads, topk=topk,
                             slopes=_alibi_slopes(heads))
    qspec = lambda w: pl.BlockSpec((1, tq, w), lambda b, i: (b, i, 0))
    kspec = pl.BlockSpec((1, s, LANES), lambda b, i: (b, 0, 0))
    return pl.pallas_call(
        kern,
        grid=(bsz, s // tq),
        in_specs=[qspec(width), qspec(width), qspec(LANES), kspec, kspec, kspec],
        out_specs=pl.BlockSpec((1, tq, width), lambda b, i: (b, i, 0)),
        out_shape=jax.ShapeDtypeStruct((bsz, s, width), BF16),
        scratch_shapes=[pltpu.VMEM((nck, tq, tk), F32),
                        pltpu.VMEM((nck, tq, tk), F32),
                        pltpu.VMEM((heads * tq, LANES), F32),
                        pltpu.VMEM((heads * tq, LANES), F32),
                        pltpu.VMEM((heads * tq, LANES), F32)],
        compiler_params=pltpu.CompilerParams(
            dimension_semantics=("arbitrary", "arbitrary"), vmem_limit_bytes=VMEM_LIMIT),
        name="dsa_attn",
    )(bq, iq, iwp, ik2, bk2, bv2)


def _swa_kernel(slope_ref, sink_ref, q_ref, kp_ref, kc_ref, vp_ref, vc_ref, o_ref, *, tq, group):
    kv = pl.program_id(1)
    i = pl.program_id(2)
    nk = 2 * WINDOW
    kk = jnp.concatenate([kp_ref[0], kc_ref[0]], axis=0)
    vv = jnp.concatenate([vp_ref[0], vc_ref[0]], axis=0)
    r = lax.broadcasted_iota(jnp.int32, (WINDOW, nk), 0)
    c = lax.broadcasted_iota(jnp.int32, (WINDOW, nk), 1)
    dist = WINDOW + r - c
    band = (dist >= 0) & (dist < WINDOW)
    distf = dist.astype(F32)
    lane = lax.broadcasted_iota(jnp.int32, (WINDOW, LANES), 1)
    for blk in range(tq // WINDOW):
        rows = slice(blk * WINDOW, (blk + 1) * WINDOW)
        qs = _stack_heads(q_ref[0, rows, :], group)
        kb = kk[blk * WINDOW:blk * WINDOW + nk]
        vb = vv[blk * WINDOW:blk * WINDOW + nk]
        logit = lax.dot_general(qs, kb, _NT, preferred_element_type=F32)
        valid = band
        if blk == 0:
            valid = band & (c >= jnp.where(i == 0, WINDOW, 0))
        outs = []
        for g in range(group):
            hq = kv * group + g
            s = logit[g * WINDOW:(g + 1) * WINDOW] - (slope_ref[hq] * LOG2E) * distf
            s = jnp.where(valid, s, NEG_BIG)
            sink = sink_ref[hq] * LOG2E
            m = jnp.maximum(jnp.max(s, axis=1, keepdims=True), sink)
            p = jnp.exp2(s - m)
            denom = jnp.sum(p, axis=1, keepdims=True) + jnp.exp2(sink - m)
            outs.append(jnp.dot(p.astype(BF16), vb, preferred_element_type=F32) / denom)
        for g in range(group // 2):
            o_ref[0, rows, g * LANES:(g + 1) * LANES] = jnp.where(
                lane < HEAD_DIM, outs[2 * g], outs[2 * g + 1]).astype(o_ref.dtype)


def _swa_attention(q, k2, v2, sinks, tq):
    bsz, s, width = q.shape
    heads = width // HEAD_DIM
    kvh = k2.shape[2] // LANES
    group = heads // kvh
    slopes = jnp.asarray(_alibi_slopes(heads), F32)
    r = tq // WINDOW
    prev = lambda b, kv, i: (b, jnp.maximum(i * r - 1, 0), kv)
    cur = lambda b, kv, i: (b, i, kv)
    smem = pl.BlockSpec(memory_space=pltpu.SMEM)
    return pl.pallas_call(
        functools.partial(_swa_kernel, tq=tq, group=group),
        grid=(bsz, kvh, s // tq),
        in_specs=[smem, smem,
                  pl.BlockSpec((1, tq, group * HEAD_DIM), cur),
                  pl.BlockSpec((1, WINDOW, LANES), prev),
                  pl.BlockSpec((1, tq, LANES), cur),
                  pl.BlockSpec((1, WINDOW, LANES), prev),
                  pl.BlockSpec((1, tq, LANES), cur)],
        out_specs=pl.BlockSpec((1, tq, group * HEAD_DIM), cur),
        out_shape=jax.ShapeDtypeStruct((bsz, s, width), BF16),
        compiler_params=pltpu.CompilerParams(
            dimension_semantics=("arbitrary", "arbitrary", "arbitrary"),
            vmem_limit_bytes=VMEM_LIMIT),
        name="swa_attn",
    )(slopes, sinks.astype(F32), q, k2, k2, v2, v2)


def _out_ffn_kernel(*refs, ny):
    x_ref = refs[0]
    y_refs = refs[1:1 + ny]
    wo_ref, g1_ref, n_ref, sc_ref, sh_ref, g2_ref, wg_ref, wu_ref, wd_ref, o_ref = refs[1 + ny:]
    mix = None
    r0 = 0
    for y_ref in y_refs:
        w = y_ref.shape[2]
        part = jnp.dot(y_ref[0], wo_ref[r0:r0 + w, :], preferred_element_type=F32)
        mix = part if mix is None else mix + part
        r0 += w
    x = x_ref[0] + g1_ref[0] * mix
    h = _modulated_norm(x, n_ref[...], sc_ref[0], sh_ref[0]).astype(BF16)
    gate = jnp.dot(h, wg_ref[...], preferred_element_type=F32)
    up = jnp.dot(h, wu_ref[...], preferred_element_type=F32)
    act = (gate * jax.nn.sigmoid(gate) * up).astype(BF16)
    o_ref[0] = x + g2_ref[0] * jnp.dot(act, wd_ref[...], preferred_element_type=F32)


def _out_ffn(x, ys, wo, g1, n, sc, sh, g2, wg, wu, wd, tm):
    bsz, s, d = x.shape
    f = wg.shape[1]
    const = lambda b, i: (0, 0)
    tok = lambda b, i: (b, i, 0)
    mod = lambda b, i: (b, 0, 0)
    once = pl.Buffered(1)
    return pl.pallas_call(
        functools.partial(_out_ffn_kernel, ny=len(ys)),
        grid=(bsz, s // tm),
        in_specs=[pl.BlockSpec((1, tm, d), tok)]
                 + [pl.BlockSpec((1, tm, y.shape[2]), tok) for y in ys]
                 + [pl.BlockSpec((d, d), const, pipeline_mode=once),
                  pl.BlockSpec((1, 1, d), mod),
                  pl.BlockSpec((1, d), const),
                  pl.BlockSpec((1, 1, d), mod),
                  pl.BlockSpec((1, 1, d), mod),
                  pl.BlockSpec((1, 1, d), mod),
                  pl.BlockSpec((d, f), const, pipeline_mode=once),
                  pl.BlockSpec((d, f), const, pipeline_mode=once),
                  pl.BlockSpec((f, d), const, pipeline_mode=once)],
        out_specs=pl.BlockSpec((1, tm, d), tok),
        out_shape=jax.ShapeDtypeStruct((bsz, s, d), F32),
        compiler_params=pltpu.CompilerParams(
            dimension_semantics=("arbitrary", "arbitrary"), vmem_limit_bytes=VMEM_LIMIT),
        name="out_ffn",
    )(x, *ys, wo, g1, n.reshape(1, d), sc, sh, g2, wg, wu, wd)


def _dup(w):
    d, c = w.shape
    w = w.reshape(d, c // HEAD_DIM, 1, HEAD_DIM)
    return jnp.broadcast_to(w, (d, c // HEAD_DIM, 2, HEAD_DIM)).reshape(d, 2 * c)


def _even_layout(w_in, qn_a, kn_a, qn_b, kn_b, d):
    a = d // 2
    sizes = (a, a, a, a, HEAD_DIM, HEAD_DIM, a, HEAD_DIM, a // HEAD_DIM)
    offs = [0]
    for sz in sizes:
        offs.append(offs[-1] + sz)
    aq, ak, av, bq, bk, bv, iq, ik, iw = [w_in[:, offs[n]:offs[n + 1]] for n in range(9)]
    nh = a // HEAD_DIM
    iw_pad = jnp.zeros((d, LANES), w_in.dtype).at[:, :nh].set(iw)
    w = jnp.concatenate([aq, ak, av, bq, iq, _dup(bk), _dup(bv), _dup(ik), iw_pad], axis=1)
    qscale = HEAD_DIM ** -0.5 * LOG2E
    ones = lambda n: jnp.ones((n,), F32)
    gain = jnp.concatenate([
        jnp.tile(qn_a, nh) * qscale, jnp.tile(kn_a, nh), ones(a),
        jnp.tile(qn_b, nh) * qscale, ones(a) * HEAD_DIM ** -0.5,
        jnp.tile(kn_b, 2), ones(LANES), ones(LANES), ones(LANES) * nh ** -0.5])
    groups = (
        (0, a, ((0, a, 0, 0, "norm"),)),
        (a, a, ((0, a, 1, 0, "norm"),)),
        (2 * a, a, ((0, a, 2, 0, "plain"),)),
        (3 * a, a, ((0, a, 3, 0, "norm"),)),
        (4 * a, a, ((0, a, 4, 0, "scale"),)),
        (5 * a, 4 * LANES, ((0, LANES, 5, 0, "norm"), (LANES, LANES, 6, 0, "plain"),
                            (2 * LANES, LANES, 7, 0, "plain"), (3 * LANES, LANES, 8, 0, "scale"))),
    )
    out_defs = ((a, BF16), (a, BF16), (a, BF16), (a, BF16), (a, BF16),
                (LANES, BF16), (LANES, BF16), (LANES, BF16), (LANES, F32))
    return w.astype(BF16), gain.reshape(1, -1), groups, out_defs


def _odd_layout(w_in, qn_c, kn_c, d):
    kvw = d // 4
    q, k, v = w_in[:, :d], w_in[:, d:d + kvw], w_in[:, d + kvw:]
    w = jnp.concatenate([q, _dup(k), _dup(v)], axis=1)
    nh = d // HEAD_DIM
    gain = jnp.concatenate([jnp.tile(qn_c, nh) * (HEAD_DIM ** -0.5 * LOG2E),
                            jnp.tile(kn_c, 2 * kvw // HEAD_DIM), jnp.ones((2 * kvw,), F32)])
    half = d // 2
    groups = (
        (0, half, ((0, half, 0, 0, "norm"),)),
        (half, half, ((0, half, 0, half, "norm"),)),
        (d, 2 * kvw, ((0, 2 * kvw, 1, 0, "norm"),)),
        (d + 2 * kvw, 2 * kvw, ((0, 2 * kvw, 2, 0, "plain"),)),
    )
    out_defs = ((d, BF16), (2 * kvw, BF16), (2 * kvw, BF16))
    return w.astype(BF16), gain.reshape(1, -1), groups, out_defs


def kernel(x, c, ada_w, ada_b, norm_mix, norm_ffn, w_out, ffn_gate, ffn_up, ffn_down,
           w_in_even, qn_a, kn_a, lam_q1, lam_k1, lam_q2, lam_k2, subln_a, qn_b, kn_b,
           w_in_odd, qn_c, kn_c, sinks_c):
    depth, d = norm_mix.shape
    mod = _ada_mod(c, ada_w, ada_b)
    for l in range(depth):
        sh1, sc1, g1, sh2, sc2, g2 = [mod[l, :, None, n * d:(n + 1) * d] for n in range(6)]
        if l % 2 == 0:
            e = l // 2
            w, gain, groups, out_defs = _even_layout(w_in_even[e], qn_a[e], kn_a[e],
                                                     qn_b[e], kn_b[e], d)
            aq, ak, av, bq, iq, bk2, bv2, ik2, iwp = _proj(
                x, norm_mix[l], sc1, sh1, w, gain, groups, out_defs, tm=256)
            lam_vecs = jnp.stack([lam_q1[e], lam_k1[e], lam_q2[e], lam_k2[e]]).astype(F32)
            ya = _diff_attention(aq, ak, av, lam_vecs, subln_a[e], l, t=256)
            yb = _dsa_attention(bq, iq, iwp, ik2, bk2, bv2, tq=128, tk=512)
            ys = (ya, yb)
        else:
            o = l // 2
            w, gain, groups, out_defs = _odd_layout(w_in_odd[o], qn_c[o], kn_c[o], d)
            q, k2, v2 = _proj(x, norm_mix[l], sc1, sh1, w, gain, groups, out_defs, tm=256)
            ys = (_swa_attention(q, k2, v2, sinks_c[o], tq=512),)
        x = _out_ffn(x, ys, w_out[l].astype(BF16), g1, norm_ffn[l], sc2, sh2, g2,
                     ffn_gate[l].astype(BF16), ffn_up[l].astype(BF16),
                     ffn_down[l].astype(BF16), tm=256)
    return x
```

```python
import functools
import math

import jax
import jax.numpy as jnp
from jax import lax
from jax.experimental import pallas as pl
from jax.experimental.pallas import tpu as pltpu

HEAD_DIM = 64
LANES = 128
ROWS = 128
NORM_EPS = 1e-6
TOPK_MAX = 256
WINDOW = 128
NEG_BIG = -1e30
LOG2E = 1.4426950408889634
KEY_NEG_INF = -2139095041
VMEM_LIMIT = 56 * 1024 * 1024

F32 = jnp.float32
BF16 = jnp.bfloat16
_NT = (((1,), (1,)), ((), ()))


def _alibi_slopes(n):
    return [2.0 ** (-8.0 * (i + 1) / n) for i in range(n)]


def _tile_lanes(x, n):
    return x if n == 1 else jnp.concatenate([x] * n, axis=1)


def _ada_kernel(c_ref, w_ref, b_ref, o_ref):
    c = c_ref[...]
    cond = c * jax.nn.sigmoid(c)
    o_ref[0] = jnp.dot(cond, w_ref[0], preferred_element_type=F32,
                       precision=lax.Precision.HIGHEST) + b_ref[0]


def _ada_mod(c, ada_w, ada_b):
    depth, d, n = ada_w.shape
    b = c.shape[0]
    rows = 8
    tn = 1536
    c_pad = jnp.zeros((rows, d), F32).at[:b].set(c)
    out = pl.pallas_call(
        _ada_kernel,
        grid=(depth, n // tn),
        in_specs=[pl.BlockSpec((rows, d), lambda l, j: (0, 0)),
                  pl.BlockSpec((1, d, tn), lambda l, j: (l, 0, j)),
                  pl.BlockSpec((1, 1, tn), lambda l, j: (l, 0, j))],
        out_specs=pl.BlockSpec((1, rows, tn), lambda l, j: (l, 0, j)),
        out_shape=jax.ShapeDtypeStruct((depth, rows, n), F32),
        compiler_params=pltpu.CompilerParams(
            dimension_semantics=("arbitrary", "arbitrary"), vmem_limit_bytes=VMEM_LIMIT),
        name="ada_mod",
    )(c_pad, ada_w, ada_b.reshape(depth, 1, n))
    return out[:, :b]


def _modulated_norm(x, g, sc, sh):
    ms = jnp.mean(x * x, axis=-1, keepdims=True)
    return (x * lax.rsqrt(ms + NORM_EPS) * g) * (1.0 + sc) + sh


def _proj_kernel(x_ref, g_ref, sc_ref, sh_ref, w_ref, bd_ref, gain_ref, *out_refs, groups):
    h = _modulated_norm(x_ref[0], g_ref[...], sc_ref[0], sh_ref[0]).astype(BF16)
    bd = bd_ref[...]
    for (c0, width, parts) in groups:
        y = jnp.dot(h, w_ref[:, c0:c0 + width], preferred_element_type=F32)
        for (p0, pw, oi, o0, mode) in parts:
            for s in range(0, pw, LANES):
                ys = y[:, p0 + s:p0 + s + LANES]
                col = c0 + p0 + s
                if mode == "norm":
                    sq = ys * ys
                    hi = sq.astype(BF16)
                    lo = (sq - hi.astype(F32)).astype(BF16)
                    ss = (jnp.dot(hi, bd, preferred_element_type=F32)
                          + jnp.dot(lo, bd, preferred_element_type=F32))
                    ys = ys * lax.rsqrt(ss * (1.0 / HEAD_DIM) + NORM_EPS)
                if mode != "plain":
                    ys = ys * gain_ref[:, col:col + LANES]
                out_refs[oi][0, :, o0 + s:o0 + s + LANES] = ys.astype(out_refs[oi].dtype)


def _proj(x, g, sc, sh, w, gain, groups, out_defs, tm):
    bsz, s, d = x.shape
    c = w.shape[1]
    r = lax.broadcasted_iota(jnp.int32, (LANES, LANES), 0) // HEAD_DIM
    cc = lax.broadcasted_iota(jnp.int32, (LANES, LANES), 1) // HEAD_DIM
    bd = (r == cc).astype(BF16)
    const = lambda b, i: (0, 0)
    return pl.pallas_call(
        functools.partial(_proj_kernel, groups=groups),
        grid=(bsz, s // tm),
        in_specs=[pl.BlockSpec((1, tm, d), lambda b, i: (b, i, 0)),
                  pl.BlockSpec((1, d), const),
                  pl.BlockSpec((1, 1, d), lambda b, i: (b, 0, 0)),
                  pl.BlockSpec((1, 1, d), lambda b, i: (b, 0, 0)),
                  pl.BlockSpec((d, c), const),
                  pl.BlockSpec((LANES, LANES), const),
                  pl.BlockSpec((1, c), const)],
        out_specs=[pl.BlockSpec((1, tm, wd), lambda b, i: (b, i, 0)) for wd, _ in out_defs],
        out_shape=[jax.ShapeDtypeStruct((bsz, s, wd), dt) for wd, dt in out_defs],
        compiler_params=pltpu.CompilerParams(
            dimension_semantics=("arbitrary", "arbitrary"), vmem_limit_bytes=VMEM_LIMIT),
        name="in_proj",
    )(x, g.reshape(1, d), sc, sh, w, bd, gain)


def _diff_attn_kernel(q_ref, k_ref, v_ref, lam_ref, subln_ref, o_ref,
                      qs_ref, m_ref, l_ref, acc_ref, *, t, heads, slopes, lam_init):
    i = pl.program_id(1)
    lane = lax.broadcasted_iota(jnp.int32, (t, LANES), 1)
    for h in range(heads):
        q = q_ref[0, :, h * LANES:(h + 1) * LANES]
        zero = jnp.zeros_like(q)
        qs_ref[h, :t] = jnp.where(lane < HEAD_DIM, q, zero)
        qs_ref[h, t:] = jnp.where(lane >= HEAD_DIM, q, zero)
    m_ref[...] = jnp.full(m_ref.shape, NEG_BIG, F32)
    l_ref[...] = jnp.zeros(l_ref.shape, F32)
    acc_ref[...] = jnp.zeros(acc_ref.shape, F32)

    def step(start, width, masked):
        col = lax.broadcasted_iota(jnp.int32, (1, width), 1)
        rel = (start - i * t + col).astype(F32)
        if masked:
            r = lax.broadcasted_iota(jnp.int32, (t, width), 0)
            c = lax.broadcasted_iota(jnp.int32, (t, width), 1)
            keep = r >= c
            keep = jnp.concatenate([keep, keep], axis=0)
        for h in range(heads):
            kc = k_ref[0, pl.ds(start, width), h * LANES:(h + 1) * LANES]
            vc = v_ref[0, pl.ds(start, width), h * LANES:(h + 1) * LANES]
            s = lax.dot_general(qs_ref[h], kc, _NT, preferred_element_type=F32)
            s = s + (slopes[h] * LOG2E) * rel
            if masked:
                s = jnp.where(keep, s, NEG_BIG)
            m_prev = m_ref[h]
            m_next = jnp.maximum(m_prev, jnp.max(s, axis=1, keepdims=True))
            alpha = jnp.exp2(m_prev - m_next)
            p = jnp.exp2(s - _tile_lanes(m_next, width // LANES))
            l_ref[h] = alpha * l_ref[h] + jnp.sum(p, axis=1, keepdims=True)
            acc_ref[h] = alpha * acc_ref[h] + jnp.dot(p.astype(BF16), vc,
                                                      preferred_element_type=F32)
            m_ref[h] = m_next

    def body(j, carry):
        step(pl.multiple_of(j * (2 * t), 2 * t), 2 * t, False)
        return carry

    lax.fori_loop(0, i // 2, body, 0)

    @pl.when(i % 2 == 1)
    def _():
        step(pl.multiple_of((i - 1) * t, t), t, False)

    step(pl.multiple_of(i * t, t), t, True)

    lam_v = lam_ref[...]
    s1 = jnp.sum(lam_v[0:1] * lam_v[1:2], axis=-1, keepdims=True)
    s2 = jnp.sum(lam_v[2:3] * lam_v[3:4], axis=-1, keepdims=True)
    lam = jnp.exp(s1) - jnp.exp(s2) + lam_init
    for h in range(heads):
        o = acc_ref[h] / l_ref[h]
        y = o[:t] - lam * o[t:]
        ms = jnp.mean(y * y, axis=-1, keepdims=True)
        y = (y * lax.rsqrt(ms + NORM_EPS) * subln_ref[...]) * (1.0 - lam_init)
        o_ref[0, :, h * LANES:(h + 1) * LANES] = y.astype(o_ref.dtype)


def _diff_attention(q, k, v, lam_vecs, subln, layer_idx, t):
    bsz, s, width = q.shape
    heads = width // LANES
    lam_init = 0.8 - 0.6 * math.exp(-0.3 * layer_idx)
    kern = functools.partial(_diff_attn_kernel, t=t, heads=heads,
                             slopes=_alibi_slopes(heads), lam_init=lam_init)
    return pl.pallas_call(
        kern,
        grid=(bsz, s // t),
        in_specs=[pl.BlockSpec((1, t, width), lambda b, i: (b, i, 0)),
                  pl.BlockSpec((1, s, width), lambda b, i: (b, 0, 0)),
                  pl.BlockSpec((1, s, width), lambda b, i: (b, 0, 0)),
                  pl.BlockSpec((4, HEAD_DIM), lambda b, i: (0, 0)),
                  pl.BlockSpec((1, LANES), lambda b, i: (0, 0))],
        out_specs=pl.BlockSpec((1, t, width), lambda b, i: (b, i, 0)),
        out_shape=jax.ShapeDtypeStruct((bsz, s, width), BF16),
        scratch_shapes=[pltpu.VMEM((heads, 2 * t, LANES), BF16),
                        pltpu.VMEM((heads, 2 * t, LANES), F32),
                        pltpu.VMEM((heads, 2 * t, LANES), F32),
                        pltpu.VMEM((heads, 2 * t, LANES), F32)],
        compiler_params=pltpu.CompilerParams(
            dimension_semantics=("arbitrary", "arbitrary"), vmem_limit_bytes=VMEM_LIMIT),
        name="diff_attn",
    )(q, k, v, lam_vecs, subln.reshape(1, LANES))


def _stack_heads(x, heads):
    lane = lax.broadcasted_iota(jnp.int32, (x.shape[0], LANES), 1)
    lo = lane < HEAD_DIM
    parts = []
    for h in range(heads):
        blk = x[:, (h // 2) * LANES:(h // 2 + 1) * LANES]
        keep = lo if h % 2 == 0 else jnp.logical_not(lo)
        parts.append(jnp.where(keep, blk, jnp.zeros_like(blk)))
    return jnp.concatenate(parts, axis=0)


def _key_to_float(k):
    return lax.bitcast_convert_type(k ^ ((k >> 31) & jnp.int32(0x7FFFFFFF)), F32)


def _dsa_kernel(bq_ref, iq_ref, iw_ref, ik_ref, bk_ref, bv_ref, o_ref,
                sc_ref, iqs_ref, bqs_ref, wb_ref, m_ref, acc_ref,
                *, tq, tk, heads, topk, slopes):
    i = pl.program_id(1)
    q0 = i * tq
    nch = q0 // tk + 1
    nsl = tk // LANES
    row = lax.broadcasted_iota(jnp.int32, (tq, tk), 0)
    col = lax.broadcasted_iota(jnp.int32, (tq, tk), 1)

    lane = lax.broadcasted_iota(jnp.int32, (tq, LANES), 1)
    iw = iw_ref[0]
    for h in range(heads):
        keep = (lane < HEAD_DIM) if h % 2 == 0 else (lane >= HEAD_DIM)
        tile = slice((h // 2) * LANES, (h // 2 + 1) * LANES)
        iqb = iq_ref[0, :, tile]
        bqb = bq_ref[0, :, tile]
        iqs_ref[h] = jnp.where(keep, iqb, jnp.zeros_like(iqb))
        bqs_ref[h * tq:(h + 1) * tq] = jnp.where(keep, bqb, jnp.zeros_like(bqb))
        wb_ref[h] = jnp.broadcast_to(iw[:, h:h + 1], (tq, LANES))

    def score_body(j, carry):
        start = pl.multiple_of(j * tk, tk)
        ikc = ik_ref[0, pl.ds(start, tk), :]
        sc = None
        for h in range(heads):
            logit = lax.dot_general(iqs_ref[h], ikc, _NT, preferred_element_type=F32)
            term = jnp.maximum(logit, 0.0) * _tile_lanes(wb_ref[h], nsl)
            sc = term if sc is None else sc + term
        sc = sc + 0.0
        causal = (start + col) <= (q0 + row)
        sc_ref[j] = jnp.where(causal, sc, -jnp.inf)
        return carry

    lax.fori_loop(0, nch, score_body, 0)

    def count(pred_fn, thr):
        parts = []
        for r0 in range(0, tq, ROWS):
            thr_b = thr[r0:r0 + ROWS]

            def body(j, cnt):
                sc = sc_ref[j, r0:r0 + ROWS, :]
                for s in range(nsl):
                    cnt = cnt + jnp.where(pred_fn(sc[:, s * LANES:(s + 1) * LANES], thr_b), 1.0, 0.0)
                return cnt

            parts.append(lax.fori_loop(0, nch, body, jnp.zeros((ROWS, LANES), F32)))
        cnt = jnp.concatenate(parts, axis=0)
        return jnp.broadcast_to(jnp.sum(cnt, axis=1, keepdims=True), (tq, LANES))

    ge = lambda a, b: a >= b
    gt = lambda a, b: a > b

    def bit_body(b, key):
        cand = key + lax.shift_left(jnp.int32(1), 31 - b)
        return jnp.where(count(ge, _key_to_float(cand)) >= float(topk), cand, key)

    int_min = jnp.int32(-2 ** 31)
    key = lax.fori_loop(0, 32, bit_body, jnp.full((tq, LANES), int_min, jnp.int32))
    thr = jnp.where(key <= KEY_NEG_INF, -jnp.inf, _key_to_float(key))
    n_ge = count(ge, thr)
    n_gt = count(gt, thr)
    need = float(topk) - n_gt
    has_excess_ties = jnp.max(jnp.where(n_ge > float(topk), 1.0, 0.0)) > 0.0
    thr_t = _tile_lanes(thr, nsl)

    @pl.when(jnp.logical_not(has_excess_ties))
    def _():
        def body(j, carry):
            start = pl.multiple_of(j * tk, tk)
            keep = (sc_ref[j] >= thr_t) & ((start + col) <= (q0 + row))
            sc_ref[j] = jnp.where(keep, 0.0, NEG_BIG)
            return carry
        lax.fori_loop(0, nch, body, 0)

    @pl.when(has_excess_ties)
    def _():
        r2 = lax.broadcasted_iota(jnp.int32, (tk, tk), 0)
        c2 = lax.broadcasted_iota(jnp.int32, (tk, tk), 1)
        upper = jnp.where(r2 <= c2, 1.0, 0.0).astype(BF16)
        need_t = _tile_lanes(need, nsl)

        def body(j, seen):
            start = pl.multiple_of(j * tk, tk)
            sc = sc_ref[j]
            eq = sc == thr_t
            eqf = jnp.where(eq, 1.0, 0.0)
            rank = _tile_lanes(seen, nsl) + jnp.dot(eqf.astype(BF16), upper,
                                                    preferred_element_type=F32)
            keep = ((sc > thr_t) | (eq & (rank <= need_t))) & ((start + col) <= (q0 + row))
            sc_ref[j] = jnp.where(keep, 0.0, NEG_BIG)
            return seen + jnp.broadcast_to(jnp.sum(eqf, axis=1, keepdims=True), (tq, LANES))

        lax.fori_loop(0, nch, body, jnp.zeros((tq, LANES), F32))

    m_ref[...] = jnp.full(m_ref.shape, NEG_BIG, F32)
    acc_ref[...] = jnp.zeros(acc_ref.shape, F32)
    kcol = lax.broadcasted_iota(jnp.int32, (1, tk), 1)
    vlane = lax.broadcasted_iota(jnp.int32, (tk, LANES), 1)

    def attn_body(j, carry):
        start = pl.multiple_of(j * tk, tk)
        kc = bk_ref[0, pl.ds(start, tk), :]
        vc = bv_ref[0, pl.ds(start, tk), :]
        one = jnp.ones_like(vc)
        v_even = jnp.where(vlane < HEAD_DIM, vc, one)
        v_odd = jnp.where(vlane < HEAD_DIM, one, vc)
        logit = lax.dot_general(bqs_ref[...], kc, _NT, preferred_element_type=F32)
        rel = (start - q0 + kcol).astype(F32)
        for h in range(heads):
            alibi = (slopes[h] * LOG2E) * rel
            for r0 in range(0, tq, ROWS):
                rs = slice(h * tq + r0, h * tq + r0 + ROWS)
                s = logit[rs] + sc_ref[j, r0:r0 + ROWS, :] + alibi
                m_prev = m_ref[rs]
                m_next = jnp.maximum(m_prev, jnp.max(s, axis=1, keepdims=True))
                alpha = jnp.exp2(m_prev - m_next)
                p = jnp.exp2(s - _tile_lanes(m_next, nsl))
                acc_ref[rs] = alpha * acc_ref[rs] + jnp.dot(
                    p.astype(BF16), v_even if h % 2 == 0 else v_odd, preferred_element_type=F32)
                m_ref[rs] = m_next
        return carry

    lax.fori_loop(0, nch, attn_body, 0)

    for g in range(heads // 2):
        a = acc_ref[(2 * g) * tq:(2 * g + 1) * tq]
        b = acc_ref[(2 * g + 1) * tq:(2 * g + 2) * tq]
        oa = a / pltpu.roll(a, HEAD_DIM, axis=1)
        ob = b / pltpu.roll(b, HEAD_DIM, axis=1)
        o_ref[0, :, g * LANES:(g + 1) * LANES] = jnp.where(lane < HEAD_DIM, oa, ob).astype(o_ref.dtype)


def _dsa_attention(bq, iq, iwp, ik2, bk2, bv2, tq, tk):
    bsz, s, width = bq.shape
    heads = width // HEAD_DIM
    topk = min(TOPK_MAX, s // 4)
    nck = s // tk
    kern = functools.partial(_dsa_kernel, tq=tq, tk=tk, heads=heads, topk=topk,
                             slopes=_alibi_slopes(heads))
    qspec = lambda w: pl.BlockSpec((1, tq, w), lambda b, i: (b, i, 0))
    kspec = pl.BlockSpec((1, s, LANES), lambda b, i: (b, 0, 0))
    return pl.pallas_call(
        kern,
        grid=(bsz, s // tq),
        in_specs=[qspec(width), qspec(width), qspec(LANES), kspec, kspec, kspec],
        out_specs=pl.BlockSpec((1, tq, width), lambda b, i: (b, i, 0)),
        out_shape=jax.ShapeDtypeStruct((bsz, s, width), BF16),
        scratch_shapes=[pltpu.VMEM((nck, tq, tk), F32),
                        pltpu.VMEM((heads, tq, LANES), BF16),
                        pltpu.VMEM((heads * tq, LANES), BF16),
                        pltpu.VMEM((heads, tq, LANES), F32),
                        pltpu.VMEM((heads * tq, LANES), F32),
                        pltpu.VMEM((heads * tq, LANES), F32)],
        compiler_params=pltpu.CompilerParams(
            dimension_semantics=("arbitrary", "arbitrary"), vmem_limit_bytes=VMEM_LIMIT),
        name="dsa_attn",
    )(bq, iq, iwp, ik2, bk2, bv2)


def _swa_kernel(slope_ref, sink_ref, q_ref, kp_ref, kc_ref, vp_ref, vc_ref, o_ref, *, tq, group):
    kv = pl.program_id(1)
    i = pl.program_id(2)
    nk = 2 * WINDOW
    kk = jnp.concatenate([kp_ref[0], kc_ref[0]], axis=0)
    vv = jnp.concatenate([vp_ref[0], vc_ref[0]], axis=0)
    r = lax.broadcasted_iota(jnp.int32, (WINDOW, nk), 0)
    c = lax.broadcasted_iota(jnp.int32, (WINDOW, nk), 1)
    dist = WINDOW + r - c
    band = (dist >= 0) & (dist < WINDOW)
    distf = dist.astype(F32)
    lane = lax.broadcasted_iota(jnp.int32, (WINDOW, LANES), 1)
    for blk in range(tq // WINDOW):
        rows = slice(blk * WINDOW, (blk + 1) * WINDOW)
        qs = _stack_heads(q_ref[0, rows, :], group)
        kb = kk[blk * WINDOW:blk * WINDOW + nk]
        vb = vv[blk * WINDOW:blk * WINDOW + nk]
        logit = lax.dot_general(qs, kb, _NT, preferred_element_type=F32)
        valid = band
        if blk == 0:
            valid = band & (c >= jnp.where(i == 0, WINDOW, 0))
        outs = []
        for g in range(group):
            hq = kv * group + g
            s = logit[g * WINDOW:(g + 1) * WINDOW] - (slope_ref[hq] * LOG2E) * distf
            s = jnp.where(valid, s, NEG_BIG)
            sink = sink_ref[hq] * LOG2E
            m = jnp.maximum(jnp.max(s, axis=1, keepdims=True), sink)
            p = jnp.exp2(s - m)
            denom = jnp.sum(p, axis=1, keepdims=True) + jnp.exp2(sink - m)
            outs.append(jnp.dot(p.astype(BF16), vb, preferred_element_type=F32) / denom)
        for g in range(group // 2):
            o_ref[0, rows, g * LANES:(g + 1) * LANES] = jnp.where(
                lane < HEAD_DIM, outs[2 * g], outs[2 * g + 1]).astype(o_ref.dtype)


def _swa_attention(q, k2, v2, sinks, tq):
    bsz, s, width = q.shape
    heads = width // HEAD_DIM
    kvh = k2.shape[2] // LANES
    group = heads // kvh
    slopes = jnp.asarray(_alibi_slopes(heads), F32)
    r = tq // WINDOW
    prev = lambda b, kv, i: (b, jnp.maximum(i * r - 1, 0), kv)
    cur = lambda b, kv, i: (b, i, kv)
    smem = pl.BlockSpec(memory_space=pltpu.SMEM)
    return pl.pallas_call(
        functools.partial(_swa_kernel, tq=tq, group=group),
        grid=(bsz, kvh, s // tq),
        in_specs=[smem, smem,
                  pl.BlockSpec((1, tq, group * HEAD_DIM), cur),
                  pl.BlockSpec((1, WINDOW, LANES), prev),
                  pl.BlockSpec((1, tq, LANES), cur),
                  pl.BlockSpec((1, WINDOW, LANES), prev),
                  pl.BlockSpec((1, tq, LANES), cur)],
        out_specs=pl.BlockSpec((1, tq, group * HEAD_DIM), cur),
        out_shape=jax.ShapeDtypeStruct((bsz, s, width), BF16),
        compiler_params=pltpu.CompilerParams(
            dimension_semantics=("arbitrary", "arbitrary", "arbitrary"),
            vmem_limit_bytes=VMEM_LIMIT),
        name="swa_attn",
    )(slopes, sinks.astype(F32), q, k2, k2, v2, v2)


def _out_ffn_kernel(*refs, ny):
    x_ref = refs[0]
    y_refs = refs[1:1 + ny]
    wo_ref, g1_ref, n_ref, sc_ref, sh_ref, g2_ref, wg_ref, wu_ref, wd_ref, o_ref = refs[1 + ny:]
    mix = None
    r0 = 0
    for y_ref in y_refs:
        w = y_ref.shape[2]
        part = jnp.dot(y_ref[0], wo_ref[r0:r0 + w, :], preferred_element_type=F32)
        mix = part if mix is None else mix + part
        r0 += w
    x = x_ref[0] + g1_ref[0] * mix
    h = _modulated_norm(x, n_ref[...], sc_ref[0], sh_ref[0]).astype(BF16)
    gate = jnp.dot(h, wg_ref[...], preferred_element_type=F32)
    up = jnp.dot(h, wu_ref[...], preferred_element_type=F32)
    act = (gate * jax.nn.sigmoid(gate) * up).astype(BF16)
    o_ref[0] = x + g2_ref[0] * jnp.dot(act, wd_ref[...], preferred_element_type=F32)


def _out_ffn(x, ys, wo, g1, n, sc, sh, g2, wg, wu, wd, tm):
    bsz, s, d = x.shape
    f = wg.shape[1]
    const = lambda b, i: (0, 0)
    tok = lambda b, i: (b, i, 0)
    mod = lambda b, i: (b, 0, 0)
    once = pl.Buffered(1)
    return pl.pallas_call(
        functools.partial(_out_ffn_kernel, ny=len(ys)),
        grid=(bsz, s // tm),
        in_specs=[pl.BlockSpec((1, tm, d), tok)]
                 + [pl.BlockSpec((1, tm, y.shape[2]), tok) for y in ys]
                 + [pl.BlockSpec((d, d), const, pipeline_mode=once),
                  pl.BlockSpec((1, 1, d), mod),
                  pl.BlockSpec((1, d), const),
                  pl.BlockSpec((1, 1, d), mod),
                  pl.BlockSpec((1, 1, d), mod),
                  pl.BlockSpec((1, 1, d), mod),
                  pl.BlockSpec((d, f), const, pipeline_mode=once),
                  pl.BlockSpec((d, f), const, pipeline_mode=once),
                  pl.BlockSpec((f, d), const, pipeline_mode=once)],
        out_specs=pl.BlockSpec((1, tm, d), tok),
        out_shape=jax.ShapeDtypeStruct((bsz, s, d), F32),
        compiler_params=pltpu.CompilerParams(
            dimension_semantics=("arbitrary", "arbitrary"), vmem_limit_bytes=VMEM_LIMIT),
        name="out_ffn",
    )(x, *ys, wo, g1, n.reshape(1, d), sc, sh, g2, wg, wu, wd)


def _dup(w):
    d, c = w.shape
    w = w.reshape(d, c // HEAD_DIM, 1, HEAD_DIM)
    return jnp.broadcast_to(w, (d, c // HEAD_DIM, 2, HEAD_DIM)).reshape(d, 2 * c)


def _even_layout(w_in, qn_a, kn_a, qn_b, kn_b, d):
    a = d // 2
    sizes = (a, a, a, a, HEAD_DIM, HEAD_DIM, a, HEAD_DIM, a // HEAD_DIM)
    offs = [0]
    for sz in sizes:
        offs.append(offs[-1] + sz)
    aq, ak, av, bq, bk, bv, iq, ik, iw = [w_in[:, offs[n]:offs[n + 1]] for n in range(9)]
    nh = a // HEAD_DIM
    iw_pad = jnp.zeros((d, LANES), w_in.dtype).at[:, :nh].set(iw)
    w = jnp.concatenate([aq, ak, av, bq, iq, _dup(bk), _dup(bv), _dup(ik), iw_pad], axis=1)
    qscale = HEAD_DIM ** -0.5 * LOG2E
    ones = lambda n: jnp.ones((n,), F32)
    gain = jnp.concatenate([
        jnp.tile(qn_a, nh) * qscale, jnp.tile(kn_a, nh), ones(a),
        jnp.tile(qn_b, nh) * qscale, ones(a) * HEAD_DIM ** -0.5,
        jnp.tile(kn_b, 2), ones(LANES), ones(LANES), ones(LANES) * nh ** -0.5])
    groups = (
        (0, a, ((0, a, 0, 0, "norm"),)),
        (a, a, ((0, a, 1, 0, "norm"),)),
        (2 * a, a, ((0, a, 2, 0, "plain"),)),
        (3 * a, a, ((0, a, 3, 0, "norm"),)),
        (4 * a, a, ((0, a, 4, 0, "scale"),)),
        (5 * a, 4 * LANES, ((0, LANES, 5, 0, "norm"), (LANES, LANES, 6, 0, "plain"),
                            (2 * LANES, LANES, 7, 0, "plain"), (3 * LANES, LANES, 8, 0, "scale"))),
    )
    out_defs = ((a, BF16), (a, BF16), (a, BF16), (a, BF16), (a, BF16),
                (LANES, BF16), (LANES, BF16), (LANES, BF16), (LANES, F32))
    return w.astype(BF16), gain.reshape(1, -1), groups, out_defs


def _odd_layout(w_in, qn_c, kn_c, d):
    kvw = d // 4
    q, k, v = w_in[:, :d], w_in[:, d:d + kvw], w_in[:, d + kvw:]
    w = jnp.concatenate([q, _dup(k), _dup(v)], axis=1)
    nh = d // HEAD_DIM
    gain = jnp.concatenate([jnp.tile(qn_c, nh) * (HEAD_DIM ** -0.5 * LOG2E),
                            jnp.tile(kn_c, 2 * kvw // HEAD_DIM), jnp.ones((2 * kvw,), F32)])
    half = d // 2
    groups = (
        (0, half, ((0, half, 0, 0, "norm"),)),
        (half, half, ((0, half, 0, half, "norm"),)),
        (d, 2 * kvw, ((0, 2 * kvw, 1, 0, "norm"),)),
        (d + 2 * kvw, 2 * kvw, ((0, 2 * kvw, 2, 0, "plain"),)),
    )
    out_defs = ((d, BF16), (2 * kvw, BF16), (2 * kvw, BF16))
    return w.astype(BF16), gain.reshape(1, -1), groups, out_defs


def kernel(x, c, ada_w, ada_b, norm_mix, norm_ffn, w_out, ffn_gate, ffn_up, ffn_down,
           w_in_even, qn_a, kn_a, lam_q1, lam_k1, lam_q2, lam_k2, subln_a, qn_b, kn_b,
           w_in_odd, qn_c, kn_c, sinks_c):
    depth, d = norm_mix.shape
    mod = _ada_mod(c, ada_w, ada_b)
    for l in range(depth):
        sh1, sc1, g1, sh2, sc2, g2 = [mod[l, :, None, n * d:(n + 1) * d] for n in range(6)]
        if l % 2 == 0:
            e = l // 2
            w, gain, groups, out_defs = _even_layout(w_in_even[e], qn_a[e], kn_a[e],
                                                     qn_b[e], kn_b[e], d)
            aq, ak, av, bq, iq, bk2, bv2, ik2, iwp = _proj(
                x, norm_mix[l], sc1, sh1, w, gain, groups, out_defs, tm=256)
            lam_vecs = jnp.stack([lam_q1[e], lam_k1[e], lam_q2[e], lam_k2[e]]).astype(F32)
            ya = _diff_attention(aq, ak, av, lam_vecs, subln_a[e], l, t=256)
            yb = _dsa_attention(bq, iq, iwp, ik2, bk2, bv2, tq=512, tk=512)
            ys = (ya, yb)
        else:
            o = l // 2
            w, gain, groups, out_defs = _odd_layout(w_in_odd[o], qn_c[o], kn_c[o], d)
            q, k2, v2 = _proj(x, norm_mix[l], sc1, sh1, w, gain, groups, out_defs, tm=256)
            ys = (_swa_attention(q, k2, v2, sinks_c[o], tq=512),)
        x = _out_ffn(x, ys, w_out[l].astype(BF16), g1, norm_ffn[l], sc2, sh2, g2,
                     ffn_gate[l].astype(BF16), ffn_up[l].astype(BF16),
                     ffn_down[l].astype(BF16), tm=256)
    return x
```

```python
import functools
import math

import jax
import jax.numpy as jnp
from jax import lax
from jax.experimental import pallas as pl
from jax.experimental.pallas import tpu as pltpu

HEAD_DIM = 64
LANES = 128
ROWS = 128
NORM_EPS = 1e-6
TOPK_MAX = 256
WINDOW = 128
NEG_BIG = -1e30
LOG2E = 1.4426950408889634
KEY_NEG_INF = -2139095041
VMEM_LIMIT = 56 * 1024 * 1024

F32 = jnp.float32
BF16 = jnp.bfloat16
_NT = (((1,), (1,)), ((), ()))


def _alibi_slopes(n):
    return [2.0 ** (-8.0 * (i + 1) / n) for i in range(n)]


def _tile_lanes(x, n):
    return x if n == 1 else jnp.concatenate([x] * n, axis=1)


def _ada_kernel(c_ref, w_ref, b_ref, o_ref):
    c = c_ref[...]
    cond = c * jax.nn.sigmoid(c)
    o_ref[0] = jnp.dot(cond, w_ref[0], preferred_element_type=F32,
                       precision=lax.Precision.HIGHEST) + b_ref[0]


def _ada_mod(c, ada_w, ada_b):
    depth, d, n = ada_w.shape
    b = c.shape[0]
    rows = 8
    tn = 1536
    c_pad = jnp.zeros((rows, d), F32).at[:b].set(c)
    out = pl.pallas_call(
        _ada_kernel,
        grid=(depth, n // tn),
        in_specs=[pl.BlockSpec((rows, d), lambda l, j: (0, 0)),
                  pl.BlockSpec((1, d, tn), lambda l, j: (l, 0, j)),
                  pl.BlockSpec((1, 1, tn), lambda l, j: (l, 0, j))],
        out_specs=pl.BlockSpec((1, rows, tn), lambda l, j: (l, 0, j)),
        out_shape=jax.ShapeDtypeStruct((depth, rows, n), F32),
        compiler_params=pltpu.CompilerParams(
            dimension_semantics=("arbitrary", "arbitrary"), vmem_limit_bytes=VMEM_LIMIT),
        name="ada_mod",
    )(c_pad, ada_w, ada_b.reshape(depth, 1, n))
    return out[:, :b]


def _modulated_norm(x, g, sc, sh):
    ms = jnp.mean(x * x, axis=-1, keepdims=True)
    return (x * lax.rsqrt(ms + NORM_EPS) * g) * (1.0 + sc) + sh


def _proj_kernel(x_ref, g_ref, sc_ref, sh_ref, w_ref, bd_ref, gain_ref, *out_refs, groups):
    h = _modulated_norm(x_ref[0], g_ref[...], sc_ref[0], sh_ref[0]).astype(BF16)
    bd = bd_ref[...]
    for (c0, width, parts) in groups:
        y = jnp.dot(h, w_ref[:, c0:c0 + width], preferred_element_type=F32)
        for (p0, pw, oi, o0, mode) in parts:
            for s in range(0, pw, LANES):
                ys = y[:, p0 + s:p0 + s + LANES]
                col = c0 + p0 + s
                if mode == "norm":
                    ss = jnp.dot((ys * ys).astype(BF16), bd, preferred_element_type=F32)
                    ys = ys * lax.rsqrt(ss * (1.0 / HEAD_DIM) + NORM_EPS)
                if mode != "plain":
                    ys = ys * gain_ref[:, col:col + LANES]
                out_refs[oi][0, :, o0 + s:o0 + s + LANES] = ys.astype(out_refs[oi].dtype)


def _proj(x, g, sc, sh, w, gain, groups, out_defs, tm):
    bsz, s, d = x.shape
    c = w.shape[1]
    r = lax.broadcasted_iota(jnp.int32, (LANES, LANES), 0) // HEAD_DIM
    cc = lax.broadcasted_iota(jnp.int32, (LANES, LANES), 1) // HEAD_DIM
    bd = (r == cc).astype(BF16)
    const = lambda b, i: (0, 0)
    return pl.pallas_call(
        functools.partial(_proj_kernel, groups=groups),
        grid=(bsz, s // tm),
        in_specs=[pl.BlockSpec((1, tm, d), lambda b, i: (b, i, 0)),
                  pl.BlockSpec((1, d), const),
                  pl.BlockSpec((1, 1, d), lambda b, i: (b, 0, 0)),
                  pl.BlockSpec((1, 1, d), lambda b, i: (b, 0, 0)),
                  pl.BlockSpec((d, c), const),
                  pl.BlockSpec((LANES, LANES), const),
                  pl.BlockSpec((1, c), const)],
        out_specs=[pl.BlockSpec((1, tm, wd), lambda b, i: (b, i, 0)) for wd, _ in out_defs],
        out_shape=[jax.ShapeDtypeStruct((bsz, s, wd), dt) for wd, dt in out_defs],
        compiler_params=pltpu.CompilerParams(
            dimension_semantics=("arbitrary", "arbitrary"), vmem_limit_bytes=VMEM_LIMIT),
        name="in_proj",
    )(x, g.reshape(1, d), sc, sh, w, bd, gain)


def _diff_attn_kernel(q_ref, k_ref, v_ref, lam_ref, subln_ref, o_ref,
                      qs_ref, m_ref, l_ref, acc_ref, *, t, heads, slopes, lam_init):
    i = pl.program_id(1)
    lane = lax.broadcasted_iota(jnp.int32, (t, LANES), 1)
    for h in range(heads):
        q = q_ref[0, :, h * LANES:(h + 1) * LANES]
        zero = jnp.zeros_like(q)
        qs_ref[h, :t] = jnp.where(lane < HEAD_DIM, q, zero)
        qs_ref[h, t:] = jnp.where(lane >= HEAD_DIM, q, zero)
    m_ref[...] = jnp.full(m_ref.shape, NEG_BIG, F32)
    l_ref[...] = jnp.zeros(l_ref.shape, F32)
    acc_ref[...] = jnp.zeros(acc_ref.shape, F32)

    def step(start, width, masked):
        col = lax.broadcasted_iota(jnp.int32, (1, width), 1)
        rel = (start - i * t + col).astype(F32)
        if masked:
            r = lax.broadcasted_iota(jnp.int32, (t, width), 0)
            c = lax.broadcasted_iota(jnp.int32, (t, width), 1)
            keep = r >= c
            keep = jnp.concatenate([keep, keep], axis=0)
        for h in range(heads):
            kc = k_ref[0, pl.ds(start, width), h * LANES:(h + 1) * LANES]
            vc = v_ref[0, pl.ds(start, width), h * LANES:(h + 1) * LANES]
            s = lax.dot_general(qs_ref[h], kc, _NT, preferred_element_type=F32)
            s = s + (slopes[h] * LOG2E) * rel
            if masked:
                s = jnp.where(keep, s, NEG_BIG)
            m_prev = m_ref[h]
            m_next = jnp.maximum(m_prev, jnp.max(s, axis=1, keepdims=True))
            alpha = jnp.exp2(m_prev - m_next)
            p = jnp.exp2(s - _tile_lanes(m_next, width // LANES))
            l_ref[h] = alpha * l_ref[h] + jnp.sum(p, axis=1, keepdims=True)
            acc_ref[h] = alpha * acc_ref[h] + jnp.dot(p.astype(BF16), vc,
                                                      preferred_element_type=F32)
            m_ref[h] = m_next

    def body(j, carry):
        step(pl.multiple_of(j * (2 * t), 2 * t), 2 * t, False)
        return carry

    lax.fori_loop(0, i // 2, body, 0)

    @pl.when(i % 2 == 1)
    def _():
        step(pl.multiple_of((i - 1) * t, t), t, False)

    step(pl.multiple_of(i * t, t), t, True)

    lam_v = lam_ref[...]
    s1 = jnp.sum(lam_v[0:1] * lam_v[1:2], axis=-1, keepdims=True)
    s2 = jnp.sum(lam_v[2:3] * lam_v[3:4], axis=-1, keepdims=True)
    lam = jnp.exp(s1) - jnp.exp(s2) + lam_init
    for h in range(heads):
        o = acc_ref[h] / l_ref[h]
        y = o[:t] - lam * o[t:]
        ms = jnp.mean(y * y, axis=-1, keepdims=True)
        y = (y * lax.rsqrt(ms + NORM_EPS) * subln_ref[...]) * (1.0 - lam_init)
        o_ref[0, :, h * LANES:(h + 1) * LANES] = y.astype(o_ref.dtype)


def _diff_attention(q, k, v, lam_vecs, subln, layer_idx, t):
    bsz, s, width = q.shape
    heads = width // LANES
    lam_init = 0.8 - 0.6 * math.exp(-0.3 * layer_idx)
    kern = functools.partial(_diff_attn_kernel, t=t, heads=heads,
                             slopes=_alibi_slopes(heads), lam_init=lam_init)
    return pl.pallas_call(
        kern,
        grid=(bsz, s // t),
        in_specs=[pl.BlockSpec((1, t, width), lambda b, i: (b, i, 0)),
                  pl.BlockSpec((1, s, width), lambda b, i: (b, 0, 0)),
                  pl.BlockSpec((1, s, width), lambda b, i: (b, 0, 0)),
                  pl.BlockSpec((4, HEAD_DIM), lambda b, i: (0, 0)),
                  pl.BlockSpec((1, LANES), lambda b, i: (0, 0))],
        out_specs=pl.BlockSpec((1, t, width), lambda b, i: (b, i, 0)),
        out_shape=jax.ShapeDtypeStruct((bsz, s, width), BF16),
        scratch_shapes=[pltpu.VMEM((heads, 2 * t, LANES), BF16),
                        pltpu.VMEM((heads, 2 * t, LANES), F32),
                        pltpu.VMEM((heads, 2 * t, LANES), F32),
                        pltpu.VMEM((heads, 2 * t, LANES), F32)],
        compiler_params=pltpu.CompilerParams(
            dimension_semantics=("arbitrary", "arbitrary"), vmem_limit_bytes=VMEM_LIMIT),
        name="diff_attn",
    )(q, k, v, lam_vecs, subln.reshape(1, LANES))


KCH = 256
ONES_ROWS = 16


def _diff_attn_t_kernel(q_ref, k_ref, vt_ref, lam_ref, subln_ref, o_ref,
                        qs_ref, m_ref, acc_ref, *, t, heads, slopes, lam_init):
    i = pl.program_id(1)
    q0 = i * t
    dv = vt_ref.shape[3] - ONES_ROWS
    lane = lax.broadcasted_iota(jnp.int32, (t, LANES), 1)
    for h in range(heads):
        q = q_ref[0, :, h * LANES:(h + 1) * LANES]
        zero = jnp.zeros_like(q)
        qs_ref[h, :t] = jnp.where(lane < HEAD_DIM, q, zero)
        qs_ref[h, t:] = jnp.where(lane >= HEAD_DIM, q, zero)
    m_ref[...] = jnp.full(m_ref.shape, NEG_BIG, F32)
    acc_ref[...] = jnp.zeros(acc_ref.shape, F32)

    def step(start, nck, masked):
        width = nck * KCH
        krow = lax.broadcasted_iota(jnp.int32, (width, LANES), 0)
        rel = (start - q0 + krow).astype(F32)
        if masked:
            kr = lax.broadcasted_iota(jnp.int32, (width, t), 0)
            qc = lax.broadcasted_iota(jnp.int32, (width, t), 1)
            keep = kr <= qc
            keep = jnp.concatenate([keep, keep], axis=1)
        c0 = start // KCH
        for h in range(heads):
            kc = k_ref[0, pl.ds(start, width), h * LANES:(h + 1) * LANES]
            s = lax.dot_general(kc, qs_ref[h], _NT, preferred_element_type=F32)
            s = s + _tile_lanes((slopes[h] * LOG2E) * rel, 2 * t // LANES)
            if masked:
                s = jnp.where(keep, s, NEG_BIG)
            m_prev = m_ref[h]
            m_next = jnp.maximum(m_prev, jnp.max(s, axis=0, keepdims=True))
            alpha = jnp.exp2(m_prev - m_next)
            p = jnp.exp2(s - m_next).astype(BF16)
            pv = None
            for c in range(nck):
                part = jnp.dot(vt_ref[0, h, c0 + c], p[c * KCH:(c + 1) * KCH],
                               preferred_element_type=F32)
                pv = part if pv is None else pv + part
            acc_ref[h] = alpha * acc_ref[h] + pv
            m_ref[h] = m_next

    def body(j, carry):
        step(pl.multiple_of(j * (2 * KCH), 2 * KCH), 2, False)
        return carry

    nfull = q0 // (2 * KCH)
    lax.fori_loop(0, nfull, body, 0)
    rem = (q0 - nfull * (2 * KCH)) // KCH

    @pl.when(rem == 1)
    def _():
        step(pl.multiple_of(q0 - KCH, KCH), 1, False)

    step(pl.multiple_of(q0, KCH), t // KCH, True)

    lam_v = lam_ref[...]
    s1 = jnp.sum(lam_v[0:1] * lam_v[1:2], axis=-1, keepdims=True)
    s2 = jnp.sum(lam_v[2:3] * lam_v[3:4], axis=-1, keepdims=True)
    lam = jnp.exp(s1) - jnp.exp(s2) + lam_init
    for h in range(heads):
        acc = acc_ref[h]
        o = acc[:dv] / acc[dv:dv + 1]
        y = (o[:, :t] - lam * o[:, t:]).T
        ms = jnp.mean(y * y, axis=-1, keepdims=True)
        y = (y * lax.rsqrt(ms + NORM_EPS) * subln_ref[...]) * (1.0 - lam_init)
        o_ref[0, :, h * LANES:(h + 1) * LANES] = y.astype(o_ref.dtype)


def _diff_attention_t(q, k, v, lam_vecs, subln, layer_idx, t):
    bsz, s, width = q.shape
    heads = width // LANES
    lam_init = 0.8 - 0.6 * math.exp(-0.3 * layer_idx)
    vt = v.reshape(bsz, s // KCH, KCH, heads, LANES).transpose(0, 3, 1, 4, 2)
    vt = jnp.concatenate([vt, jnp.ones(vt.shape[:3] + (ONES_ROWS, KCH), vt.dtype)], axis=3)
    kern = functools.partial(_diff_attn_t_kernel, t=t, heads=heads,
                             slopes=_alibi_slopes(heads), lam_init=lam_init)
    return pl.pallas_call(
        kern,
        grid=(bsz, s // t),
        in_specs=[pl.BlockSpec((1, t, width), lambda b, i: (b, i, 0)),
                  pl.BlockSpec((1, s, width), lambda b, i: (b, 0, 0)),
                  pl.BlockSpec((1,) + vt.shape[1:], lambda b, i: (b, 0, 0, 0, 0)),
                  pl.BlockSpec((4, HEAD_DIM), lambda b, i: (0, 0)),
                  pl.BlockSpec((1, LANES), lambda b, i: (0, 0))],
        out_specs=pl.BlockSpec((1, t, width), lambda b, i: (b, i, 0)),
        out_shape=jax.ShapeDtypeStruct((bsz, s, width), BF16),
        scratch_shapes=[pltpu.VMEM((heads, 2 * t, LANES), BF16),
                        pltpu.VMEM((heads, 1, 2 * t), F32),
                        pltpu.VMEM((heads, LANES + ONES_ROWS, 2 * t), F32)],
        compiler_params=pltpu.CompilerParams(
            dimension_semantics=("arbitrary", "arbitrary"), vmem_limit_bytes=VMEM_LIMIT),
        name="diff_attn",
    )(q, k, vt, lam_vecs, subln.reshape(1, LANES))


def _diff_attn_b_kernel(q_ref, k_ref, v_ref, lam_ref, subln_ref, o_ref,
                        qs_ref, m_ref, acc_ref, *, t, kw, heads, slopes, lam_init):
    i = pl.program_id(1)
    q0 = i * t
    lane = lax.broadcasted_iota(jnp.int32, (t, LANES), 1)
    for h in range(heads):
        q = q_ref[0, :, h * LANES:(h + 1) * LANES]
        zero = jnp.zeros_like(q)
        qs_ref[h, :t] = jnp.where(lane < HEAD_DIM, q, zero)
        qs_ref[h, t:] = jnp.where(lane >= HEAD_DIM, q, zero)
    m_ref[...] = jnp.full(m_ref.shape, NEG_BIG, F32)
    acc_ref[...] = jnp.zeros(acc_ref.shape, F32)

    def step(start, width, masked):
        col = lax.broadcasted_iota(jnp.int32, (1, width), 1)
        rel = (start - q0 + col).astype(F32)
        nsl = width // LANES
        logits, values = [], []
        for h in range(heads):
            kc = k_ref[0, pl.ds(start, width), h * LANES:(h + 1) * LANES]
            vc = v_ref[0, pl.ds(start, width), h * LANES:(h + 1) * LANES]
            logits.append(lax.dot_general(qs_ref[h], kc, _NT, preferred_element_type=F32))
            values.append(jnp.concatenate([vc, jnp.ones_like(vc)], axis=1))
        for h in range(heads):
            alibi = (slopes[h] * LOG2E) * rel
            for r0 in range(0, 2 * t, ROWS):
                s = logits[h][r0:r0 + ROWS] + alibi
                if masked:
                    r = lax.broadcasted_iota(jnp.int32, (ROWS, width), 0) + (r0 % t)
                    c = lax.broadcasted_iota(jnp.int32, (ROWS, width), 1)
                    s = jnp.where(r >= c, s, NEG_BIG)
                m_prev = m_ref[h, r0:r0 + ROWS]
                m_next = jnp.maximum(m_prev, jnp.max(s, axis=1, keepdims=True))
                alpha = jnp.exp2(m_prev - m_next)
                p = jnp.exp2(s - _tile_lanes(m_next, nsl))
                acc_ref[h, r0:r0 + ROWS] = (
                    _tile_lanes(alpha, 2) * acc_ref[h, r0:r0 + ROWS]
                    + jnp.dot(p.astype(BF16), values[h], preferred_element_type=F32))
                m_ref[h, r0:r0 + ROWS] = m_next

    def body(j, carry):
        step(pl.multiple_of(j * kw, kw), kw, False)
        return carry

    nfull = q0 // kw
    lax.fori_loop(0, nfull, body, 0)

    if t < kw:
        @pl.when(q0 - nfull * kw > 0)
        def _():
            step(pl.multiple_of(q0 - t, t), t, False)

    step(pl.multiple_of(q0, t), t, True)

    lam_v = lam_ref[...]
    s1 = jnp.sum(lam_v[0:1] * lam_v[1:2], axis=-1, keepdims=True)
    s2 = jnp.sum(lam_v[2:3] * lam_v[3:4], axis=-1, keepdims=True)
    lam = jnp.exp(s1) - jnp.exp(s2) + lam_init
    for h in range(heads):
        acc = acc_ref[h]
        o = acc[:, :LANES] / acc[:, LANES:]
        y = o[:t] - lam * o[t:]
        ms = jnp.mean(y * y, axis=-1, keepdims=True)
        y = (y * lax.rsqrt(ms + NORM_EPS) * subln_ref[...]) * (1.0 - lam_init)
        o_ref[0, :, h * LANES:(h + 1) * LANES] = y.astype(o_ref.dtype)


def _diff_attention_b(q, k, v, lam_vecs, subln, layer_idx, t, kw):
    bsz, s, width = q.shape
    heads = width // LANES
    assert kw in (t, 2 * t)
    lam_init = 0.8 - 0.6 * math.exp(-0.3 * layer_idx)
    kern = functools.partial(_diff_attn_b_kernel, t=t, kw=kw, heads=heads,
                             slopes=_alibi_slopes(heads), lam_init=lam_init)
    return pl.pallas_call(
        kern,
        grid=(bsz, s // t),
        in_specs=[pl.BlockSpec((1, t, width), lambda b, i: (b, i, 0)),
                  pl.BlockSpec((1, s, width), lambda b, i: (b, 0, 0)),
                  pl.BlockSpec((1, s, width), lambda b, i: (b, 0, 0)),
                  pl.BlockSpec((4, HEAD_DIM), lambda b, i: (0, 0)),
                  pl.BlockSpec((1, LANES), lambda b, i: (0, 0))],
        out_specs=pl.BlockSpec((1, t, width), lambda b, i: (b, i, 0)),
        out_shape=jax.ShapeDtypeStruct((bsz, s, width), BF16),
        scratch_shapes=[pltpu.VMEM((heads, 2 * t, LANES), BF16),
                        pltpu.VMEM((heads, 2 * t, LANES), F32),
                        pltpu.VMEM((heads, 2 * t, 2 * LANES), F32)],
        compiler_params=pltpu.CompilerParams(
            dimension_semantics=("arbitrary", "arbitrary"), vmem_limit_bytes=VMEM_LIMIT),
        name="diff_attn",
    )(q, k, v, lam_vecs, subln.reshape(1, LANES))


def _stack_heads(x, heads):
    lane = lax.broadcasted_iota(jnp.int32, (x.shape[0], LANES), 1)
    lo = lane < HEAD_DIM
    parts = []
    for h in range(heads):
        blk = x[:, (h // 2) * LANES:(h // 2 + 1) * LANES]
        keep = lo if h % 2 == 0 else jnp.logical_not(lo)
        parts.append(jnp.where(keep, blk, jnp.zeros_like(blk)))
    return jnp.concatenate(parts, axis=0)


def _key_to_float(k):
    return lax.bitcast_convert_type(k ^ ((k >> 31) & jnp.int32(0x7FFFFFFF)), F32)


def _dsa_kernel(bq_ref, iq_ref, iw_ref, ik_ref, bk_ref, bv_ref, o_ref,
                sc_ref, iqs_ref, bqs_ref, wb_ref, m_ref, acc_ref,
                *, tq, tk, heads, topk, slopes):
    i = pl.program_id(1)
    q0 = i * tq
    nch = q0 // tk + 1
    nsl = tk // LANES
    row = lax.broadcasted_iota(jnp.int32, (tq, tk), 0)
    col = lax.broadcasted_iota(jnp.int32, (tq, tk), 1)

    lane = lax.broadcasted_iota(jnp.int32, (tq, LANES), 1)
    iw = iw_ref[0]
    for h in range(heads):
        keep = (lane < HEAD_DIM) if h % 2 == 0 else (lane >= HEAD_DIM)
        tile = slice((h // 2) * LANES, (h // 2 + 1) * LANES)
        iqb = iq_ref[0, :, tile]
        bqb = bq_ref[0, :, tile]
        iqs_ref[h] = jnp.where(keep, iqb, jnp.zeros_like(iqb))
        bqs_ref[h * tq:(h + 1) * tq] = jnp.where(keep, bqb, jnp.zeros_like(bqb))
        wb_ref[h] = jnp.broadcast_to(iw[:, h:h + 1], (tq, LANES))

    def score_body(j, carry):
        start = pl.multiple_of(j * tk, tk)
        ikc = ik_ref[0, pl.ds(start, tk), :]
        sc = None
        for h in range(heads):
            logit = lax.dot_general(iqs_ref[h], ikc, _NT, preferred_element_type=F32)
            term = jnp.maximum(logit, 0.0) * _tile_lanes(wb_ref[h], nsl)
            sc = term if sc is None else sc + term
        sc = sc + 0.0
        causal = (start + col) <= (q0 + row)
        sc_ref[j] = jnp.where(causal, sc, -jnp.inf)
        return carry

    lax.fori_loop(0, nch, score_body, 0)

    def count(pred_fn, thr):
        parts = []
        for r0 in range(0, tq, ROWS):
            thr_b = thr[r0:r0 + ROWS]

            def body(j, cnt):
                sc = sc_ref[j, r0:r0 + ROWS, :]
                for s in range(nsl):
                    cnt = cnt + jnp.where(pred_fn(sc[:, s * LANES:(s + 1) * LANES], thr_b), 1.0, 0.0)
                return cnt

            parts.append(lax.fori_loop(0, nch, body, jnp.zeros((ROWS, LANES), F32)))
        cnt = jnp.concatenate(parts, axis=0)
        return jnp.broadcast_to(jnp.sum(cnt, axis=1, keepdims=True), (tq, LANES))

    ge = lambda a, b: a >= b
    gt = lambda a, b: a > b

    def bit_body(b, key):
        cand = key + lax.shift_left(jnp.int32(1), 31 - b)
        return jnp.where(count(ge, _key_to_float(cand)) >= float(topk), cand, key)

    int_min = jnp.int32(-2 ** 31)
    key = lax.fori_loop(0, 32, bit_body, jnp.full((tq, LANES), int_min, jnp.int32))
    thr = jnp.where(key <= KEY_NEG_INF, -jnp.inf, _key_to_float(key))
    n_ge = count(ge, thr)
    n_gt = count(gt, thr)
    need = float(topk) - n_gt
    has_excess_ties = jnp.max(jnp.where(n_ge > float(topk), 1.0, 0.0)) > 0.0
    thr_t = _tile_lanes(thr, nsl)

    @pl.when(jnp.logical_not(has_excess_ties))
    def _():
        def body(j, carry):
            start = pl.multiple_of(j * tk, tk)
            keep = (sc_ref[j] >= thr_t) & ((start + col) <= (q0 + row))
            sc_ref[j] = jnp.where(keep, 0.0, NEG_BIG)
            return carry
        lax.fori_loop(0, nch, body, 0)

    @pl.when(has_excess_ties)
    def _():
        r2 = lax.broadcasted_iota(jnp.int32, (tk, tk), 0)
        c2 = lax.broadcasted_iota(jnp.int32, (tk, tk), 1)
        upper = jnp.where(r2 <= c2, 1.0, 0.0).astype(BF16)
        need_t = _tile_lanes(need, nsl)

        def body(j, seen):
            start = pl.multiple_of(j * tk, tk)
            sc = sc_ref[j]
            eq = sc == thr_t
            eqf = jnp.where(eq, 1.0, 0.0)
            rank = _tile_lanes(seen, nsl) + jnp.dot(eqf.astype(BF16), upper,
                                                    preferred_element_type=F32)
            keep = ((sc > thr_t) | (eq & (rank <= need_t))) & ((start + col) <= (q0 + row))
            sc_ref[j] = jnp.where(keep, 0.0, NEG_BIG)
            return seen + jnp.broadcast_to(jnp.sum(eqf, axis=1, keepdims=True), (tq, LANES))

        lax.fori_loop(0, nch, body, jnp.zeros((tq, LANES), F32))

    m_ref[...] = jnp.full(m_ref.shape, NEG_BIG, F32)
    acc_ref[...] = jnp.zeros(acc_ref.shape, F32)
    kcol = lax.broadcasted_iota(jnp.int32, (1, tk), 1)
    vlane = lax.broadcasted_iota(jnp.int32, (tk, LANES), 1)

    def attn_body(j, carry):
        start = pl.multiple_of(j * tk, tk)
        kc = bk_ref[0, pl.ds(start, tk), :]
        vc = bv_ref[0, pl.ds(start, tk), :]
        one = jnp.ones_like(vc)
        v_even = jnp.where(vlane < HEAD_DIM, vc, one)
        v_odd = jnp.where(vlane < HEAD_DIM, one, vc)
        logit = lax.dot_general(bqs_ref[...], kc, _NT, preferred_element_type=F32)
        rel = (start - q0 + kcol).astype(F32)
        for h in range(heads):
            alibi = (slopes[h] * LOG2E) * rel
            for r0 in range(0, tq, ROWS):
                rs = slice(h * tq + r0, h * tq + r0 + ROWS)
                s = logit[rs] + sc_ref[j, r0:r0 + ROWS, :] + alibi
                m_prev = m_ref[rs]
                m_next = jnp.maximum(m_prev, jnp.max(s, axis=1, keepdims=True))
                alpha = jnp.exp2(m_prev - m_next)
                p = jnp.exp2(s - _tile_lanes(m_next, nsl))
                acc_ref[rs] = alpha * acc_ref[rs] + jnp.dot(
                    p.astype(BF16), v_even if h % 2 == 0 else v_odd, preferred_element_type=F32)
                m_ref[rs] = m_next
        return carry

    lax.fori_loop(0, nch, attn_body, 0)

    for g in range(heads // 2):
        a = acc_ref[(2 * g) * tq:(2 * g + 1) * tq]
        b = acc_ref[(2 * g + 1) * tq:(2 * g + 2) * tq]
        oa = a / pltpu.roll(a, HEAD_DIM, axis=1)
        ob = b / pltpu.roll(b, HEAD_DIM, axis=1)
        o_ref[0, :, g * LANES:(g + 1) * LANES] = jnp.where(lane < HEAD_DIM, oa, ob).astype(o_ref.dtype)


def _dsa_attention(bq, iq, iwp, ik2, bk2, bv2, tq, tk):
    bsz, s, width = bq.shape
    heads = width // HEAD_DIM
    topk = min(TOPK_MAX, s // 4)
    nck = s // tk
    kern = functools.partial(_dsa_kernel, tq=tq, tk=tk, heads=heads, topk=topk,
                             slopes=_alibi_slopes(heads))
    qspec = lambda w: pl.BlockSpec((1, tq, w), lambda b, i: (b, i, 0))
    kspec = pl.BlockSpec((1, s, LANES), lambda b, i: (b, 0, 0))
    return pl.pallas_call(
        kern,
        grid=(bsz, s // tq),
        in_specs=[qspec(width), qspec(width), qspec(LANES), kspec, kspec, kspec],
        out_specs=pl.BlockSpec((1, tq, width), lambda b, i: (b, i, 0)),
        out_shape=jax.ShapeDtypeStruct((bsz, s, width), BF16),
        scratch_shapes=[pltpu.VMEM((nck, tq, tk), F32),
                        pltpu.VMEM((heads, tq, LANES), BF16),
                        pltpu.VMEM((heads * tq, LANES), BF16),
                        pltpu.VMEM((heads, tq, LANES), F32),
                        pltpu.VMEM((heads * tq, LANES), F32),
                        pltpu.VMEM((heads * tq, LANES), F32)],
        compiler_params=pltpu.CompilerParams(
            dimension_semantics=("arbitrary", "arbitrary"), vmem_limit_bytes=VMEM_LIMIT),
        name="dsa_attn",
    )(bq, iq, iwp, ik2, bk2, bv2)


SUB = 8
NACC = 4


def _dsa_t_kernel(bq_ref, iq_ref, iw_ref, ik_ref, bk_ref, bv_ref, o_ref,
                  sc_ref, bias_ref, iqs_ref, bqs_ref, wt_ref, m_ref, acc_ref,
                  *, tq, tk, heads, topk, slopes):
    i = pl.program_id(1)
    q0 = i * tq
    nch = q0 // tk + 1
    nsl = tk // LANES
    krow = lax.broadcasted_iota(jnp.int32, (tk, tq), 0)
    qcol = lax.broadcasted_iota(jnp.int32, (tk, tq), 1)

    lane = lax.broadcasted_iota(jnp.int32, (tq, LANES), 1)
    for h in range(heads):
        keep = (lane < HEAD_DIM) if h % 2 == 0 else (lane >= HEAD_DIM)
        tile = slice((h // 2) * LANES, (h // 2 + 1) * LANES)
        iqb = iq_ref[0, :, tile]
        bqb = bq_ref[0, :, tile]
        iqs_ref[h] = jnp.where(keep, iqb, jnp.zeros_like(iqb))
        bqs_ref[h * tq:(h + 1) * tq] = jnp.where(keep, bqb, jnp.zeros_like(bqb))
    wt_ref[...] = iw_ref[0].T

    def score_body(j, carry):
        start = pl.multiple_of(j * tk, tk)
        ikc = ik_ref[0, pl.ds(start, tk), :]
        sc = None
        for h in range(heads):
            logit = lax.dot_general(ikc, iqs_ref[h], _NT, preferred_element_type=F32)
            term = jnp.maximum(logit, 0.0) * wt_ref[h:h + 1, :]
            sc = term if sc is None else sc + term
        sc = sc + 0.0
        causal = (start + krow) <= (q0 + qcol)
        sc_ref[j] = jnp.where(causal, sc, -jnp.inf)
        return carry

    lax.fori_loop(0, nch, score_body, 0)

    def count(pred_fn, thr):
        def body(j, accs):
            accs = list(accs)
            for r in range(tk // SUB):
                blk = sc_ref[j, r * SUB:(r + 1) * SUB, :]
                accs[r % NACC] = accs[r % NACC] + jnp.where(pred_fn(blk, thr), 1.0, 0.0)
            return tuple(accs)

        accs = lax.fori_loop(0, nch, body, tuple(jnp.zeros((SUB, tq), F32) for _ in range(NACC)))
        cnt = accs[0]
        for a in accs[1:]:
            cnt = cnt + a
        return jnp.broadcast_to(jnp.sum(cnt, axis=0, keepdims=True), (SUB, tq))

    ge = lambda a, b: a >= b
    gt = lambda a, b: a > b

    def bit_body(b, key):
        cand = key + lax.shift_left(jnp.int32(1), 31 - b)
        return jnp.where(count(ge, _key_to_float(cand)) >= float(topk), cand, key)

    int_min = jnp.int32(-2 ** 31)
    key = lax.fori_loop(0, 32, bit_body, jnp.full((SUB, tq), int_min, jnp.int32))
    thr = jnp.where(key <= KEY_NEG_INF, -jnp.inf, _key_to_float(key))
    n_ge = count(ge, thr)
    n_gt = count(gt, thr)
    need = (float(topk) - n_gt)[0:1]
    has_excess_ties = jnp.max(jnp.where(n_ge > float(topk), 1.0, 0.0)) > 0.0
    thr_row = thr[0:1]

    @pl.when(jnp.logical_not(has_excess_ties))
    def _():
        def body(j, carry):
            start = pl.multiple_of(j * tk, tk)
            keep = (sc_ref[j] >= thr_row) & ((start + krow) <= (q0 + qcol))
            bias_ref[j] = jnp.where(keep, 0.0, NEG_BIG).T
            return carry
        lax.fori_loop(0, nch, body, 0)

    @pl.when(has_excess_ties)
    def _():
        r2 = lax.broadcasted_iota(jnp.int32, (tk, tk), 0)
        c2 = lax.broadcasted_iota(jnp.int32, (tk, tk), 1)
        lower = jnp.where(c2 <= r2, 1.0, 0.0).astype(BF16)

        def body(j, seen):
            start = pl.multiple_of(j * tk, tk)
            sc = sc_ref[j]
            eq = sc == thr_row
            eqf = jnp.where(eq, 1.0, 0.0)
            rank = seen + jnp.dot(lower, eqf.astype(BF16), preferred_element_type=F32)
            keep = ((sc > thr_row) | (eq & (rank <= need))) & ((start + krow) <= (q0 + qcol))
            bias_ref[j] = jnp.where(keep, 0.0, NEG_BIG).T
            return seen + jnp.sum(eqf, axis=0, keepdims=True)

        lax.fori_loop(0, nch, body, jnp.zeros((1, tq), F32))

    m_ref[...] = jnp.full(m_ref.shape, NEG_BIG, F32)
    acc_ref[...] = jnp.zeros(acc_ref.shape, F32)
    kcol = lax.broadcasted_iota(jnp.int32, (1, tk), 1)
    vlane = lax.broadcasted_iota(jnp.int32, (tk, LANES), 1)

    def attn_body(j, carry):
        start = pl.multiple_of(j * tk, tk)
        kc = bk_ref[0, pl.ds(start, tk), :]
        vc = bv_ref[0, pl.ds(start, tk), :]
        one = jnp.ones_like(vc)
        v_even = jnp.where(vlane < HEAD_DIM, vc, one)
        v_odd = jnp.where(vlane < HEAD_DIM, one, vc)
        logit = lax.dot_general(bqs_ref[...], kc, _NT, preferred_element_type=F32)
        rel = (start - q0 + kcol).astype(F32)
        for h in range(heads):
            alibi = (slopes[h] * LOG2E) * rel
            for r0 in range(0, tq, ROWS):
                rs = slice(h * tq + r0, h * tq + r0 + ROWS)
                s = logit[rs] + bias_ref[j, r0:r0 + ROWS, :] + alibi
                m_prev = m_ref[rs]
                m_next = jnp.maximum(m_prev, jnp.max(s, axis=1, keepdims=True))
                alpha = jnp.exp2(m_prev - m_next)
                p = jnp.exp2(s - _tile_lanes(m_next, nsl))
                acc_ref[rs] = alpha * acc_ref[rs] + jnp.dot(
                    p.astype(BF16), v_even if h % 2 == 0 else v_odd, preferred_element_type=F32)
                m_ref[rs] = m_next
        return carry

    lax.fori_loop(0, nch, attn_body, 0)

    for g in range(heads // 2):
        a = acc_ref[(2 * g) * tq:(2 * g + 1) * tq]
        b = acc_ref[(2 * g + 1) * tq:(2 * g + 2) * tq]
        oa = a / pltpu.roll(a, HEAD_DIM, axis=1)
        ob = b / pltpu.roll(b, HEAD_DIM, axis=1)
        o_ref[0, :, g * LANES:(g + 1) * LANES] = jnp.where(lane < HEAD_DIM, oa, ob).astype(o_ref.dtype)


def _dsa_attention_t(bq, iq, iwp, ik2, bk2, bv2, tq, tk):
    bsz, s, width = bq.shape
    heads = width // HEAD_DIM
    topk = min(TOPK_MAX, s // 4)
    nck = s // tk
    kern = functools.partial(_dsa_t_kernel, tq=tq, tk=tk, heads=heads, topk=topk,
                             slopes=_alibi_slopes(heads))
    qspec = lambda w: pl.BlockSpec((1, tq, w), lambda b, i: (b, i, 0))
    kspec = pl.BlockSpec((1, s, LANES), lambda b, i: (b, 0, 0))
    return pl.pallas_call(
        kern,
        grid=(bsz, s // tq),
        in_specs=[qspec(width), qspec(width), qspec(LANES), kspec, kspec, kspec],
        out_specs=pl.BlockSpec((1, tq, width), lambda b, i: (b, i, 0)),
        out_shape=jax.ShapeDtypeStruct((bsz, s, width), BF16),
        scratch_shapes=[pltpu.VMEM((nck, tk, tq), F32),
                        pltpu.VMEM((nck, tq, tk), F32),
                        pltpu.VMEM((heads, tq, LANES), BF16),
                        pltpu.VMEM((heads * tq, LANES), BF16),
                        pltpu.VMEM((LANES, tq), F32),
                        pltpu.VMEM((heads * tq, LANES), F32),
                        pltpu.VMEM((heads * tq, LANES), F32)],
        compiler_params=pltpu.CompilerParams(
            dimension_semantics=("arbitrary", "arbitrary"), vmem_limit_bytes=VMEM_LIMIT),
        name="dsa_attn",
    )(bq, iq, iwp, ik2, bk2, bv2)


def _swa_kernel(slope_ref, sink_ref, q_ref, kp_ref, kc_ref, vp_ref, vc_ref, o_ref, *, tq, group):
    kv = pl.program_id(1)
    i = pl.program_id(2)
    nk = 2 * WINDOW
    kk = jnp.concatenate([kp_ref[0], kc_ref[0]], axis=0)
    vv = jnp.concatenate([vp_ref[0], vc_ref[0]], axis=0)
    r = lax.broadcasted_iota(jnp.int32, (WINDOW, nk), 0)
    c = lax.broadcasted_iota(jnp.int32, (WINDOW, nk), 1)
    dist = WINDOW + r - c
    band = (dist >= 0) & (dist < WINDOW)
    distf = dist.astype(F32)
    lane = lax.broadcasted_iota(jnp.int32, (WINDOW, LANES), 1)
    for blk in range(tq // WINDOW):
        rows = slice(blk * WINDOW, (blk + 1) * WINDOW)
        qs = _stack_heads(q_ref[0, rows, :], group)
        kb = kk[blk * WINDOW:blk * WINDOW + nk]
        vb = vv[blk * WINDOW:blk * WINDOW + nk]
        logit = lax.dot_general(qs, kb, _NT, preferred_element_type=F32)
        valid = band
        if blk == 0:
            valid = band & (c >= jnp.where(i == 0, WINDOW, 0))
        outs = []
        for g in range(group):
            hq = kv * group + g
            s = logit[g * WINDOW:(g + 1) * WINDOW] - (slope_ref[hq] * LOG2E) * distf
            s = jnp.where(valid, s, NEG_BIG)
            sink = sink_ref[hq] * LOG2E
            m = jnp.maximum(jnp.max(s, axis=1, keepdims=True), sink)
            p = jnp.exp2(s - m)
            denom = jnp.sum(p, axis=1, keepdims=True) + jnp.exp2(sink - m)
            outs.append(jnp.dot(p.astype(BF16), vb, preferred_element_type=F32) / denom)
        for g in range(group // 2):
            o_ref[0, rows, g * LANES:(g + 1) * LANES] = jnp.where(
                lane < HEAD_DIM, outs[2 * g], outs[2 * g + 1]).astype(o_ref.dtype)


def _swa_attention(q, k2, v2, sinks, tq):
    bsz, s, width = q.shape
    heads = width // HEAD_DIM
    kvh = k2.shape[2] // LANES
    group = heads // kvh
    slopes = jnp.asarray(_alibi_slopes(heads), F32)
    r = tq // WINDOW
    prev = lambda b, kv, i: (b, jnp.maximum(i * r - 1, 0), kv)
    cur = lambda b, kv, i: (b, i, kv)
    smem = pl.BlockSpec(memory_space=pltpu.SMEM)
    return pl.pallas_call(
        functools.partial(_swa_kernel, tq=tq, group=group),
        grid=(bsz, kvh, s // tq),
        in_specs=[smem, smem,
                  pl.BlockSpec((1, tq, group * HEAD_DIM), cur),
                  pl.BlockSpec((1, WINDOW, LANES), prev),
                  pl.BlockSpec((1, tq, LANES), cur),
                  pl.BlockSpec((1, WINDOW, LANES), prev),
                  pl.BlockSpec((1, tq, LANES), cur)],
        out_specs=pl.BlockSpec((1, tq, group * HEAD_DIM), cur),
        out_shape=jax.ShapeDtypeStruct((bsz, s, width), BF16),
        compiler_params=pltpu.CompilerParams(
            dimension_semantics=("arbitrary", "arbitrary", "arbitrary"),
            vmem_limit_bytes=VMEM_LIMIT),
        name="swa_attn",
    )(slopes, sinks.astype(F32), q, k2, k2, v2, v2)


def _out_ffn_kernel(*refs, ny):
    x_ref = refs[0]
    y_refs = refs[1:1 + ny]
    wo_ref, g1_ref, n_ref, sc_ref, sh_ref, g2_ref, wg_ref, wu_ref, wd_ref, o_ref = refs[1 + ny:]
    mix = None
    r0 = 0
    for y_ref in y_refs:
        w = y_ref.shape[2]
        part = jnp.dot(y_ref[0], wo_ref[r0:r0 + w, :], preferred_element_type=F32)
        mix = part if mix is None else mix + part
        r0 += w
    x = x_ref[0] + g1_ref[0] * mix
    h = _modulated_norm(x, n_ref[...], sc_ref[0], sh_ref[0]).astype(BF16)
    gate = jnp.dot(h, wg_ref[...], preferred_element_type=F32)
    up = jnp.dot(h, wu_ref[...], preferred_element_type=F32)
    act = (gate * jax.nn.sigmoid(gate) * up).astype(BF16)
    o_ref[0] = x + g2_ref[0] * jnp.dot(act, wd_ref[...], preferred_element_type=F32)


def _out_ffn(x, ys, wo, g1, n, sc, sh, g2, wg, wu, wd, tm):
    bsz, s, d = x.shape
    f = wg.shape[1]
    const = lambda b, i: (0, 0)
    tok = lambda b, i: (b, i, 0)
    mod = lambda b, i: (b, 0, 0)
    once = pl.Buffered(1)
    return pl.pallas_call(
        functools.partial(_out_ffn_kernel, ny=len(ys)),
        grid=(bsz, s // tm),
        in_specs=[pl.BlockSpec((1, tm, d), tok)]
                 + [pl.BlockSpec((1, tm, y.shape[2]), tok) for y in ys]
                 + [pl.BlockSpec((d, d), const, pipeline_mode=once),
                  pl.BlockSpec((1, 1, d), mod),
                  pl.BlockSpec((1, d), const),
                  pl.BlockSpec((1, 1, d), mod),
                  pl.BlockSpec((1, 1, d), mod),
                  pl.BlockSpec((1, 1, d), mod),
                  pl.BlockSpec((d, f), const, pipeline_mode=once),
                  pl.BlockSpec((d, f), const, pipeline_mode=once),
                  pl.BlockSpec((f, d), const, pipeline_mode=once)],
        out_specs=pl.BlockSpec((1, tm, d), tok),
        out_shape=jax.ShapeDtypeStruct((bsz, s, d), F32),
        compiler_params=pltpu.CompilerParams(
            dimension_semantics=("arbitrary", "arbitrary"), vmem_limit_bytes=VMEM_LIMIT),
        name="out_ffn",
    )(x, *ys, wo, g1, n.reshape(1, d), sc, sh, g2, wg, wu, wd)


def _dup(w):
    d, c = w.shape
    w = w.reshape(d, c // HEAD_DIM, 1, HEAD_DIM)
    return jnp.broadcast_to(w, (d, c // HEAD_DIM, 2, HEAD_DIM)).reshape(d, 2 * c)


def _even_layout(w_in, qn_a, kn_a, qn_b, kn_b, d):
    a = d // 2
    sizes = (a, a, a, a, HEAD_DIM, HEAD_DIM, a, HEAD_DIM, a // HEAD_DIM)
    offs = [0]
    for sz in sizes:
        offs.append(offs[-1] + sz)
    aq, ak, av, bq, bk, bv, iq, ik, iw = [w_in[:, offs[n]:offs[n + 1]] for n in range(9)]
    nh = a // HEAD_DIM
    iw_pad = jnp.zeros((d, LANES), w_in.dtype).at[:, :nh].set(iw)
    w = jnp.concatenate([aq, ak, av, bq, iq, _dup(bk), _dup(bv), _dup(ik), iw_pad], axis=1)
    qscale = HEAD_DIM ** -0.5 * LOG2E
    ones = lambda n: jnp.ones((n,), F32)
    gain = jnp.concatenate([
        jnp.tile(qn_a, nh) * qscale, jnp.tile(kn_a, nh), ones(a),
        jnp.tile(qn_b, nh) * qscale, ones(a) * HEAD_DIM ** -0.5,
        jnp.tile(kn_b, 2), ones(LANES), ones(LANES), ones(LANES) * nh ** -0.5])
    groups = (
        (0, a, ((0, a, 0, 0, "norm"),)),
        (a, a, ((0, a, 1, 0, "norm"),)),
        (2 * a, a, ((0, a, 2, 0, "plain"),)),
        (3 * a, a, ((0, a, 3, 0, "norm"),)),
        (4 * a, a, ((0, a, 4, 0, "scale"),)),
        (5 * a, 4 * LANES, ((0, LANES, 5, 0, "norm"), (LANES, LANES, 6, 0, "plain"),
                            (2 * LANES, LANES, 7, 0, "plain"), (3 * LANES, LANES, 8, 0, "scale"))),
    )
    out_defs = ((a, BF16), (a, BF16), (a, BF16), (a, BF16), (a, BF16),
                (LANES, BF16), (LANES, BF16), (LANES, BF16), (LANES, F32))
    return w.astype(BF16), gain.reshape(1, -1), groups, out_defs


def _odd_layout(w_in, qn_c, kn_c, d):
    kvw = d // 4
    q, k, v = w_in[:, :d], w_in[:, d:d + kvw], w_in[:, d + kvw:]
    w = jnp.concatenate([q, _dup(k), _dup(v)], axis=1)
    nh = d // HEAD_DIM
    gain = jnp.concatenate([jnp.tile(qn_c, nh) * (HEAD_DIM ** -0.5 * LOG2E),
                            jnp.tile(kn_c, 2 * kvw // HEAD_DIM), jnp.ones((2 * kvw,), F32)])
    half = d // 2
    groups = (
        (0, half, ((0, half, 0, 0, "norm"),)),
        (half, half, ((0, half, 0, half, "norm"),)),
        (d, 2 * kvw, ((0, 2 * kvw, 1, 0, "norm"),)),
        (d + 2 * kvw, 2 * kvw, ((0, 2 * kvw, 2, 0, "plain"),)),
    )
    out_defs = ((d, BF16), (2 * kvw, BF16), (2 * kvw, BF16))
    return w.astype(BF16), gain.reshape(1, -1), groups, out_defs


def kernel(x, c, ada_w, ada_b, norm_mix, norm_ffn, w_out, ffn_gate, ffn_up, ffn_down,
           w_in_even, qn_a, kn_a, lam_q1, lam_k1, lam_q2, lam_k2, subln_a, qn_b, kn_b,
           w_in_odd, qn_c, kn_c, sinks_c):
    depth, d = norm_mix.shape
    mod = _ada_mod(c, ada_w, ada_b)
    for l in range(depth):
        sh1, sc1, g1, sh2, sc2, g2 = [mod[l, :, None, n * d:(n + 1) * d] for n in range(6)]
        if l % 2 == 0:
            e = l // 2
            w, gain, groups, out_defs = _even_layout(w_in_even[e], qn_a[e], kn_a[e],
                                                     qn_b[e], kn_b[e], d)
            aq, ak, av, bq, iq, bk2, bv2, ik2, iwp = _proj(
                x, norm_mix[l], sc1, sh1, w, gain, groups, out_defs, tm=256)
            lam_vecs = jnp.stack([lam_q1[e], lam_k1[e], lam_q2[e], lam_k2[e]]).astype(F32)
            ya = _diff_attention_b(aq, ak, av, lam_vecs, subln_a[e], l, t=512, kw=512)
            yb = _dsa_attention_t(bq, iq, iwp, ik2, bk2, bv2, tq=512, tk=512)
            ys = (ya, yb)
        else:
            o = l // 2
            w, gain, groups, out_defs = _odd_layout(w_in_odd[o], qn_c[o], kn_c[o], d)
            q, k2, v2 = _proj(x, norm_mix[l], sc1, sh1, w, gain, groups, out_defs, tm=256)
            ys = (_swa_attention(q, k2, v2, sinks_c[o], tq=512),)
        x = _out_ffn(x, ys, w_out[l].astype(BF16), g1, norm_ffn[l], sc2, sh2, g2,
                     ffn_gate[l].astype(BF16), ffn_up[l].astype(BF16),
                     ffn_down[l].astype(BF16), tm=256)
    return x
```

```python
import functools
import math

import jax
import jax.numpy as jnp
from jax import lax
from jax.experimental import pallas as pl
from jax.experimental.pallas import tpu as pltpu

HEAD_DIM = 64
LANES = 128
ROWS = 128
NORM_EPS = 1e-6
TOPK_MAX = 256
WINDOW = 128
NEG_BIG = -1e30
LOG2E = 1.4426950408889634
KEY_NEG_INF = -2139095041
VMEM_LIMIT = 56 * 1024 * 1024

F32 = jnp.float32
BF16 = jnp.bfloat16
_NT = (((1,), (1,)), ((), ()))


def _alibi_slopes(n):
    return [2.0 ** (-8.0 * (i + 1) / n) for i in range(n)]


def _tile_lanes(x, n):
    return x if n == 1 else jnp.concatenate([x] * n, axis=1)


def _ada_kernel(c_ref, w_ref, b_ref, o_ref):
    c = c_ref[...]
    cond = c * jax.nn.sigmoid(c)
    o_ref[0] = jnp.dot(cond, w_ref[0], preferred_element_type=F32,
                       precision=lax.Precision.HIGHEST) + b_ref[0]


def _ada_mod(c, ada_w, ada_b):
    depth, d, n = ada_w.shape
    b = c.shape[0]
    rows = 8
    tn = 1536
    c_pad = jnp.zeros((rows, d), F32).at[:b].set(c)
    out = pl.pallas_call(
        _ada_kernel,
        grid=(depth, n // tn),
        in_specs=[pl.BlockSpec((rows, d), lambda l, j: (0, 0)),
                  pl.BlockSpec((1, d, tn), lambda l, j: (l, 0, j)),
                  pl.BlockSpec((1, 1, tn), lambda l, j: (l, 0, j))],
        out_specs=pl.BlockSpec((1, rows, tn), lambda l, j: (l, 0, j)),
        out_shape=jax.ShapeDtypeStruct((depth, rows, n), F32),
        compiler_params=pltpu.CompilerParams(
            dimension_semantics=("arbitrary", "arbitrary"), vmem_limit_bytes=VMEM_LIMIT),
        name="ada_mod",
    )(c_pad, ada_w, ada_b.reshape(depth, 1, n))
    return out[:, :b]


def _modulated_norm(x, g, sc, sh):
    ms = jnp.mean(x * x, axis=-1, keepdims=True)
    return (x * lax.rsqrt(ms + NORM_EPS) * g) * (1.0 + sc) + sh


def _proj_kernel(x_ref, g_ref, sc_ref, sh_ref, w_ref, bd_ref, gain_ref, *out_refs, groups):
    tm = x_ref.shape[1]
    half = tm // 2
    bd = bd_ref[...]
    nb = bd.shape[0]
    for rows in (slice(0, half), slice(half, tm)):
        h = _modulated_norm(x_ref[0, rows, :], g_ref[...], sc_ref[0], sh_ref[0]).astype(BF16)
        for (c0, width, parts) in groups:
            y = jnp.dot(h, w_ref[:, c0:c0 + width], preferred_element_type=F32)
            for (p0, pw, oi, o0, mode) in parts:
                step = min(nb, pw)
                for s in range(0, pw, step):
                    ys = y[:, p0 + s:p0 + s + step]
                    col = c0 + p0 + s
                    if mode == "norm":
                        ss = jnp.dot((ys * ys).astype(BF16), bd[:step, :step],
                                     preferred_element_type=F32)
                        ys = ys * lax.rsqrt(ss * (1.0 / HEAD_DIM) + NORM_EPS)
                    if mode != "plain":
                        ys = ys * gain_ref[:, col:col + step]
                    out_refs[oi][0, rows, o0 + s:o0 + s + step] = ys.astype(out_refs[oi].dtype)


def _proj(x, g, sc, sh, w, gain, groups, out_defs, tm):
    bsz, s, d = x.shape
    c = w.shape[1]
    nb = 2 * LANES
    r = lax.broadcasted_iota(jnp.int32, (nb, nb), 0) // HEAD_DIM
    cc = lax.broadcasted_iota(jnp.int32, (nb, nb), 1) // HEAD_DIM
    bd = (r == cc).astype(BF16)
    const = lambda b, i: (0, 0)
    return pl.pallas_call(
        functools.partial(_proj_kernel, groups=groups),
        grid=(bsz, s // tm),
        in_specs=[pl.BlockSpec((1, tm, d), lambda b, i: (b, i, 0)),
                  pl.BlockSpec((1, d), const),
                  pl.BlockSpec((1, 1, d), lambda b, i: (b, 0, 0)),
                  pl.BlockSpec((1, 1, d), lambda b, i: (b, 0, 0)),
                  pl.BlockSpec((d, c), const),
                  pl.BlockSpec((nb, nb), const),
                  pl.BlockSpec((1, c), const)],
        out_specs=[pl.BlockSpec((1, tm, wd), lambda b, i: (b, i, 0)) for wd, _ in out_defs],
        out_shape=[jax.ShapeDtypeStruct((bsz, s, wd), dt) for wd, dt in out_defs],
        compiler_params=pltpu.CompilerParams(
            dimension_semantics=("arbitrary", "arbitrary"), vmem_limit_bytes=VMEM_LIMIT),
        name="in_proj",
    )(x, g.reshape(1, d), sc, sh, w, bd, gain)


def _diff_attn_kernel(q_ref, k_ref, v_ref, lam_ref, subln_ref, o_ref,
                      qs_ref, m_ref, l_ref, acc_ref, *, t, heads, slopes, lam_init):
    i = pl.program_id(1)
    lane = lax.broadcasted_iota(jnp.int32, (t, LANES), 1)
    for h in range(heads):
        q = q_ref[0, :, h * LANES:(h + 1) * LANES]
        zero = jnp.zeros_like(q)
        qs_ref[h, :t] = jnp.where(lane < HEAD_DIM, q, zero)
        qs_ref[h, t:] = jnp.where(lane >= HEAD_DIM, q, zero)
    m_ref[...] = jnp.full(m_ref.shape, NEG_BIG, F32)
    l_ref[...] = jnp.zeros(l_ref.shape, F32)
    acc_ref[...] = jnp.zeros(acc_ref.shape, F32)

    def step(start, width, masked):
        col = lax.broadcasted_iota(jnp.int32, (1, width), 1)
        rel = (start - i * t + col).astype(F32)
        if masked:
            r = lax.broadcasted_iota(jnp.int32, (t, width), 0)
            c = lax.broadcasted_iota(jnp.int32, (t, width), 1)
            keep = r >= c
            keep = jnp.concatenate([keep, keep], axis=0)
        for h in range(heads):
            kc = k_ref[0, pl.ds(start, width), h * LANES:(h + 1) * LANES]
            vc = v_ref[0, pl.ds(start, width), h * LANES:(h + 1) * LANES]
            s = lax.dot_general(qs_ref[h], kc, _NT, preferred_element_type=F32)
            s = s + (slopes[h] * LOG2E) * rel
            if masked:
                s = jnp.where(keep, s, NEG_BIG)
            m_prev = m_ref[h]
            m_next = jnp.maximum(m_prev, jnp.max(s, axis=1, keepdims=True))
            alpha = jnp.exp2(m_prev - m_next)
            p = jnp.exp2(s - _tile_lanes(m_next, width // LANES))
            l_ref[h] = alpha * l_ref[h] + jnp.sum(p, axis=1, keepdims=True)
            acc_ref[h] = alpha * acc_ref[h] + jnp.dot(p.astype(BF16), vc,
                                                      preferred_element_type=F32)
            m_ref[h] = m_next

    def body(j, carry):
        step(pl.multiple_of(j * (2 * t), 2 * t), 2 * t, False)
        return carry

    lax.fori_loop(0, i // 2, body, 0)

    @pl.when(i % 2 == 1)
    def _():
        step(pl.multiple_of((i - 1) * t, t), t, False)

    step(pl.multiple_of(i * t, t), t, True)

    lam_v = lam_ref[...]
    s1 = jnp.sum(lam_v[0:1] * lam_v[1:2], axis=-1, keepdims=True)
    s2 = jnp.sum(lam_v[2:3] * lam_v[3:4], axis=-1, keepdims=True)
    lam = jnp.exp(s1) - jnp.exp(s2) + lam_init
    for h in range(heads):
        o = acc_ref[h] / l_ref[h]
        y = o[:t] - lam * o[t:]
        ms = jnp.mean(y * y, axis=-1, keepdims=True)
        y = (y * lax.rsqrt(ms + NORM_EPS) * subln_ref[...]) * (1.0 - lam_init)
        o_ref[0, :, h * LANES:(h + 1) * LANES] = y.astype(o_ref.dtype)


def _diff_attention(q, k, v, lam_vecs, subln, layer_idx, t):
    bsz, s, width = q.shape
    heads = width // LANES
    lam_init = 0.8 - 0.6 * math.exp(-0.3 * layer_idx)
    kern = functools.partial(_diff_attn_kernel, t=t, heads=heads,
                             slopes=_alibi_slopes(heads), lam_init=lam_init)
    return pl.pallas_call(
        kern,
        grid=(bsz, s // t),
        in_specs=[pl.BlockSpec((1, t, width), lambda b, i: (b, i, 0)),
                  pl.BlockSpec((1, s, width), lambda b, i: (b, 0, 0)),
                  pl.BlockSpec((1, s, width), lambda b, i: (b, 0, 0)),
                  pl.BlockSpec((4, HEAD_DIM), lambda b, i: (0, 0)),
                  pl.BlockSpec((1, LANES), lambda b, i: (0, 0))],
        out_specs=pl.BlockSpec((1, t, width), lambda b, i: (b, i, 0)),
        out_shape=jax.ShapeDtypeStruct((bsz, s, width), BF16),
        scratch_shapes=[pltpu.VMEM((heads, 2 * t, LANES), BF16),
                        pltpu.VMEM((heads, 2 * t, LANES), F32),
                        pltpu.VMEM((heads, 2 * t, LANES), F32),
                        pltpu.VMEM((heads, 2 * t, LANES), F32)],
        compiler_params=pltpu.CompilerParams(
            dimension_semantics=("arbitrary", "arbitrary"), vmem_limit_bytes=VMEM_LIMIT),
        name="diff_attn",
    )(q, k, v, lam_vecs, subln.reshape(1, LANES))


KCH = 256
ONES_ROWS = 16


def _diff_attn_t_kernel(q_ref, k_ref, vt_ref, lam_ref, subln_ref, o_ref,
                        qs_ref, m_ref, acc_ref, *, t, heads, slopes, lam_init):
    i = pl.program_id(1)
    q0 = i * t
    dv = vt_ref.shape[3] - ONES_ROWS
    lane = lax.broadcasted_iota(jnp.int32, (t, LANES), 1)
    for h in range(heads):
        q = q_ref[0, :, h * LANES:(h + 1) * LANES]
        zero = jnp.zeros_like(q)
        qs_ref[h, :t] = jnp.where(lane < HEAD_DIM, q, zero)
        qs_ref[h, t:] = jnp.where(lane >= HEAD_DIM, q, zero)
    m_ref[...] = jnp.full(m_ref.shape, NEG_BIG, F32)
    acc_ref[...] = jnp.zeros(acc_ref.shape, F32)

    def step(start, nck, masked):
        width = nck * KCH
        krow = lax.broadcasted_iota(jnp.int32, (width, LANES), 0)
        rel = (start - q0 + krow).astype(F32)
        ck = start // KCH
        logits = []
        for h in range(heads):
            kc = k_ref[0, pl.ds(start, width), h * LANES:(h + 1) * LANES]
            logits.append(lax.dot_general(kc, qs_ref[h], _NT, preferred_element_type=F32))
        for h in range(heads):
            alibi = _tile_lanes((slopes[h] * LOG2E) * rel, QBLK // LANES)
            for c0 in range(0, 2 * t, QBLK):
                s = logits[h][:, c0:c0 + QBLK] + alibi
                if masked:
                    kr = lax.broadcasted_iota(jnp.int32, (width, QBLK), 0) + (start - q0)
                    qc = lax.broadcasted_iota(jnp.int32, (width, QBLK), 1) + (c0 % t)
                    s = jnp.where(kr <= qc, s, NEG_BIG)
                m_prev = m_ref[h, :, c0:c0 + QBLK]
                m_next = jnp.maximum(m_prev, jnp.max(s, axis=0, keepdims=True))
                alpha = jnp.exp2(m_prev - m_next)
                p = jnp.exp2(s - m_next).astype(BF16)
                pv = None
                for c in range(nck):
                    part = jnp.dot(vt_ref[0, h, ck + c], p[c * KCH:(c + 1) * KCH],
                                   preferred_element_type=F32)
                    pv = part if pv is None else pv + part
                acc_ref[h, :, c0:c0 + QBLK] = alpha * acc_ref[h, :, c0:c0 + QBLK] + pv
                m_ref[h, :, c0:c0 + QBLK] = m_next

    def body(j, carry):
        step(pl.multiple_of(j * KCH, KCH), 1, False)
        return carry

    lax.fori_loop(0, q0 // KCH, body, 0)
    for c in range(t // KCH):
        step(pl.multiple_of(q0 + c * KCH, KCH), 1, True)

    lam_v = lam_ref[...]
    s1 = jnp.sum(lam_v[0:1] * lam_v[1:2], axis=-1, keepdims=True)
    s2 = jnp.sum(lam_v[2:3] * lam_v[3:4], axis=-1, keepdims=True)
    lam = jnp.exp(s1) - jnp.exp(s2) + lam_init
    for h in range(heads):
        acc = acc_ref[h]
        o = acc[:dv] / acc[dv:dv + 1]
        y = (o[:, :t] - lam * o[:, t:]).T
        ms = jnp.mean(y * y, axis=-1, keepdims=True)
        y = (y * lax.rsqrt(ms + NORM_EPS) * subln_ref[...]) * (1.0 - lam_init)
        o_ref[0, :, h * LANES:(h + 1) * LANES] = y.astype(o_ref.dtype)


def _diff_attention_t(q, k, v, lam_vecs, subln, layer_idx, t):
    bsz, s, width = q.shape
    heads = width // LANES
    lam_init = 0.8 - 0.6 * math.exp(-0.3 * layer_idx)
    vt = v.reshape(bsz, s // KCH, KCH, heads, LANES).transpose(0, 3, 1, 4, 2)
    vt = jnp.concatenate([vt, jnp.ones(vt.shape[:3] + (ONES_ROWS, KCH), vt.dtype)], axis=3)
    kern = functools.partial(_diff_attn_t_kernel, t=t, heads=heads,
                             slopes=_alibi_slopes(heads), lam_init=lam_init)
    return pl.pallas_call(
        kern,
        grid=(bsz, s // t),
        in_specs=[pl.BlockSpec((1, t, width), lambda b, i: (b, i, 0)),
                  pl.BlockSpec((1, s, width), lambda b, i: (b, 0, 0)),
                  pl.BlockSpec((1,) + vt.shape[1:], lambda b, i: (b, 0, 0, 0, 0)),
                  pl.BlockSpec((4, HEAD_DIM), lambda b, i: (0, 0)),
                  pl.BlockSpec((1, LANES), lambda b, i: (0, 0))],
        out_specs=pl.BlockSpec((1, t, width), lambda b, i: (b, i, 0)),
        out_shape=jax.ShapeDtypeStruct((bsz, s, width), BF16),
        scratch_shapes=[pltpu.VMEM((heads, 2 * t, LANES), BF16),
                        pltpu.VMEM((heads, 1, 2 * t), F32),
                        pltpu.VMEM((heads, LANES + ONES_ROWS, 2 * t), F32)],
        compiler_params=pltpu.CompilerParams(
            dimension_semantics=("arbitrary", "arbitrary"), vmem_limit_bytes=VMEM_LIMIT),
        name="diff_attn",
    )(q, k, vt, lam_vecs, subln.reshape(1, LANES))


def _diff_attn_b_kernel(q_ref, k_ref, v_ref, lam_ref, subln_ref, o_ref,
                        qs_ref, m_ref, acc_ref, *, t, kw, heads, slopes, lam_init):
    i = pl.program_id(1)
    q0 = i * t
    lane = lax.broadcasted_iota(jnp.int32, (t, LANES), 1)
    for h in range(heads):
        q = q_ref[0, :, h * LANES:(h + 1) * LANES]
        zero = jnp.zeros_like(q)
        qs_ref[h, :t] = jnp.where(lane < HEAD_DIM, q, zero)
        qs_ref[h, t:] = jnp.where(lane >= HEAD_DIM, q, zero)
    m_ref[...] = jnp.full(m_ref.shape, NEG_BIG, F32)
    acc_ref[...] = jnp.zeros(acc_ref.shape, F32)

    def step(start, width, masked):
        col = lax.broadcasted_iota(jnp.int32, (1, width), 1)
        rel = (start - q0 + col).astype(F32)
        nsl = width // LANES
        logits, values = [], []
        for h in range(heads):
            kc = k_ref[0, pl.ds(start, width), h * LANES:(h + 1) * LANES]
            vc = v_ref[0, pl.ds(start, width), h * LANES:(h + 1) * LANES]
            logits.append(lax.dot_general(qs_ref[h], kc, _NT, preferred_element_type=F32))
            values.append(jnp.concatenate([vc, jnp.ones_like(vc)], axis=1))
        for h in range(heads):
            alibi = (slopes[h] * LOG2E) * rel
            for r0 in range(0, 2 * t, ROWS):
                s = logits[h][r0:r0 + ROWS] + alibi
                if masked:
                    r = lax.broadcasted_iota(jnp.int32, (ROWS, width), 0) + (r0 % t)
                    c = lax.broadcasted_iota(jnp.int32, (ROWS, width), 1)
                    s = jnp.where(r >= c, s, NEG_BIG)
                m_prev = m_ref[h, r0:r0 + ROWS]
                m_next = jnp.maximum(m_prev, jnp.max(s, axis=1, keepdims=True))
                alpha = jnp.exp2(m_prev - m_next)
                p = jnp.exp2(s - _tile_lanes(m_next, nsl))
                acc_ref[h, r0:r0 + ROWS] = (
                    _tile_lanes(alpha, 2) * acc_ref[h, r0:r0 + ROWS]
                    + jnp.dot(p.astype(BF16), values[h], preferred_element_type=F32))
                m_ref[h, r0:r0 + ROWS] = m_next

    def body(j, carry):
        step(pl.multiple_of(j * kw, kw), kw, False)
        return carry

    nfull = q0 // kw
    lax.fori_loop(0, nfull, body, 0)

    if t < kw:
        @pl.when(q0 - nfull * kw > 0)
        def _():
            step(pl.multiple_of(q0 - t, t), t, False)

    step(pl.multiple_of(q0, t), t, True)

    lam_v = lam_ref[...]
    s1 = jnp.sum(lam_v[0:1] * lam_v[1:2], axis=-1, keepdims=True)
    s2 = jnp.sum(lam_v[2:3] * lam_v[3:4], axis=-1, keepdims=True)
    lam = jnp.exp(s1) - jnp.exp(s2) + lam_init
    for h in range(heads):
        acc = acc_ref[h]
        o = acc[:, :LANES] / acc[:, LANES:]
        y = o[:t] - lam * o[t:]
        ms = jnp.mean(y * y, axis=-1, keepdims=True)
        y = (y * lax.rsqrt(ms + NORM_EPS) * subln_ref[...]) * (1.0 - lam_init)
        o_ref[0, :, h * LANES:(h + 1) * LANES] = y.astype(o_ref.dtype)


def _diff_attention_b(q, k, v, lam_vecs, subln, layer_idx, t, kw):
    bsz, s, width = q.shape
    heads = width // LANES
    assert kw in (t, 2 * t)
    lam_init = 0.8 - 0.6 * math.exp(-0.3 * layer_idx)
    kern = functools.partial(_diff_attn_b_kernel, t=t, kw=kw, heads=heads,
                             slopes=_alibi_slopes(heads), lam_init=lam_init)
    return pl.pallas_call(
        kern,
        grid=(bsz, s // t),
        in_specs=[pl.BlockSpec((1, t, width), lambda b, i: (b, i, 0)),
                  pl.BlockSpec((1, s, width), lambda b, i: (b, 0, 0)),
                  pl.BlockSpec((1, s, width), lambda b, i: (b, 0, 0)),
                  pl.BlockSpec((4, HEAD_DIM), lambda b, i: (0, 0)),
                  pl.BlockSpec((1, LANES), lambda b, i: (0, 0))],
        out_specs=pl.BlockSpec((1, t, width), lambda b, i: (b, i, 0)),
        out_shape=jax.ShapeDtypeStruct((bsz, s, width), BF16),
        scratch_shapes=[pltpu.VMEM((heads, 2 * t, LANES), BF16),
                        pltpu.VMEM((heads, 2 * t, LANES), F32),
                        pltpu.VMEM((heads, 2 * t, 2 * LANES), F32)],
        compiler_params=pltpu.CompilerParams(
            dimension_semantics=("arbitrary", "arbitrary"), vmem_limit_bytes=VMEM_LIMIT),
        name="diff_attn",
    )(q, k, v, lam_vecs, subln.reshape(1, LANES))


def _stack_heads(x, heads):
    lane = lax.broadcasted_iota(jnp.int32, (x.shape[0], LANES), 1)
    lo = lane < HEAD_DIM
    parts = []
    for h in range(heads):
        blk = x[:, (h // 2) * LANES:(h // 2 + 1) * LANES]
        keep = lo if h % 2 == 0 else jnp.logical_not(lo)
        parts.append(jnp.where(keep, blk, jnp.zeros_like(blk)))
    return jnp.concatenate(parts, axis=0)


def _key_to_float(k):
    return lax.bitcast_convert_type(k ^ ((k >> 31) & jnp.int32(0x7FFFFFFF)), F32)


def _dsa_kernel(bq_ref, iq_ref, iw_ref, ik_ref, bk_ref, bv_ref, o_ref,
                sc_ref, iqs_ref, bqs_ref, wb_ref, m_ref, acc_ref,
                *, tq, tk, heads, topk, slopes):
    i = pl.program_id(1)
    q0 = i * tq
    nch = q0 // tk + 1
    nsl = tk // LANES
    row = lax.broadcasted_iota(jnp.int32, (tq, tk), 0)
    col = lax.broadcasted_iota(jnp.int32, (tq, tk), 1)

    lane = lax.broadcasted_iota(jnp.int32, (tq, LANES), 1)
    iw = iw_ref[0]
    for h in range(heads):
        keep = (lane < HEAD_DIM) if h % 2 == 0 else (lane >= HEAD_DIM)
        tile = slice((h // 2) * LANES, (h // 2 + 1) * LANES)
        iqb = iq_ref[0, :, tile]
        bqb = bq_ref[0, :, tile]
        iqs_ref[h] = jnp.where(keep, iqb, jnp.zeros_like(iqb))
        bqs_ref[h * tq:(h + 1) * tq] = jnp.where(keep, bqb, jnp.zeros_like(bqb))
        wb_ref[h] = jnp.broadcast_to(iw[:, h:h + 1], (tq, LANES))

    def score_body(j, carry):
        start = pl.multiple_of(j * tk, tk)
        ikc = ik_ref[0, pl.ds(start, tk), :]
        sc = None
        for h in range(heads):
            logit = lax.dot_general(iqs_ref[h], ikc, _NT, preferred_element_type=F32)
            term = jnp.maximum(logit, 0.0) * _tile_lanes(wb_ref[h], nsl)
            sc = term if sc is None else sc + term
        sc = sc + 0.0
        causal = (start + col) <= (q0 + row)
        sc_ref[j] = jnp.where(causal, sc, -jnp.inf)
        return carry

    lax.fori_loop(0, nch, score_body, 0)

    def count(pred_fn, thr):
        parts = []
        for r0 in range(0, tq, ROWS):
            thr_b = thr[r0:r0 + ROWS]

            def body(j, cnt):
                sc = sc_ref[j, r0:r0 + ROWS, :]
                for s in range(nsl):
                    cnt = cnt + jnp.where(pred_fn(sc[:, s * LANES:(s + 1) * LANES], thr_b), 1.0, 0.0)
                return cnt

            parts.append(lax.fori_loop(0, nch, body, jnp.zeros((ROWS, LANES), F32)))
        cnt = jnp.concatenate(parts, axis=0)
        return jnp.broadcast_to(jnp.sum(cnt, axis=1, keepdims=True), (tq, LANES))

    ge = lambda a, b: a >= b
    gt = lambda a, b: a > b

    def bit_body(b, key):
        cand = key + lax.shift_left(jnp.int32(1), 31 - b)
        return jnp.where(count(ge, _key_to_float(cand)) >= float(topk), cand, key)

    int_min = jnp.int32(-2 ** 31)
    key = lax.fori_loop(0, 32, bit_body, jnp.full((tq, LANES), int_min, jnp.int32))
    thr = jnp.where(key <= KEY_NEG_INF, -jnp.inf, _key_to_float(key))
    n_ge = count(ge, thr)
    n_gt = count(gt, thr)
    need = float(topk) - n_gt
    has_excess_ties = jnp.max(jnp.where(n_ge > float(topk), 1.0, 0.0)) > 0.0
    thr_t = _tile_lanes(thr, nsl)

    @pl.when(jnp.logical_not(has_excess_ties))
    def _():
        def body(j, carry):
            start = pl.multiple_of(j * tk, tk)
            keep = (sc_ref[j] >= thr_t) & ((start + col) <= (q0 + row))
            sc_ref[j] = jnp.where(keep, 0.0, NEG_BIG)
            return carry
        lax.fori_loop(0, nch, body, 0)

    @pl.when(has_excess_ties)
    def _():
        r2 = lax.broadcasted_iota(jnp.int32, (tk, tk), 0)
        c2 = lax.broadcasted_iota(jnp.int32, (tk, tk), 1)
        upper = jnp.where(r2 <= c2, 1.0, 0.0).astype(BF16)
        need_t = _tile_lanes(need, nsl)

        def body(j, seen):
            start = pl.multiple_of(j * tk, tk)
            sc = sc_ref[j]
            eq = sc == thr_t
            eqf = jnp.where(eq, 1.0, 0.0)
            rank = _tile_lanes(seen, nsl) + jnp.dot(eqf.astype(BF16), upper,
                                                    preferred_element_type=F32)
            keep = ((sc > thr_t) | (eq & (rank <= need_t))) & ((start + col) <= (q0 + row))
            sc_ref[j] = jnp.where(keep, 0.0, NEG_BIG)
            return seen + jnp.broadcast_to(jnp.sum(eqf, axis=1, keepdims=True), (tq, LANES))

        lax.fori_loop(0, nch, body, jnp.zeros((tq, LANES), F32))

    m_ref[...] = jnp.full(m_ref.shape, NEG_BIG, F32)
    acc_ref[...] = jnp.zeros(acc_ref.shape, F32)
    kcol = lax.broadcasted_iota(jnp.int32, (1, tk), 1)
    vlane = lax.broadcasted_iota(jnp.int32, (tk, LANES), 1)

    def attn_body(j, carry):
        start = pl.multiple_of(j * tk, tk)
        kc = bk_ref[0, pl.ds(start, tk), :]
        vc = bv_ref[0, pl.ds(start, tk), :]
        one = jnp.ones_like(vc)
        v_even = jnp.where(vlane < HEAD_DIM, vc, one)
        v_odd = jnp.where(vlane < HEAD_DIM, one, vc)
        logit = lax.dot_general(bqs_ref[...], kc, _NT, preferred_element_type=F32)
        rel = (start - q0 + kcol).astype(F32)
        for h in range(heads):
            alibi = (slopes[h] * LOG2E) * rel
            for r0 in range(0, tq, ROWS):
                rs = slice(h * tq + r0, h * tq + r0 + ROWS)
                s = logit[rs] + sc_ref[j, r0:r0 + ROWS, :] + alibi
                m_prev = m_ref[rs]
                m_next = jnp.maximum(m_prev, jnp.max(s, axis=1, keepdims=True))
                alpha = jnp.exp2(m_prev - m_next)
                p = jnp.exp2(s - _tile_lanes(m_next, nsl))
                acc_ref[rs] = alpha * acc_ref[rs] + jnp.dot(
                    p.astype(BF16), v_even if h % 2 == 0 else v_odd, preferred_element_type=F32)
                m_ref[rs] = m_next
        return carry

    lax.fori_loop(0, nch, attn_body, 0)

    for g in range(heads // 2):
        a = acc_ref[(2 * g) * tq:(2 * g + 1) * tq]
        b = acc_ref[(2 * g + 1) * tq:(2 * g + 2) * tq]
        oa = a / pltpu.roll(a, HEAD_DIM, axis=1)
        ob = b / pltpu.roll(b, HEAD_DIM, axis=1)
        o_ref[0, :, g * LANES:(g + 1) * LANES] = jnp.where(lane < HEAD_DIM, oa, ob).astype(o_ref.dtype)


def _dsa_attention(bq, iq, iwp, ik2, bk2, bv2, tq, tk):
    bsz, s, width = bq.shape
    heads = width // HEAD_DIM
    topk = min(TOPK_MAX, s // 4)
    nck = s // tk
    kern = functools.partial(_dsa_kernel, tq=tq, tk=tk, heads=heads, topk=topk,
                             slopes=_alibi_slopes(heads))
    qspec = lambda w: pl.BlockSpec((1, tq, w), lambda b, i: (b, i, 0))
    kspec = pl.BlockSpec((1, s, LANES), lambda b, i: (b, 0, 0))
    return pl.pallas_call(
        kern,
        grid=(bsz, s // tq),
        in_specs=[qspec(width), qspec(width), qspec(LANES), kspec, kspec, kspec],
        out_specs=pl.BlockSpec((1, tq, width), lambda b, i: (b, i, 0)),
        out_shape=jax.ShapeDtypeStruct((bsz, s, width), BF16),
        scratch_shapes=[pltpu.VMEM((nck, tq, tk), F32),
                        pltpu.VMEM((heads, tq, LANES), BF16),
                        pltpu.VMEM((heads * tq, LANES), BF16),
                        pltpu.VMEM((heads, tq, LANES), F32),
                        pltpu.VMEM((heads * tq, LANES), F32),
                        pltpu.VMEM((heads * tq, LANES), F32)],
        compiler_params=pltpu.CompilerParams(
            dimension_semantics=("arbitrary", "arbitrary"), vmem_limit_bytes=VMEM_LIMIT),
        name="dsa_attn",
    )(bq, iq, iwp, ik2, bk2, bv2)


SUB = 8
QBLK = 256
NACC = 4


def _dsa_t_kernel(bq_ref, iq_ref, iw_ref, ik_ref, bk_ref, vt_ref, o_ref,
                  sc_ref, iqs_ref, bqs_ref, wt_ref, m_ref, acc_ref,
                  *, tq, tk, heads, topk, slopes):
    i = pl.program_id(1)
    q0 = i * tq
    nch = q0 // tk + 1
    nsl = tk // LANES
    krow = lax.broadcasted_iota(jnp.int32, (tk, tq), 0)
    qcol = lax.broadcasted_iota(jnp.int32, (tk, tq), 1)

    lane = lax.broadcasted_iota(jnp.int32, (tq, LANES), 1)
    for h in range(heads):
        keep = (lane < HEAD_DIM) if h % 2 == 0 else (lane >= HEAD_DIM)
        tile = slice((h // 2) * LANES, (h // 2 + 1) * LANES)
        iqb = iq_ref[0, :, tile]
        bqb = bq_ref[0, :, tile]
        iqs_ref[h] = jnp.where(keep, iqb, jnp.zeros_like(iqb))
        bqs_ref[h * tq:(h + 1) * tq] = jnp.where(keep, bqb, jnp.zeros_like(bqb))
    wt_ref[...] = iw_ref[0].T

    def score_body(j, carry):
        start = pl.multiple_of(j * tk, tk)
        ikc = ik_ref[0, pl.ds(start, tk), :]
        sc = None
        for h in range(heads):
            logit = lax.dot_general(ikc, iqs_ref[h], _NT, preferred_element_type=F32)
            term = jnp.maximum(logit, 0.0) * wt_ref[h:h + 1, :]
            sc = term if sc is None else sc + term
        sc = sc + 0.0
        causal = (start + krow) <= (q0 + qcol)
        sc_ref[j] = jnp.where(causal, sc, -jnp.inf)
        return carry

    lax.fori_loop(0, nch, score_body, 0)

    def count(pred_fn, thr):
        def body(j, accs):
            accs = list(accs)
            for r in range(tk // SUB):
                blk = sc_ref[j, r * SUB:(r + 1) * SUB, :]
                accs[r % NACC] = accs[r % NACC] + jnp.where(pred_fn(blk, thr), 1.0, 0.0)
            return tuple(accs)

        accs = lax.fori_loop(0, nch, body, tuple(jnp.zeros((SUB, tq), F32) for _ in range(NACC)))
        cnt = accs[0]
        for a in accs[1:]:
            cnt = cnt + a
        return jnp.broadcast_to(jnp.sum(cnt, axis=0, keepdims=True), (SUB, tq))

    ge = lambda a, b: a >= b
    gt = lambda a, b: a > b

    def bit_body(b, key):
        cand = key + lax.shift_left(jnp.int32(1), 31 - b)
        return jnp.where(count(ge, _key_to_float(cand)) >= float(topk), cand, key)

    int_min = jnp.int32(-2 ** 31)
    key = lax.fori_loop(0, 32, bit_body, jnp.full((SUB, tq), int_min, jnp.int32))
    thr = jnp.where(key <= KEY_NEG_INF, -jnp.inf, _key_to_float(key))
    n_ge = count(ge, thr)
    n_gt = count(gt, thr)
    need = (float(topk) - n_gt)[0:1]
    has_excess_ties = jnp.max(jnp.where(n_ge > float(topk), 1.0, 0.0)) > 0.0
    thr_row = thr[0:1]

    @pl.when(jnp.logical_not(has_excess_ties))
    def _():
        def body(j, carry):
            start = pl.multiple_of(j * tk, tk)
            keep = (sc_ref[j] >= thr_row) & ((start + krow) <= (q0 + qcol))
            sc_ref[j] = jnp.where(keep, 0.0, NEG_BIG)
            return carry
        lax.fori_loop(0, nch, body, 0)

    @pl.when(has_excess_ties)
    def _():
        r2 = lax.broadcasted_iota(jnp.int32, (tk, tk), 0)
        c2 = lax.broadcasted_iota(jnp.int32, (tk, tk), 1)
        lower = jnp.where(c2 <= r2, 1.0, 0.0).astype(BF16)

        def body(j, seen):
            start = pl.multiple_of(j * tk, tk)
            sc = sc_ref[j]
            eq = sc == thr_row
            eqf = jnp.where(eq, 1.0, 0.0)
            rank = seen + jnp.dot(lower, eqf.astype(BF16), preferred_element_type=F32)
            keep = ((sc > thr_row) | (eq & (rank <= need))) & ((start + krow) <= (q0 + qcol))
            sc_ref[j] = jnp.where(keep, 0.0, NEG_BIG)
            return seen + jnp.sum(eqf, axis=0, keepdims=True)

        lax.fori_loop(0, nch, body, jnp.zeros((1, tq), F32))

    m_ref[...] = jnp.full(m_ref.shape, NEG_BIG, F32)
    acc_ref[...] = jnp.zeros(acc_ref.shape, F32)
    krow_t = lax.broadcasted_iota(jnp.int32, (tk, LANES), 0)

    def attn_body(j, carry):
        start = pl.multiple_of(j * tk, tk)
        kc = bk_ref[0, pl.ds(start, tk), :]
        vt = vt_ref[0, j]
        logit = lax.dot_general(kc, bqs_ref[...], _NT, preferred_element_type=F32)
        rel = (start - q0 + krow_t).astype(F32)
        for h in range(heads):
            alibi = _tile_lanes((slopes[h] * LOG2E) * rel, QBLK // LANES)
            for c0 in range(0, tq, QBLK):
                s = logit[:, h * tq + c0:h * tq + c0 + QBLK] + sc_ref[j, :, c0:c0 + QBLK] + alibi
                m_prev = m_ref[h, :, c0:c0 + QBLK]
                m_next = jnp.maximum(m_prev, jnp.max(s, axis=0, keepdims=True))
                alpha = jnp.exp2(m_prev - m_next)
                p = jnp.exp2(s - m_next).astype(BF16)
                acc_ref[h, :, c0:c0 + QBLK] = (alpha * acc_ref[h, :, c0:c0 + QBLK]
                                               + jnp.dot(vt, p, preferred_element_type=F32))
                m_ref[h, :, c0:c0 + QBLK] = m_next
        return carry

    lax.fori_loop(0, nch, attn_body, 0)

    for g in range(heads // 2):
        a = acc_ref[2 * g]
        b = acc_ref[2 * g + 1]
        pair = jnp.concatenate([a[:HEAD_DIM] / a[HEAD_DIM:HEAD_DIM + 1],
                                b[:HEAD_DIM] / b[HEAD_DIM:HEAD_DIM + 1]], axis=0)
        o_ref[0, :, g * LANES:(g + 1) * LANES] = pair.T.astype(o_ref.dtype)


def _dsa_attention_t(bq, iq, iwp, ik2, bk2, bv2, tq, tk):
    bsz, s, width = bq.shape
    heads = width // HEAD_DIM
    topk = min(TOPK_MAX, s // 4)
    nck = s // tk
    kern = functools.partial(_dsa_t_kernel, tq=tq, tk=tk, heads=heads, topk=topk,
                             slopes=_alibi_slopes(heads))
    vt = bv2[:, :, :HEAD_DIM].reshape(bsz, nck, tk, HEAD_DIM).transpose(0, 1, 3, 2)
    vt = jnp.concatenate([vt, jnp.ones_like(vt)], axis=2)
    qspec = lambda w: pl.BlockSpec((1, tq, w), lambda b, i: (b, i, 0))
    kspec = pl.BlockSpec((1, s, LANES), lambda b, i: (b, 0, 0))
    return pl.pallas_call(
        kern,
        grid=(bsz, s // tq),
        in_specs=[qspec(width), qspec(width), qspec(LANES), kspec, kspec,
                  pl.BlockSpec((1, nck, LANES, tk), lambda b, i: (b, 0, 0, 0))],
        out_specs=pl.BlockSpec((1, tq, width), lambda b, i: (b, i, 0)),
        out_shape=jax.ShapeDtypeStruct((bsz, s, width), BF16),
        scratch_shapes=[pltpu.VMEM((nck, tk, tq), F32),
                        pltpu.VMEM((heads, tq, LANES), BF16),
                        pltpu.VMEM((heads * tq, LANES), BF16),
                        pltpu.VMEM((LANES, tq), F32),
                        pltpu.VMEM((heads, 1, tq), F32),
                        pltpu.VMEM((heads, LANES, tq), F32)],
        compiler_params=pltpu.CompilerParams(
            dimension_semantics=("arbitrary", "arbitrary"), vmem_limit_bytes=VMEM_LIMIT),
        name="dsa_attn",
    )(bq, iq, iwp, ik2, bk2, vt)


def _swa_kernel(slope_ref, sink_ref, q_ref, kp_ref, kc_ref, vp_ref, vc_ref, o_ref, *, tq, group):
    kv = pl.program_id(1)
    i = pl.program_id(2)
    nk = 2 * WINDOW
    kk = jnp.concatenate([kp_ref[0], kc_ref[0]], axis=0)
    vv = jnp.concatenate([vp_ref[0], vc_ref[0]], axis=0)
    r = lax.broadcasted_iota(jnp.int32, (WINDOW, nk), 0)
    c = lax.broadcasted_iota(jnp.int32, (WINDOW, nk), 1)
    dist = WINDOW + r - c
    band = (dist >= 0) & (dist < WINDOW)
    distf = dist.astype(F32)
    lane = lax.broadcasted_iota(jnp.int32, (WINDOW, LANES), 1)
    biases = [jnp.where(band, -(slope_ref[kv * group + g] * LOG2E) * distf, NEG_BIG)
              for g in range(group)]
    for blk in range(tq // WINDOW):
        rows = slice(blk * WINDOW, (blk + 1) * WINDOW)
        qs = _stack_heads(q_ref[0, rows, :], group)
        kb = kk[blk * WINDOW:blk * WINDOW + nk]
        vb = vv[blk * WINDOW:blk * WINDOW + nk]
        logit = lax.dot_general(qs, kb, _NT, preferred_element_type=F32)
        outs = []
        for g in range(group):
            hq = kv * group + g
            s = logit[g * WINDOW:(g + 1) * WINDOW] + biases[g]
            if blk == 0:
                s = jnp.where(c >= jnp.where(i == 0, WINDOW, 0), s, NEG_BIG)
            sink = sink_ref[hq] * LOG2E
            m = jnp.maximum(jnp.max(s, axis=1, keepdims=True), sink)
            p = jnp.exp2(s - m)
            denom = jnp.sum(p, axis=1, keepdims=True) + jnp.exp2(sink - m)
            outs.append(jnp.dot(p.astype(BF16), vb, preferred_element_type=F32) / denom)
        for g in range(group // 2):
            o_ref[0, rows, g * LANES:(g + 1) * LANES] = jnp.where(
                lane < HEAD_DIM, outs[2 * g], outs[2 * g + 1]).astype(o_ref.dtype)


def _swa_attention(q, k2, v2, sinks, tq):
    bsz, s, width = q.shape
    heads = width // HEAD_DIM
    kvh = k2.shape[2] // LANES
    group = heads // kvh
    slopes = jnp.asarray(_alibi_slopes(heads), F32)
    r = tq // WINDOW
    prev = lambda b, kv, i: (b, jnp.maximum(i * r - 1, 0), kv)
    cur = lambda b, kv, i: (b, i, kv)
    smem = pl.BlockSpec(memory_space=pltpu.SMEM)
    return pl.pallas_call(
        functools.partial(_swa_kernel, tq=tq, group=group),
        grid=(bsz, kvh, s // tq),
        in_specs=[smem, smem,
                  pl.BlockSpec((1, tq, group * HEAD_DIM), cur),
                  pl.BlockSpec((1, WINDOW, LANES), prev),
                  pl.BlockSpec((1, tq, LANES), cur),
                  pl.BlockSpec((1, WINDOW, LANES), prev),
                  pl.BlockSpec((1, tq, LANES), cur)],
        out_specs=pl.BlockSpec((1, tq, group * HEAD_DIM), cur),
        out_shape=jax.ShapeDtypeStruct((bsz, s, width), BF16),
        compiler_params=pltpu.CompilerParams(
            dimension_semantics=("arbitrary", "arbitrary", "arbitrary"),
            vmem_limit_bytes=VMEM_LIMIT),
        name="swa_attn",
    )(slopes, sinks.astype(F32), q, k2, k2, v2, v2)


def _out_ffn_kernel(*refs, ny):
    x_ref = refs[0]
    y_refs = refs[1:1 + ny]
    wo_ref, g1_ref, n_ref, sc_ref, sh_ref, g2_ref, wg_ref, wu_ref, wd_ref, o_ref = refs[1 + ny:]
    tm = x_ref.shape[1]
    half = tm // 2
    for rows in (slice(0, half), slice(half, tm)):
        mix = None
        r0 = 0
        for y_ref in y_refs:
            w = y_ref.shape[2]
            part = jnp.dot(y_ref[0, rows, :], wo_ref[r0:r0 + w, :], preferred_element_type=F32)
            mix = part if mix is None else mix + part
            r0 += w
        x = x_ref[0, rows, :] + g1_ref[0] * mix
        h = _modulated_norm(x, n_ref[...], sc_ref[0], sh_ref[0]).astype(BF16)
        gate = jnp.dot(h, wg_ref[...], preferred_element_type=F32)
        up = jnp.dot(h, wu_ref[...], preferred_element_type=F32)
        act = (gate * jax.nn.sigmoid(gate) * up).astype(BF16)
        o_ref[0, rows, :] = x + g2_ref[0] * jnp.dot(act, wd_ref[...], preferred_element_type=F32)


def _out_ffn(x, ys, wo, g1, n, sc, sh, g2, wg, wu, wd, tm):
    bsz, s, d = x.shape
    f = wg.shape[1]
    const = lambda b, i: (0, 0)
    tok = lambda b, i: (b, i, 0)
    mod = lambda b, i: (b, 0, 0)
    once = pl.Buffered(1)
    return pl.pallas_call(
        functools.partial(_out_ffn_kernel, ny=len(ys)),
        grid=(bsz, s // tm),
        in_specs=[pl.BlockSpec((1, tm, d), tok)]
                 + [pl.BlockSpec((1, tm, y.shape[2]), tok) for y in ys]
                 + [pl.BlockSpec((d, d), const, pipeline_mode=once),
                  pl.BlockSpec((1, 1, d), mod),
                  pl.BlockSpec((1, d), const),
                  pl.BlockSpec((1, 1, d), mod),
                  pl.BlockSpec((1, 1, d), mod),
                  pl.BlockSpec((1, 1, d), mod),
                  pl.BlockSpec((d, f), const, pipeline_mode=once),
                  pl.BlockSpec((d, f), const, pipeline_mode=once),
                  pl.BlockSpec((f, d), const, pipeline_mode=once)],
        out_specs=pl.BlockSpec((1, tm, d), tok),
        out_shape=jax.ShapeDtypeStruct((bsz, s, d), F32),
        compiler_params=pltpu.CompilerParams(
            dimension_semantics=("arbitrary", "arbitrary"), vmem_limit_bytes=VMEM_LIMIT),
        name="out_ffn",
    )(x, *ys, wo, g1, n.reshape(1, d), sc, sh, g2, wg, wu, wd)


def _dup(w):
    d, c = w.shape
    w = w.reshape(d, c // HEAD_DIM, 1, HEAD_DIM)
    return jnp.broadcast_to(w, (d, c // HEAD_DIM, 2, HEAD_DIM)).reshape(d, 2 * c)


def _even_layout(w_in, qn_a, kn_a, qn_b, kn_b, d):
    a = d // 2
    sizes = (a, a, a, a, HEAD_DIM, HEAD_DIM, a, HEAD_DIM, a // HEAD_DIM)
    offs = [0]
    for sz in sizes:
        offs.append(offs[-1] + sz)
    aq, ak, av, bq, bk, bv, iq, ik, iw = [w_in[:, offs[n]:offs[n + 1]] for n in range(9)]
    nh = a // HEAD_DIM
    iw_pad = jnp.zeros((d, LANES), w_in.dtype).at[:, :nh].set(iw)
    w = jnp.concatenate([aq, ak, av, bq, iq, _dup(bk), _dup(bv), _dup(ik), iw_pad], axis=1)
    qscale = HEAD_DIM ** -0.5 * LOG2E
    ones = lambda n: jnp.ones((n,), F32)
    gain = jnp.concatenate([
        jnp.tile(qn_a, nh) * qscale, jnp.tile(kn_a, nh), ones(a),
        jnp.tile(qn_b, nh) * qscale, ones(a) * HEAD_DIM ** -0.5,
        jnp.tile(kn_b, 2), ones(LANES), ones(LANES), ones(LANES) * nh ** -0.5])
    groups = (
        (0, a, ((0, a, 0, 0, "norm"),)),
        (a, a, ((0, a, 1, 0, "norm"),)),
        (2 * a, a, ((0, a, 2, 0, "plain"),)),
        (3 * a, a, ((0, a, 3, 0, "norm"),)),
        (4 * a, a, ((0, a, 4, 0, "scale"),)),
        (5 * a, 4 * LANES, ((0, LANES, 5, 0, "norm"), (LANES, LANES, 6, 0, "plain"),
                            (2 * LANES, LANES, 7, 0, "plain"), (3 * LANES, LANES, 8, 0, "scale"))),
    )
    out_defs = ((a, BF16), (a, BF16), (a, BF16), (a, BF16), (a, BF16),
                (LANES, BF16), (LANES, BF16), (LANES, BF16), (LANES, F32))
    return w.astype(BF16), gain.reshape(1, -1), groups, out_defs


def _odd_layout(w_in, qn_c, kn_c, d):
    kvw = d // 4
    q, k, v = w_in[:, :d], w_in[:, d:d + kvw], w_in[:, d + kvw:]
    w = jnp.concatenate([q, _dup(k), _dup(v)], axis=1)
    nh = d // HEAD_DIM
    gain = jnp.concatenate([jnp.tile(qn_c, nh) * (HEAD_DIM ** -0.5 * LOG2E),
                            jnp.tile(kn_c, 2 * kvw // HEAD_DIM), jnp.ones((2 * kvw,), F32)])
    half = d // 2
    groups = (
        (0, half, ((0, half, 0, 0, "norm"),)),
        (half, half, ((0, half, 0, half, "norm"),)),
        (d, 2 * kvw, ((0, 2 * kvw, 1, 0, "norm"),)),
        (d + 2 * kvw, 2 * kvw, ((0, 2 * kvw, 2, 0, "plain"),)),
    )
    out_defs = ((d, BF16), (2 * kvw, BF16), (2 * kvw, BF16))
    return w.astype(BF16), gain.reshape(1, -1), groups, out_defs


def kernel(x, c, ada_w, ada_b, norm_mix, norm_ffn, w_out, ffn_gate, ffn_up, ffn_down,
           w_in_even, qn_a, kn_a, lam_q1, lam_k1, lam_q2, lam_k2, subln_a, qn_b, kn_b,
           w_in_odd, qn_c, kn_c, sinks_c):
    depth, d = norm_mix.shape
    mod = _ada_mod(c, ada_w, ada_b)
    for l in range(depth):
        sh1, sc1, g1, sh2, sc2, g2 = [mod[l, :, None, n * d:(n + 1) * d] for n in range(6)]
        if l % 2 == 0:
            e = l // 2
            w, gain, groups, out_defs = _even_layout(w_in_even[e], qn_a[e], kn_a[e],
                                                     qn_b[e], kn_b[e], d)
            aq, ak, av, bq, iq, bk2, bv2, ik2, iwp = _proj(
                x, norm_mix[l], sc1, sh1, w, gain, groups, out_defs, tm=512)
            lam_vecs = jnp.stack([lam_q1[e], lam_k1[e], lam_q2[e], lam_k2[e]]).astype(F32)
            ya = _diff_attention_b(aq, ak, av, lam_vecs, subln_a[e], l, t=512, kw=512)
            yb = _dsa_attention_t(bq, iq, iwp, ik2, bk2, bv2, tq=512, tk=512)
            ys = (ya, yb)
        else:
            o = l // 2
            w, gain, groups, out_defs = _odd_layout(w_in_odd[o], qn_c[o], kn_c[o], d)
            q, k2, v2 = _proj(x, norm_mix[l], sc1, sh1, w, gain, groups, out_defs, tm=512)
            ys = (_swa_attention(q, k2, v2, sinks_c[o], tq=1024),)
        x = _out_ffn(x, ys, w_out[l].astype(BF16), g1, norm_ffn[l], sc2, sh2, g2,
                     ffn_gate[l].astype(BF16), ffn_up[l].astype(BF16),
                     ffn_down[l].astype(BF16), tm=512)
    return x
```

```python
import functools
import math

import jax
import jax.numpy as jnp
from jax import lax
from jax.experimental import pallas as pl
from jax.experimental.pallas import tpu as pltpu

HEAD_DIM = 64
NORM_EPS = 1e-6
TOPK_MAX = 256
WINDOW = 128

LANES = 128
SUB = 8
PACK = 16
VMEM_LIMIT = 56 * 1024 * 1024

PROJ_TM = 512
FFN_TM = 512
DIFF_T = 512
DIFF_KW = 512
DSA_TQ = 512
DSA_TK = 512
SWA_TQ = 1024
ROWS = 128
QBLK = 256
NACC = 4

NEG_BIG = -1e30
LOG2E = 1.4426950408889634
KEY_NEG_INF = -2139095041
KEY16_NEG_INF = -32641

F32 = jnp.float32
BF16 = jnp.bfloat16
_NT = (((1,), (1,)), ((), ()))


def _alibi_slopes(n):
    return [2.0 ** (-8.0 * (i + 1) / n) for i in range(n)]


def _tile_lanes(x, n):
    return x if n == 1 else jnp.concatenate([x] * n, axis=1)


def _ada_kernel(c_ref, w_ref, b_ref, o_ref):
    c = c_ref[...]
    cond = c * jax.nn.sigmoid(c)
    o_ref[0] = jnp.dot(cond, w_ref[0], preferred_element_type=F32,
                       precision=lax.Precision.HIGHEST) + b_ref[0]


def _ada_mod(c, ada_w, ada_b):
    depth, d, n = ada_w.shape
    b = c.shape[0]
    tn = 1536
    c_pad = jnp.zeros((SUB, d), F32).at[:b].set(c)
    out = pl.pallas_call(
        _ada_kernel,
        grid=(depth, n // tn),
        in_specs=[pl.BlockSpec((SUB, d), lambda l, j: (0, 0)),
                  pl.BlockSpec((1, d, tn), lambda l, j: (l, 0, j)),
                  pl.BlockSpec((1, 1, tn), lambda l, j: (l, 0, j))],
        out_specs=pl.BlockSpec((1, SUB, tn), lambda l, j: (l, 0, j)),
        out_shape=jax.ShapeDtypeStruct((depth, SUB, n), F32),
        compiler_params=pltpu.CompilerParams(
            dimension_semantics=("arbitrary", "arbitrary"), vmem_limit_bytes=VMEM_LIMIT),
        name="ada_mod",
    )(c_pad, ada_w, ada_b.reshape(depth, 1, n))
    return out[:, :b]


def _modulated_norm(x, g, sc, sh):
    ms = jnp.mean(x * x, axis=-1, keepdims=True)
    return (x * lax.rsqrt(ms + NORM_EPS) * g) * (1.0 + sc) + sh


def _proj_kernel(x_ref, g_ref, sc_ref, sh_ref, w_ref, bd_ref, gain_ref, *out_refs, groups):
    tm = x_ref.shape[1]
    half = tm // 2
    bd = bd_ref[...]
    nb = bd.shape[0]
    for rows in (slice(0, half), slice(half, tm)):
        h = _modulated_norm(x_ref[0, rows, :], g_ref[...], sc_ref[0], sh_ref[0]).astype(BF16)
        for (c0, width, parts) in groups:
            y = jnp.dot(h, w_ref[:, c0:c0 + width], preferred_element_type=F32)
            for (p0, pw, oi, o0, mode) in parts:
                step = min(nb, pw)
                for s in range(0, pw, step):
                    ys = y[:, p0 + s:p0 + s + step]
                    col = c0 + p0 + s
                    if mode == "norm":
                        ss = jnp.dot((ys * ys).astype(BF16), bd[:step, :step],
                                     preferred_element_type=F32)
                        ys = ys * lax.rsqrt(ss * (1.0 / HEAD_DIM) + NORM_EPS)
                    if mode != "plain":
                        ys = ys * gain_ref[:, col:col + step]
                    out_refs[oi][0, rows, o0 + s:o0 + s + step] = ys.astype(out_refs[oi].dtype)


def _proj(x, g, sc, sh, w, gain, groups, out_defs):
    bsz, s, d = x.shape
    c = w.shape[1]
    tm = PROJ_TM
    nb = 2 * LANES
    r = lax.broadcasted_iota(jnp.int32, (nb, nb), 0) // HEAD_DIM
    cc = lax.broadcasted_iota(jnp.int32, (nb, nb), 1) // HEAD_DIM
    bd = (r == cc).astype(BF16)
    const = lambda b, i: (0, 0)
    return pl.pallas_call(
        functools.partial(_proj_kernel, groups=groups),
        grid=(bsz, s // tm),
        in_specs=[pl.BlockSpec((1, tm, d), lambda b, i: (b, i, 0)),
                  pl.BlockSpec((1, d), const),
                  pl.BlockSpec((1, 1, d), lambda b, i: (b, 0, 0)),
                  pl.BlockSpec((1, 1, d), lambda b, i: (b, 0, 0)),
                  pl.BlockSpec((d, c), const),
                  pl.BlockSpec((nb, nb), const),
                  pl.BlockSpec((1, c), const)],
        out_specs=[pl.BlockSpec((1, tm, wd), lambda b, i: (b, i, 0)) for wd, _ in out_defs],
        out_shape=[jax.ShapeDtypeStruct((bsz, s, wd), dt) for wd, dt in out_defs],
        compiler_params=pltpu.CompilerParams(
            dimension_semantics=("arbitrary", "arbitrary"), vmem_limit_bytes=VMEM_LIMIT),
        name="in_proj",
    )(x, g.reshape(1, d), sc, sh, w, bd, gain)


def _diff_attn_kernel(q_ref, k_ref, v_ref, lam_ref, subln_ref, o_ref,
                      qs_ref, m_ref, acc_ref, *, t, kw, heads, slopes, lam_init):
    i = pl.program_id(1)
    q0 = i * t
    lane = lax.broadcasted_iota(jnp.int32, (t, LANES), 1)
    for h in range(heads):
        q = q_ref[0, :, h * LANES:(h + 1) * LANES]
        zero = jnp.zeros_like(q)
        qs_ref[h, :t] = jnp.where(lane < HEAD_DIM, q, zero)
        qs_ref[h, t:] = jnp.where(lane >= HEAD_DIM, q, zero)
    m_ref[...] = jnp.full(m_ref.shape, NEG_BIG, F32)
    acc_ref[...] = jnp.zeros(acc_ref.shape, F32)

    def step(start, width, masked):
        col = lax.broadcasted_iota(jnp.int32, (1, width), 1)
        rel = (start - q0 + col).astype(F32)
        nsl = width // LANES
        logits, values = [], []
        for h in range(heads):
            kc = k_ref[0, pl.ds(start, width), h * LANES:(h + 1) * LANES]
            vc = v_ref[0, pl.ds(start, width), h * LANES:(h + 1) * LANES]
            logits.append(lax.dot_general(qs_ref[h], kc, _NT, preferred_element_type=F32))
            values.append(jnp.concatenate([vc, jnp.ones_like(vc)], axis=1))
        for h in range(heads):
            alibi = (slopes[h] * LOG2E) * rel
            for r0 in range(0, 2 * t, ROWS):
                s = logits[h][r0:r0 + ROWS] + alibi
                if masked:
                    r = lax.broadcasted_iota(jnp.int32, (ROWS, width), 0) + (r0 % t)
                    c = lax.broadcasted_iota(jnp.int32, (ROWS, width), 1)
                    s = jnp.where(r >= c, s, NEG_BIG)
                m_prev = m_ref[h, r0:r0 + ROWS]
                m_next = jnp.maximum(m_prev, jnp.max(s, axis=1, keepdims=True))
                alpha = jnp.exp2(m_prev - m_next)
                p = jnp.exp2(s - _tile_lanes(m_next, nsl))
                acc_ref[h, r0:r0 + ROWS] = (
                    _tile_lanes(alpha, 2) * acc_ref[h, r0:r0 + ROWS]
                    + jnp.dot(p.astype(BF16), values[h], preferred_element_type=F32))
                m_ref[h, r0:r0 + ROWS] = m_next

    def body(j, carry):
        step(pl.multiple_of(j * kw, kw), kw, False)
        return carry

    nfull = q0 // kw
    lax.fori_loop(0, nfull, body, 0)

    if t < kw:
        @pl.when(q0 - nfull * kw > 0)
        def _():
            step(pl.multiple_of(q0 - t, t), t, False)

    step(pl.multiple_of(q0, t), t, True)

    lam_v = lam_ref[...]
    s1 = jnp.sum(lam_v[0:1] * lam_v[1:2], axis=-1, keepdims=True)
    s2 = jnp.sum(lam_v[2:3] * lam_v[3:4], axis=-1, keepdims=True)
    lam = jnp.exp(s1) - jnp.exp(s2) + lam_init
    for h in range(heads):
        acc = acc_ref[h]
        o = acc[:, :LANES] / acc[:, LANES:]
        y = o[:t] - lam * o[t:]
        ms = jnp.mean(y * y, axis=-1, keepdims=True)
        y = (y * lax.rsqrt(ms + NORM_EPS) * subln_ref[...]) * (1.0 - lam_init)
        o_ref[0, :, h * LANES:(h + 1) * LANES] = y.astype(o_ref.dtype)


def _diff_attention(q, k, v, lam_vecs, subln, layer_idx):
    bsz, s, width = q.shape
    heads = width // LANES
    t, kw = DIFF_T, DIFF_KW
    assert kw in (t, 2 * t)
    lam_init = 0.8 - 0.6 * math.exp(-0.3 * layer_idx)
    kern = functools.partial(_diff_attn_kernel, t=t, kw=kw, heads=heads,
                             slopes=_alibi_slopes(heads), lam_init=lam_init)
    return pl.pallas_call(
        kern,
        grid=(bsz, s // t),
        in_specs=[pl.BlockSpec((1, t, width), lambda b, i: (b, i, 0)),
                  pl.BlockSpec((1, s, width), lambda b, i: (b, 0, 0)),
                  pl.BlockSpec((1, s, width), lambda b, i: (b, 0, 0)),
                  pl.BlockSpec((4, HEAD_DIM), lambda b, i: (0, 0)),
                  pl.BlockSpec((1, LANES), lambda b, i: (0, 0))],
        out_specs=pl.BlockSpec((1, t, width), lambda b, i: (b, i, 0)),
        out_shape=jax.ShapeDtypeStruct((bsz, s, width), BF16),
        scratch_shapes=[pltpu.VMEM((heads, 2 * t, LANES), BF16),
                        pltpu.VMEM((heads, 2 * t, LANES), F32),
                        pltpu.VMEM((heads, 2 * t, 2 * LANES), F32)],
        compiler_params=pltpu.CompilerParams(
            dimension_semantics=("arbitrary", "arbitrary"), vmem_limit_bytes=VMEM_LIMIT),
        name="diff_attn",
    )(q, k, v, lam_vecs, subln.reshape(1, LANES))


def _key_to_float(k):
    return lax.bitcast_convert_type(k ^ ((k >> 31) & jnp.int32(0x7FFFFFFF)), F32)


def _floor_bf16(x):
    r = x.astype(BF16).astype(F32)
    ulp_down = jnp.where(x < 0, jnp.int32(0x10000), jnp.int32(-0x10000))
    down = lax.bitcast_convert_type(lax.bitcast_convert_type(r, jnp.int32) + ulp_down, F32)
    return jnp.where(r > x, down, r).astype(BF16)


def _dsa_kernel(bq_ref, iq_ref, iw_ref, ik_ref, bk_ref, vt_ref, o_ref,
                sc_ref, scb_ref, iqs_ref, bqs_ref, wt_ref, m_ref, acc_ref,
                *, tq, tk, heads, topk, slopes):
    i = pl.program_id(1)
    q0 = i * tq
    nch = q0 // tk + 1
    krow = lax.broadcasted_iota(jnp.int32, (tk, tq), 0)
    qcol = lax.broadcasted_iota(jnp.int32, (tk, tq), 1)

    lane = lax.broadcasted_iota(jnp.int32, (tq, LANES), 1)
    for h in range(heads):
        keep = (lane < HEAD_DIM) if h % 2 == 0 else (lane >= HEAD_DIM)
        tile = slice((h // 2) * LANES, (h // 2 + 1) * LANES)
        iqb = iq_ref[0, :, tile]
        bqb = bq_ref[0, :, tile]
        iqs_ref[h] = jnp.where(keep, iqb, jnp.zeros_like(iqb))
        bqs_ref[h * tq:(h + 1) * tq] = jnp.where(keep, bqb, jnp.zeros_like(bqb))
    wt_ref[...] = iw_ref[0].T

    def score_body(j, carry):
        start = pl.multiple_of(j * tk, tk)
        ikc = ik_ref[0, pl.ds(start, tk), :]
        sc = None
        for h in range(heads):
            logit = lax.dot_general(ikc, iqs_ref[h], _NT, preferred_element_type=F32)
            term = jnp.maximum(logit, 0.0) * wt_ref[h:h + 1, :]
            sc = term if sc is None else sc + term
        sc = sc + 0.0
        causal = (start + krow) <= (q0 + qcol)
        sc = jnp.where(causal, sc, -jnp.inf)
        sc_ref[j] = sc
        scb_ref[j] = _floor_bf16(sc)
        return carry

    lax.fori_loop(0, nch, score_body, 0)

    def count(pred_fn, thr):
        def body(j, accs):
            accs = list(accs)
            for r in range(tk // SUB):
                blk = sc_ref[j, r * SUB:(r + 1) * SUB, :]
                accs[r % NACC] = accs[r % NACC] + jnp.where(pred_fn(blk, thr), 1.0, 0.0)
            return tuple(accs)

        accs = lax.fori_loop(0, nch, body, tuple(jnp.zeros((SUB, tq), F32) for _ in range(NACC)))
        cnt = accs[0]
        for a in accs[1:]:
            cnt = cnt + a
        return jnp.broadcast_to(jnp.sum(cnt, axis=0, keepdims=True), (SUB, tq))

    ge = lambda a, b: a >= b
    gt = lambda a, b: a > b

    one_b = jnp.ones((PACK, tq), BF16)
    zero_b = jnp.zeros((PACK, tq), BF16)

    def count_b(thr_b):
        def body(j, accs):
            accs = list(accs)
            for r in range(tk // PACK):
                blk = scb_ref[j, r * PACK:(r + 1) * PACK, :]
                accs[r % NACC] = accs[r % NACC] + jnp.where(blk >= thr_b, one_b, zero_b)
            return tuple(accs)

        accs = lax.fori_loop(0, nch, body, tuple(zero_b for _ in range(NACC)))
        cnt = accs[0].astype(F32)
        for a in accs[1:]:
            cnt = cnt + a.astype(F32)
        return jnp.broadcast_to(jnp.sum(cnt, axis=0, keepdims=True), (SUB, tq))

    def key16_to_bits(k):
        return lax.shift_left(k ^ ((k >> 31) & jnp.int32(0x7FFF)), 16)

    def hi_body(b, key):
        cand = key + lax.shift_left(jnp.int32(1), 15 - b)
        cf = lax.bitcast_convert_type(key16_to_bits(cand), F32)
        thr_b = jnp.concatenate([cf, cf], axis=0).astype(BF16)
        return jnp.where(count_b(thr_b) >= float(topk), cand, key)

    key16 = lax.fori_loop(0, 16, hi_body, jnp.full((SUB, tq), -2 ** 15, jnp.int32))
    bits = key16_to_bits(key16)
    base = bits ^ ((bits >> 31) & jnp.int32(0x7FFFFFFF))
    base = jnp.where(key16 <= KEY16_NEG_INF, KEY_NEG_INF, base)

    def lo_body(b, off):
        step = lax.shift_left(jnp.int32(1), 15 - b)
        ok = count(ge, _key_to_float(base + off + step)) >= float(topk)
        return jnp.where(ok, off + step, off)

    key = base + lax.fori_loop(0, 16, lo_body, jnp.zeros((SUB, tq), jnp.int32))
    thr = jnp.where(key <= KEY_NEG_INF, -jnp.inf, _key_to_float(key))
    n_ge = count(ge, thr)
    has_excess_ties = jnp.max(jnp.where(n_ge > float(topk), 1.0, 0.0)) > 0.0
    thr_row = thr[0:1]

    @pl.when(jnp.logical_not(has_excess_ties))
    def _():
        def body(j, carry):
            start = pl.multiple_of(j * tk, tk)
            keep = (sc_ref[j] >= thr_row) & ((start + krow) <= (q0 + qcol))
            sc_ref[j] = jnp.where(keep, 0.0, NEG_BIG)
            return carry
        lax.fori_loop(0, nch, body, 0)

    @pl.when(has_excess_ties)
    def _():
        need = (float(topk) - count(gt, thr))[0:1]
        r2 = lax.broadcasted_iota(jnp.int32, (tk, tk), 0)
        c2 = lax.broadcasted_iota(jnp.int32, (tk, tk), 1)
        lower = jnp.where(c2 <= r2, 1.0, 0.0).astype(BF16)

        def body(j, seen):
            start = pl.multiple_of(j * tk, tk)
            sc = sc_ref[j]
            eq = sc == thr_row
            eqf = jnp.where(eq, 1.0, 0.0)
            rank = seen + jnp.dot(lower, eqf.astype(BF16), preferred_element_type=F32)
            keep = ((sc > thr_row) | (eq & (rank <= need))) & ((start + krow) <= (q0 + qcol))
            sc_ref[j] = jnp.where(keep, 0.0, NEG_BIG)
            return seen + jnp.sum(eqf, axis=0, keepdims=True)

        lax.fori_loop(0, nch, body, jnp.zeros((1, tq), F32))

    m_ref[...] = jnp.full(m_ref.shape, NEG_BIG, F32)
    acc_ref[...] = jnp.zeros(acc_ref.shape, F32)
    krow_t = lax.broadcasted_iota(jnp.int32, (tk, LANES), 0)

    def attn_body(j, carry):
        start = pl.multiple_of(j * tk, tk)
        kc = bk_ref[0, pl.ds(start, tk), :]
        vt = vt_ref[0, j]
        rel = (start - q0 + krow_t).astype(F32)

        def qk(h):
            return lax.dot_general(kc, bqs_ref[h * tq:(h + 1) * tq], _NT,
                                   preferred_element_type=F32)

        logits = {0: qk(0)}
        for h in range(heads):
            if h + 1 < heads:
                logits[h + 1] = qk(h + 1)
            logit = logits.pop(h)
            alibi = _tile_lanes((slopes[h] * LOG2E) * rel, QBLK // LANES)
            for c0 in range(0, tq, QBLK):
                s = logit[:, c0:c0 + QBLK] + sc_ref[j, :, c0:c0 + QBLK] + alibi
                m_prev = m_ref[h, :, c0:c0 + QBLK]
                m_next = jnp.maximum(m_prev, jnp.max(s, axis=0, keepdims=True))
                alpha = jnp.exp2(m_prev - m_next)
                p = jnp.exp2(s - m_next).astype(BF16)
                acc_ref[h, :, c0:c0 + QBLK] = (alpha * acc_ref[h, :, c0:c0 + QBLK]
                                               + jnp.dot(vt, p, preferred_element_type=F32))
                m_ref[h, :, c0:c0 + QBLK] = m_next
        return carry

    lax.fori_loop(0, nch, attn_body, 0)

    for g in range(heads // 2):
        a = acc_ref[2 * g]
        b = acc_ref[2 * g + 1]
        pair = jnp.concatenate([a[:HEAD_DIM] / a[HEAD_DIM:HEAD_DIM + 1],
                                b[:HEAD_DIM] / b[HEAD_DIM:HEAD_DIM + 1]], axis=0)
        o_ref[0, :, g * LANES:(g + 1) * LANES] = pair.T.astype(o_ref.dtype)


def _dsa_attention(bq, iq, iwp, ik2, bk2, bv2):
    bsz, s, width = bq.shape
    heads = width // HEAD_DIM
    tq, tk = DSA_TQ, DSA_TK
    topk = min(TOPK_MAX, s // 4)
    nck = s // tk
    kern = functools.partial(_dsa_kernel, tq=tq, tk=tk, heads=heads, topk=topk,
                             slopes=_alibi_slopes(heads))
    vt = bv2[:, :, :HEAD_DIM].reshape(bsz, nck, tk, HEAD_DIM).transpose(0, 1, 3, 2)
    vt = jnp.concatenate([vt, jnp.ones_like(vt)], axis=2)
    qspec = lambda w: pl.BlockSpec((1, tq, w), lambda b, i: (b, i, 0))
    kspec = pl.BlockSpec((1, s, LANES), lambda b, i: (b, 0, 0))
    return pl.pallas_call(
        kern,
        grid=(bsz, s // tq),
        in_specs=[qspec(width), qspec(width), qspec(LANES), kspec, kspec,
                  pl.BlockSpec((1, nck, LANES, tk), lambda b, i: (b, 0, 0, 0))],
        out_specs=pl.BlockSpec((1, tq, width), lambda b, i: (b, i, 0)),
        out_shape=jax.ShapeDtypeStruct((bsz, s, width), BF16),
        scratch_shapes=[pltpu.VMEM((nck, tk, tq), F32),
                        pltpu.VMEM((nck, tk, tq), BF16),
                        pltpu.VMEM((heads, tq, LANES), BF16),
                        pltpu.VMEM((heads * tq, LANES), BF16),
                        pltpu.VMEM((LANES, tq), F32),
                        pltpu.VMEM((heads, 1, tq), F32),
                        pltpu.VMEM((heads, LANES, tq), F32)],
        compiler_params=pltpu.CompilerParams(
            dimension_semantics=("arbitrary", "arbitrary"), vmem_limit_bytes=VMEM_LIMIT),
        name="dsa_attn",
    )(bq, iq, iwp, ik2, bk2, vt)


def _stack_heads(x, heads):
    lane = lax.broadcasted_iota(jnp.int32, (x.shape[0], LANES), 1)
    lo = lane < HEAD_DIM
    parts = []
    for h in range(heads):
        blk = x[:, (h // 2) * LANES:(h // 2 + 1) * LANES]
        keep = lo if h % 2 == 0 else jnp.logical_not(lo)
        parts.append(jnp.where(keep, blk, jnp.zeros_like(blk)))
    return jnp.concatenate(parts, axis=0)


def _swa_kernel(slope_ref, sink_ref, q_ref, kp_ref, kc_ref, vp_ref, vc_ref, o_ref, *, tq, group):
    kv = pl.program_id(1)
    i = pl.program_id(2)
    nk = 2 * WINDOW
    kk = jnp.concatenate([kp_ref[0], kc_ref[0]], axis=0)
    vv = jnp.concatenate([vp_ref[0], vc_ref[0]], axis=0)
    r = lax.broadcasted_iota(jnp.int32, (WINDOW, nk), 0)
    c = lax.broadcasted_iota(jnp.int32, (WINDOW, nk), 1)
    dist = WINDOW + r - c
    band = (dist >= 0) & (dist < WINDOW)
    distf = dist.astype(F32)
    lane = lax.broadcasted_iota(jnp.int32, (WINDOW, LANES), 1)
    biases = [jnp.where(band, -(slope_ref[kv * group + g] * LOG2E) * distf, NEG_BIG)
              for g in range(group)]
    for blk in range(tq // WINDOW):
        rows = slice(blk * WINDOW, (blk + 1) * WINDOW)
        qs = _stack_heads(q_ref[0, rows, :], group)
        kb = kk[blk * WINDOW:blk * WINDOW + nk]
        vb = vv[blk * WINDOW:blk * WINDOW + nk]
        logit = lax.dot_general(qs, kb, _NT, preferred_element_type=F32)
        outs = []
        for g in range(group):
            hq = kv * group + g
            s = logit[g * WINDOW:(g + 1) * WINDOW] + biases[g]
            if blk == 0:
                s = jnp.where(c >= jnp.where(i == 0, WINDOW, 0), s, NEG_BIG)
            sink = sink_ref[hq] * LOG2E
            m = jnp.maximum(jnp.max(s, axis=1, keepdims=True), sink)
            p = jnp.exp2(s - m)
            denom = jnp.sum(p, axis=1, keepdims=True) + jnp.exp2(sink - m)
            outs.append(jnp.dot(p.astype(BF16), vb, preferred_element_type=F32) / denom)
        for g in range(group // 2):
            o_ref[0, rows, g * LANES:(g + 1) * LANES] = jnp.where(
                lane < HEAD_DIM, outs[2 * g], outs[2 * g + 1]).astype(o_ref.dtype)


def _swa_attention(q, k2, v2, sinks):
    bsz, s, width = q.shape
    heads = width // HEAD_DIM
    kvh = k2.shape[2] // LANES
    group = heads // kvh
    tq = min(SWA_TQ, s)
    slopes = jnp.asarray(_alibi_slopes(heads), F32)
    r = tq // WINDOW
    prev = lambda b, kv, i: (b, jnp.maximum(i * r - 1, 0), kv)
    cur = lambda b, kv, i: (b, i, kv)
    smem = pl.BlockSpec(memory_space=pltpu.SMEM)
    return pl.pallas_call(
        functools.partial(_swa_kernel, tq=tq, group=group),
        grid=(bsz, kvh, s // tq),
        in_specs=[smem, smem,
                  pl.BlockSpec((1, tq, group * HEAD_DIM), cur),
                  pl.BlockSpec((1, WINDOW, LANES), prev),
                  pl.BlockSpec((1, tq, LANES), cur),
                  pl.BlockSpec((1, WINDOW, LANES), prev),
                  pl.BlockSpec((1, tq, LANES), cur)],
        out_specs=pl.BlockSpec((1, tq, group * HEAD_DIM), cur),
        out_shape=jax.ShapeDtypeStruct((bsz, s, width), BF16),
        compiler_params=pltpu.CompilerParams(
            dimension_semantics=("arbitrary", "arbitrary", "arbitrary"),
            vmem_limit_bytes=VMEM_LIMIT),
        name="swa_attn",
    )(slopes, sinks.astype(F32), q, k2, k2, v2, v2)


def _out_ffn_kernel(*refs, ny):
    x_ref = refs[0]
    y_refs = refs[1:1 + ny]
    wo_ref, g1_ref, n_ref, sc_ref, sh_ref, g2_ref, wg_ref, wu_ref, wd_ref, o_ref = refs[1 + ny:]
    tm = x_ref.shape[1]
    half = tm // 2
    for rows in (slice(0, half), slice(half, tm)):
        mix = None
        r0 = 0
        for y_ref in y_refs:
            w = y_ref.shape[2]
            part = jnp.dot(y_ref[0, rows, :], wo_ref[r0:r0 + w, :], preferred_element_type=F32)
            mix = part if mix is None else mix + part
            r0 += w
        x = x_ref[0, rows, :] + g1_ref[0] * mix
        h = _modulated_norm(x, n_ref[...], sc_ref[0], sh_ref[0]).astype(BF16)
        gate = jnp.dot(h, wg_ref[...], preferred_element_type=F32)
        up = jnp.dot(h, wu_ref[...], preferred_element_type=F32)
        act = (gate * jax.nn.sigmoid(gate) * up).astype(BF16)
        o_ref[0, rows, :] = x + g2_ref[0] * jnp.dot(act, wd_ref[...], preferred_element_type=F32)


def _out_ffn(x, ys, wo, g1, n, sc, sh, g2, wg, wu, wd):
    bsz, s, d = x.shape
    f = wg.shape[1]
    tm = FFN_TM
    const = lambda b, i: (0, 0)
    tok = lambda b, i: (b, i, 0)
    mod = lambda b, i: (b, 0, 0)
    once = pl.Buffered(1)
    return pl.pallas_call(
        functools.partial(_out_ffn_kernel, ny=len(ys)),
        grid=(bsz, s // tm),
        in_specs=[pl.BlockSpec((1, tm, d), tok)]
                 + [pl.BlockSpec((1, tm, y.shape[2]), tok) for y in ys]
                 + [pl.BlockSpec((d, d), const, pipeline_mode=once),
                  pl.BlockSpec((1, 1, d), mod),
                  pl.BlockSpec((1, d), const),
                  pl.BlockSpec((1, 1, d), mod),
                  pl.BlockSpec((1, 1, d), mod),
                  pl.BlockSpec((1, 1, d), mod),
                  pl.BlockSpec((d, f), const, pipeline_mode=once),
                  pl.BlockSpec((d, f), const, pipeline_mode=once),
                  pl.BlockSpec((f, d), const, pipeline_mode=once)],
        out_specs=pl.BlockSpec((1, tm, d), tok),
        out_shape=jax.ShapeDtypeStruct((bsz, s, d), F32),
        compiler_params=pltpu.CompilerParams(
            dimension_semantics=("arbitrary", "arbitrary"), vmem_limit_bytes=VMEM_LIMIT),
        name="out_ffn",
    )(x, *ys, wo, g1, n.reshape(1, d), sc, sh, g2, wg, wu, wd)


def _dup(w):
    d, c = w.shape
    w = w.reshape(d, c // HEAD_DIM, 1, HEAD_DIM)
    return jnp.broadcast_to(w, (d, c // HEAD_DIM, 2, HEAD_DIM)).reshape(d, 2 * c)


def _even_layout(w_in, qn_a, kn_a, qn_b, kn_b, d):
    a = d // 2
    sizes = (a, a, a, a, HEAD_DIM, HEAD_DIM, a, HEAD_DIM, a // HEAD_DIM)
    offs = [0]
    for sz in sizes:
        offs.append(offs[-1] + sz)
    aq, ak, av, bq, bk, bv, iq, ik, iw = [w_in[:, offs[n]:offs[n + 1]] for n in range(9)]
    nh = a // HEAD_DIM
    iw_pad = jnp.zeros((d, LANES), w_in.dtype).at[:, :nh].set(iw)
    w = jnp.concatenate([aq, ak, av, bq, iq, _dup(bk), _dup(bv), _dup(ik), iw_pad], axis=1)
    qscale = HEAD_DIM ** -0.5 * LOG2E
    ones = lambda n: jnp.ones((n,), F32)
    gain = jnp.concatenate([
        jnp.tile(qn_a, nh) * qscale, jnp.tile(kn_a, nh), ones(a),
        jnp.tile(qn_b, nh) * qscale, ones(a) * HEAD_DIM ** -0.5,
        jnp.tile(kn_b, 2), ones(LANES), ones(LANES), ones(LANES) * nh ** -0.5])
    groups = (
        (0, a, ((0, a, 0, 0, "norm"),)),
        (a, a, ((0, a, 1, 0, "norm"),)),
        (2 * a, a, ((0, a, 2, 0, "plain"),)),
        (3 * a, a, ((0, a, 3, 0, "norm"),)),
        (4 * a, a, ((0, a, 4, 0, "scale"),)),
        (5 * a, 4 * LANES, ((0, LANES, 5, 0, "norm"), (LANES, LANES, 6, 0, "plain"),
                            (2 * LANES, LANES, 7, 0, "plain"), (3 * LANES, LANES, 8, 0, "scale"))),
    )
    out_defs = ((a, BF16), (a, BF16), (a, BF16), (a, BF16), (a, BF16),
                (LANES, BF16), (LANES, BF16), (LANES, BF16), (LANES, F32))
    return w.astype(BF16), gain.reshape(1, -1), groups, out_defs


def _odd_layout(w_in, qn_c, kn_c, d):
    kvw = d // 4
    q, k, v = w_in[:, :d], w_in[:, d:d + kvw], w_in[:, d + kvw:]
    w = jnp.concatenate([q, _dup(k), _dup(v)], axis=1)
    nh = d // HEAD_DIM
    gain = jnp.concatenate([jnp.tile(qn_c, nh) * (HEAD_DIM ** -0.5 * LOG2E),
                            jnp.tile(kn_c, 2 * kvw // HEAD_DIM), jnp.ones((2 * kvw,), F32)])
    half = d // 2
    groups = (
        (0, half, ((0, half, 0, 0, "norm"),)),
        (half, half, ((0, half, 0, half, "norm"),)),
        (d, 2 * kvw, ((0, 2 * kvw, 1, 0, "norm"),)),
        (d + 2 * kvw, 2 * kvw, ((0, 2 * kvw, 2, 0, "plain"),)),
    )
    out_defs = ((d, BF16), (2 * kvw, BF16), (2 * kvw, BF16))
    return w.astype(BF16), gain.reshape(1, -1), groups, out_defs


def kernel(x, c, ada_w, ada_b, norm_mix, norm_ffn, w_out, ffn_gate, ffn_up, ffn_down,
           w_in_even, qn_a, kn_a, lam_q1, lam_k1, lam_q2, lam_k2, subln_a, qn_b, kn_b,
           w_in_odd, qn_c, kn_c, sinks_c):
    depth, d = norm_mix.shape
    mod = _ada_mod(c, ada_w, ada_b)
    for l in range(depth):
        sh1, sc1, g1, sh2, sc2, g2 = [mod[l, :, None, n * d:(n + 1) * d] for n in range(6)]
        if l % 2 == 0:
            e = l // 2
            w, gain, groups, out_defs = _even_layout(w_in_even[e], qn_a[e], kn_a[e],
                                                     qn_b[e], kn_b[e], d)
            aq, ak, av, bq, iq, bk2, bv2, ik2, iwp = _proj(
                x, norm_mix[l], sc1, sh1, w, gain, groups, out_defs)
            lam_vecs = jnp.stack([lam_q1[e], lam_k1[e], lam_q2[e], lam_k2[e]]).astype(F32)
            ya = _diff_attention(aq, ak, av, lam_vecs, subln_a[e], l)
            yb = _dsa_attention(bq, iq, iwp, ik2, bk2, bv2)
            ys = (ya, yb)
        else:
            o = l // 2
            w, gain, groups, out_defs = _odd_layout(w_in_odd[o], qn_c[o], kn_c[o], d)
            q, k2, v2 = _proj(x, norm_mix[l], sc1, sh1, w, gain, groups, out_defs)
            ys = (_swa_attention(q, k2, v2, sinks_c[o]),)
        x = _out_ffn(x, ys, w_out[l].astype(BF16), g1, norm_ffn[l], sc2, sh2, g2,
                     ffn_gate[l].astype(BF16), ffn_up[l].astype(BF16),
                     ffn_down[l].astype(BF16))
    return x
```

```python
import functools
import math

import jax
import jax.numpy as jnp
from jax import lax
from jax.experimental import pallas as pl
from jax.experimental.pallas import tpu as pltpu

HEAD_DIM = 64
NORM_EPS = 1e-6
TOPK_MAX = 256
WINDOW = 128

LANES = 128
SUB = 8
PACK = 16
VMEM_LIMIT = 56 * 1024 * 1024

PROJ_TM = 512
FFN_TM = 512
DIFF_T = 512
DIFF_KW = 512
DSA_TQ = 512
DSA_TK = 512
SWA_TQ = 1024
ROWS = 128
QBLK = 256
NACC = 4

NEG_BIG = -1e30
LOG2E = 1.4426950408889634
KEY_NEG_INF = -2139095041
KEY16_NEG_INF = -32641

F32 = jnp.float32
BF16 = jnp.bfloat16
_NT = (((1,), (1,)), ((), ()))


def _alibi_slopes(n):
    return [2.0 ** (-8.0 * (i + 1) / n) for i in range(n)]


def _tile_lanes(x, n):
    return x if n == 1 else jnp.concatenate([x] * n, axis=1)


def _ada_kernel(c_ref, w_ref, b_ref, o_ref):
    c = c_ref[...]
    cond = c * jax.nn.sigmoid(c)
    o_ref[0] = jnp.dot(cond, w_ref[0], preferred_element_type=F32,
                       precision=lax.Precision.HIGHEST) + b_ref[0]


def _ada_mod(c, ada_w, ada_b):
    depth, d, n = ada_w.shape
    b = c.shape[0]
    tn = 1536
    c_pad = jnp.zeros((SUB, d), F32).at[:b].set(c)
    out = pl.pallas_call(
        _ada_kernel,
        grid=(depth, n // tn),
        in_specs=[pl.BlockSpec((SUB, d), lambda l, j: (0, 0)),
                  pl.BlockSpec((1, d, tn), lambda l, j: (l, 0, j)),
                  pl.BlockSpec((1, 1, tn), lambda l, j: (l, 0, j))],
        out_specs=pl.BlockSpec((1, SUB, tn), lambda l, j: (l, 0, j)),
        out_shape=jax.ShapeDtypeStruct((depth, SUB, n), F32),
        compiler_params=pltpu.CompilerParams(
            dimension_semantics=("arbitrary", "arbitrary"), vmem_limit_bytes=VMEM_LIMIT),
        name="ada_mod",
    )(c_pad, ada_w, ada_b.reshape(depth, 1, n))
    return out[:, :b]


def _modulated_norm(x, g, sc, sh):
    ms = jnp.mean(x * x, axis=-1, keepdims=True)
    return (x * lax.rsqrt(ms + NORM_EPS) * g) * (1.0 + sc) + sh


def _proj_kernel(x_ref, g_ref, sc_ref, sh_ref, w_ref, bd_ref, gain_ref, *out_refs, groups):
    tm = x_ref.shape[1]
    half = tm // 2
    bd = bd_ref[...]
    nb = bd.shape[0]
    for rows in (slice(0, half), slice(half, tm)):
        h = _modulated_norm(x_ref[0, rows, :], g_ref[...], sc_ref[0], sh_ref[0]).astype(BF16)
        for (c0, width, parts) in groups:
            y = jnp.dot(h, w_ref[:, c0:c0 + width], preferred_element_type=F32)
            for (p0, pw, oi, o0, mode) in parts:
                step = min(nb, pw)
                for s in range(0, pw, step):
                    ys = y[:, p0 + s:p0 + s + step]
                    col = c0 + p0 + s
                    if mode == "norm":
                        ss = jnp.dot((ys * ys).astype(BF16), bd[:step, :step],
                                     preferred_element_type=F32)
                        ys = ys * lax.rsqrt(ss * (1.0 / HEAD_DIM) + NORM_EPS)
                    if mode != "plain":
                        ys = ys * gain_ref[:, col:col + step]
                    out_refs[oi][0, rows, o0 + s:o0 + s + step] = ys.astype(out_refs[oi].dtype)


def _proj(x, g, sc, sh, w, gain, groups, out_defs):
    bsz, s, d = x.shape
    c = w.shape[1]
    tm = PROJ_TM
    nb = 2 * LANES
    r = lax.broadcasted_iota(jnp.int32, (nb, nb), 0) // HEAD_DIM
    cc = lax.broadcasted_iota(jnp.int32, (nb, nb), 1) // HEAD_DIM
    bd = (r == cc).astype(BF16)
    const = lambda b, i: (0, 0)
    return pl.pallas_call(
        functools.partial(_proj_kernel, groups=groups),
        grid=(bsz, s // tm),
        in_specs=[pl.BlockSpec((1, tm, d), lambda b, i: (b, i, 0)),
                  pl.BlockSpec((1, d), const),
                  pl.BlockSpec((1, 1, d), lambda b, i: (b, 0, 0)),
                  pl.BlockSpec((1, 1, d), lambda b, i: (b, 0, 0)),
                  pl.BlockSpec((d, c), const),
                  pl.BlockSpec((nb, nb), const),
                  pl.BlockSpec((1, c), const)],
        out_specs=[pl.BlockSpec((1, tm, wd), lambda b, i: (b, i, 0)) for wd, _ in out_defs],
        out_shape=[jax.ShapeDtypeStruct((bsz, s, wd), dt) for wd, dt in out_defs],
        compiler_params=pltpu.CompilerParams(
            dimension_semantics=("arbitrary", "arbitrary"), vmem_limit_bytes=VMEM_LIMIT),
        name="in_proj",
    )(x, g.reshape(1, d), sc, sh, w, bd, gain)


def _diff_attn_kernel(q_ref, k_ref, v_ref, lam_ref, subln_ref, o_ref,
                      qs_ref, m_ref, acc_ref, *, t, kw, heads, slopes, lam_init):
    i = pl.program_id(1)
    q0 = i * t
    lane = lax.broadcasted_iota(jnp.int32, (t, LANES), 1)
    for h in range(heads):
        q = q_ref[0, :, h * LANES:(h + 1) * LANES]
        zero = jnp.zeros_like(q)
        qs_ref[h, :t] = jnp.where(lane < HEAD_DIM, q, zero)
        qs_ref[h, t:] = jnp.where(lane >= HEAD_DIM, q, zero)
    m_ref[...] = jnp.full(m_ref.shape, NEG_BIG, F32)
    acc_ref[...] = jnp.zeros(acc_ref.shape, F32)

    def step(start, width, masked):
        col = lax.broadcasted_iota(jnp.int32, (1, width), 1)
        rel = (start - q0 + col).astype(F32)
        nsl = width // LANES
        logits, values = [], []
        for h in range(heads):
            kc = k_ref[0, pl.ds(start, width), h * LANES:(h + 1) * LANES]
            vc = v_ref[0, pl.ds(start, width), h * LANES:(h + 1) * LANES]
            logits.append(lax.dot_general(qs_ref[h], kc, _NT, preferred_element_type=F32))
            values.append(jnp.concatenate([vc, jnp.ones_like(vc)], axis=1))
        for h in range(heads):
            alibi = (slopes[h] * LOG2E) * rel
            for r0 in range(0, 2 * t, ROWS):
                s = logits[h][r0:r0 + ROWS] + alibi
                if masked:
                    r = lax.broadcasted_iota(jnp.int32, (ROWS, width), 0) + (r0 % t)
                    c = lax.broadcasted_iota(jnp.int32, (ROWS, width), 1)
                    s = jnp.where(r >= c, s, NEG_BIG)
                m_prev = m_ref[h, r0:r0 + ROWS]
                m_next = jnp.maximum(m_prev, jnp.max(s, axis=1, keepdims=True))
                alpha = jnp.exp2(m_prev - m_next)
                p = jnp.exp2(s - _tile_lanes(m_next, nsl))
                acc_ref[h, r0:r0 + ROWS] = (
                    _tile_lanes(alpha, 2) * acc_ref[h, r0:r0 + ROWS]
                    + jnp.dot(p.astype(BF16), values[h], preferred_element_type=F32))
                m_ref[h, r0:r0 + ROWS] = m_next

    def body(j, carry):
        step(pl.multiple_of(j * kw, kw), kw, False)
        return carry

    nfull = q0 // kw
    lax.fori_loop(0, nfull, body, 0)

    if t < kw:
        @pl.when(q0 - nfull * kw > 0)
        def _():
            step(pl.multiple_of(q0 - t, t), t, False)

    step(pl.multiple_of(q0, t), t, True)

    lam_v = lam_ref[...]
    s1 = jnp.sum(lam_v[0:1] * lam_v[1:2], axis=-1, keepdims=True)
    s2 = jnp.sum(lam_v[2:3] * lam_v[3:4], axis=-1, keepdims=True)
    lam = jnp.exp(s1) - jnp.exp(s2) + lam_init
    for h in range(heads):
        acc = acc_ref[h]
        o = acc[:, :LANES] / acc[:, LANES:]
        y = o[:t] - lam * o[t:]
        ms = jnp.mean(y * y, axis=-1, keepdims=True)
        y = (y * lax.rsqrt(ms + NORM_EPS) * subln_ref[...]) * (1.0 - lam_init)
        o_ref[0, :, h * LANES:(h + 1) * LANES] = y.astype(o_ref.dtype)


def _diff_attention(q, k, v, lam_vecs, subln, layer_idx):
    bsz, s, width = q.shape
    heads = width // LANES
    t, kw = DIFF_T, DIFF_KW
    assert kw in (t, 2 * t)
    lam_init = 0.8 - 0.6 * math.exp(-0.3 * layer_idx)
    kern = functools.partial(_diff_attn_kernel, t=t, kw=kw, heads=heads,
                             slopes=_alibi_slopes(heads), lam_init=lam_init)
    return pl.pallas_call(
        kern,
        grid=(bsz, s // t),
        in_specs=[pl.BlockSpec((1, t, width), lambda b, i: (b, i, 0)),
                  pl.BlockSpec((1, s, width), lambda b, i: (b, 0, 0)),
                  pl.BlockSpec((1, s, width), lambda b, i: (b, 0, 0)),
                  pl.BlockSpec((4, HEAD_DIM), lambda b, i: (0, 0)),
                  pl.BlockSpec((1, LANES), lambda b, i: (0, 0))],
        out_specs=pl.BlockSpec((1, t, width), lambda b, i: (b, i, 0)),
        out_shape=jax.ShapeDtypeStruct((bsz, s, width), BF16),
        scratch_shapes=[pltpu.VMEM((heads, 2 * t, LANES), BF16),
                        pltpu.VMEM((heads, 2 * t, LANES), F32),
                        pltpu.VMEM((heads, 2 * t, 2 * LANES), F32)],
        compiler_params=pltpu.CompilerParams(
            dimension_semantics=("arbitrary", "arbitrary"), vmem_limit_bytes=VMEM_LIMIT),
        name="diff_attn",
    )(q, k, v, lam_vecs, subln.reshape(1, LANES))


def _key_to_float(k):
    return lax.bitcast_convert_type(k ^ ((k >> 31) & jnp.int32(0x7FFFFFFF)), F32)


def _floor_bf16(x):
    r = x.astype(BF16).astype(F32)
    ulp_down = jnp.where(x < 0, jnp.int32(0x10000), jnp.int32(-0x10000))
    down = lax.bitcast_convert_type(lax.bitcast_convert_type(r, jnp.int32) + ulp_down, F32)
    return jnp.where(r > x, down, r).astype(BF16)


def _dsa_kernel(bq_ref, iq_ref, iw_ref, ik_ref, bk_ref, vt_ref, o_ref,
                sc_ref, scb_ref, iqs_ref, bqs_ref, wt_ref, m_ref, acc_ref,
                *, tq, tk, heads, topk, slopes):
    i = pl.program_id(1)
    q0 = i * tq
    nch = q0 // tk + 1
    krow = lax.broadcasted_iota(jnp.int32, (tk, tq), 0)
    qcol = lax.broadcasted_iota(jnp.int32, (tk, tq), 1)

    lane = lax.broadcasted_iota(jnp.int32, (tq, LANES), 1)
    for h in range(heads):
        keep = (lane < HEAD_DIM) if h % 2 == 0 else (lane >= HEAD_DIM)
        tile = slice((h // 2) * LANES, (h // 2 + 1) * LANES)
        iqb = iq_ref[0, :, tile]
        bqb = bq_ref[0, :, tile]
        iqs_ref[h] = jnp.where(keep, iqb, jnp.zeros_like(iqb))
        bqs_ref[h * tq:(h + 1) * tq] = jnp.where(keep, bqb, jnp.zeros_like(bqb))
    wt_ref[...] = iw_ref[0].T

    def scores(j, r0, nrows, c_lo, band):
        start = pl.multiple_of(j * tk, tk)
        ikc = ik_ref[0, pl.ds(start + r0, nrows), :]
        sc = None
        for h in range(heads):
            logit = lax.dot_general(ikc, iqs_ref[h, c_lo:, :], _NT,
                                    preferred_element_type=F32)
            term = jnp.maximum(logit, 0.0) * wt_ref[h:h + 1, c_lo:]
            sc = term if sc is None else sc + term
        sc = sc + 0.0
        if band:
            kr = lax.broadcasted_iota(jnp.int32, sc.shape, 0) + (r0 - c_lo)
            qc = lax.broadcasted_iota(jnp.int32, sc.shape, 1)
            sc = jnp.where(kr <= qc, sc, -jnp.inf)
            if c_lo:
                sc = jnp.concatenate([jnp.full((nrows, c_lo), -jnp.inf, F32), sc], axis=1)
        sc_ref[j, r0:r0 + nrows, :] = sc
        scb_ref[j, r0:r0 + nrows, :] = _floor_bf16(sc)

    def score_body(j, carry):
        scores(j, 0, tk, 0, False)
        return carry

    lax.fori_loop(0, nch - 1, score_body, 0)
    for r0 in range(0, tk, QBLK):
        scores(nch - 1, r0, QBLK, r0, True)

    def count_in(ref, rows, pred_fn, thr):
        dt = ref.dtype

        def add_rows(j, accs, r_lo, r_hi, c_lo):
            accs = list(accs)
            for r in range(r_lo // rows, r_hi // rows):
                blk = ref[j, r * rows:(r + 1) * rows, c_lo:]
                hit = jnp.where(pred_fn(blk, thr[:, c_lo:]), jnp.ones(blk.shape, dt),
                                jnp.zeros(blk.shape, dt))
                a = accs[r % NACC]
                accs[r % NACC] = (a + hit if c_lo == 0 else
                                  jnp.concatenate([a[:, :c_lo], a[:, c_lo:] + hit], axis=1))
            return tuple(accs)

        accs = lax.fori_loop(0, nch - 1, lambda j, accs: add_rows(j, accs, 0, tk, 0),
                             tuple(jnp.zeros((rows, tq), dt) for _ in range(NACC)))
        for r0 in range(0, tk, LANES):
            accs = add_rows(nch - 1, accs, r0, r0 + LANES, r0)
        cnt = accs[0].astype(F32)
        for a in accs[1:]:
            cnt = cnt + a.astype(F32)
        return jnp.broadcast_to(jnp.sum(cnt, axis=0, keepdims=True), (SUB, tq))

    def count(pred_fn, thr):
        return count_in(sc_ref, SUB, pred_fn, thr)

    ge = lambda a, b: a >= b
    gt = lambda a, b: a > b

    def count_b(thr_b):
        return count_in(scb_ref, PACK, ge, thr_b)

    def key16_to_bits(k):
        return lax.shift_left(k ^ ((k >> 31) & jnp.int32(0x7FFF)), 16)

    def hi_body(b, key):
        cand = key + lax.shift_left(jnp.int32(1), 15 - b)
        cf = lax.bitcast_convert_type(key16_to_bits(cand), F32)
        thr_b = jnp.concatenate([cf, cf], axis=0).astype(BF16)
        return jnp.where(count_b(thr_b) >= float(topk), cand, key)

    key16 = lax.fori_loop(0, 16, hi_body, jnp.full((SUB, tq), -2 ** 15, jnp.int32))
    bits = key16_to_bits(key16)
    base = bits ^ ((bits >> 31) & jnp.int32(0x7FFFFFFF))
    base = jnp.where(key16 <= KEY16_NEG_INF, KEY_NEG_INF, base)

    def lo_body(b, off):
        step = lax.shift_left(jnp.int32(1), 15 - b)
        ok = count(ge, _key_to_float(base + off + step)) >= float(topk)
        return jnp.where(ok, off + step, off)

    key = base + lax.fori_loop(0, 16, lo_body, jnp.zeros((SUB, tq), jnp.int32))
    thr = jnp.where(key <= KEY_NEG_INF, -jnp.inf, _key_to_float(key))
    n_ge = count(ge, thr)
    has_excess_ties = jnp.max(jnp.where(n_ge > float(topk), 1.0, 0.0)) > 0.0
    thr_row = thr[0:1]

    @pl.when(jnp.logical_not(has_excess_ties))
    def _():
        def body(j, carry):
            sc_ref[j] = jnp.where(sc_ref[j] >= thr_row, 0.0, NEG_BIG)
            return carry
        lax.fori_loop(0, nch - 1, body, 0)
        last = nch - 1
        keep = (sc_ref[last] >= thr_row) & ((last * tk + krow) <= (q0 + qcol))
        sc_ref[last] = jnp.where(keep, 0.0, NEG_BIG)

    @pl.when(has_excess_ties)
    def _():
        need = (float(topk) - count(gt, thr))[0:1]
        r2 = lax.broadcasted_iota(jnp.int32, (tk, tk), 0)
        c2 = lax.broadcasted_iota(jnp.int32, (tk, tk), 1)
        lower = jnp.where(c2 <= r2, 1.0, 0.0).astype(BF16)

        def body(j, seen):
            start = pl.multiple_of(j * tk, tk)
            sc = sc_ref[j]
            eq = sc == thr_row
            eqf = jnp.where(eq, 1.0, 0.0)
            rank = seen + jnp.dot(lower, eqf.astype(BF16), preferred_element_type=F32)
            keep = ((sc > thr_row) | (eq & (rank <= need))) & ((start + krow) <= (q0 + qcol))
            sc_ref[j] = jnp.where(keep, 0.0, NEG_BIG)
            return seen + jnp.sum(eqf, axis=0, keepdims=True)

        lax.fori_loop(0, nch, body, jnp.zeros((1, tq), F32))

    m_ref[...] = jnp.full(m_ref.shape, NEG_BIG, F32)
    acc_ref[...] = jnp.zeros(acc_ref.shape, F32)
    krow_t = lax.broadcasted_iota(jnp.int32, (tk, LANES), 0)

    def attend(j, last):
        start = pl.multiple_of(j * tk, tk)
        kc = bk_ref[0, pl.ds(start, tk), :]
        vt = vt_ref[0, j]
        rel = (start - q0 + krow_t).astype(F32)

        def qk(h):
            return lax.dot_general(kc, bqs_ref[h * tq:(h + 1) * tq], _NT,
                                   preferred_element_type=F32)

        logits = {0: qk(0)}
        for h in range(heads):
            if h + 1 < heads:
                logits[h + 1] = qk(h + 1)
            logit = logits.pop(h)
            alibi = _tile_lanes((slopes[h] * LOG2E) * rel, QBLK // LANES)
            for c0 in range(0, tq, QBLK):
                nk = c0 + QBLK if last else tk
                s = logit[:nk, c0:c0 + QBLK] + sc_ref[j, :nk, c0:c0 + QBLK] + alibi[:nk]
                m_prev = m_ref[h, :, c0:c0 + QBLK]
                m_next = jnp.maximum(m_prev, jnp.max(s, axis=0, keepdims=True))
                alpha = jnp.exp2(m_prev - m_next)
                p = jnp.exp2(s - m_next).astype(BF16)
                acc_ref[h, :, c0:c0 + QBLK] = (alpha * acc_ref[h, :, c0:c0 + QBLK]
                                               + jnp.dot(vt[:, :nk], p, preferred_element_type=F32))
                m_ref[h, :, c0:c0 + QBLK] = m_next

    def attn_body(j, carry):
        attend(j, False)
        return carry

    lax.fori_loop(0, nch - 1, attn_body, 0)
    attend(nch - 1, True)

    for g in range(heads // 2):
        a = acc_ref[2 * g]
        b = acc_ref[2 * g + 1]
        pair = jnp.concatenate([a[:HEAD_DIM] / a[HEAD_DIM:HEAD_DIM + 1],
                                b[:HEAD_DIM] / b[HEAD_DIM:HEAD_DIM + 1]], axis=0)
        o_ref[0, :, g * LANES:(g + 1) * LANES] = pair.T.astype(o_ref.dtype)


def _dsa_attention(bq, iq, iwp, ik2, bk2, bv2):
    bsz, s, width = bq.shape
    heads = width // HEAD_DIM
    tq, tk = DSA_TQ, DSA_TK
    assert tq == tk
    topk = min(TOPK_MAX, s // 4)
    nck = s // tk
    kern = functools.partial(_dsa_kernel, tq=tq, tk=tk, heads=heads, topk=topk,
                             slopes=_alibi_slopes(heads))
    vt = bv2[:, :, :HEAD_DIM].reshape(bsz, nck, tk, HEAD_DIM).transpose(0, 1, 3, 2)
    vt = jnp.concatenate([vt, jnp.ones_like(vt)], axis=2)
    qspec = lambda w: pl.BlockSpec((1, tq, w), lambda b, i: (b, i, 0))
    kspec = pl.BlockSpec((1, s, LANES), lambda b, i: (b, 0, 0))
    return pl.pallas_call(
        kern,
        grid=(bsz, s // tq),
        in_specs=[qspec(width), qspec(width), qspec(LANES), kspec, kspec,
                  pl.BlockSpec((1, nck, LANES, tk), lambda b, i: (b, 0, 0, 0))],
        out_specs=pl.BlockSpec((1, tq, width), lambda b, i: (b, i, 0)),
        out_shape=jax.ShapeDtypeStruct((bsz, s, width), BF16),
        scratch_shapes=[pltpu.VMEM((nck, tk, tq), F32),
                        pltpu.VMEM((nck, tk, tq), BF16),
                        pltpu.VMEM((heads, tq, LANES), BF16),
                        pltpu.VMEM((heads * tq, LANES), BF16),
                        pltpu.VMEM((LANES, tq), F32),
                        pltpu.VMEM((heads, 1, tq), F32),
                        pltpu.VMEM((heads, LANES, tq), F32)],
        compiler_params=pltpu.CompilerParams(
            dimension_semantics=("arbitrary", "arbitrary"), vmem_limit_bytes=VMEM_LIMIT),
        name="dsa_attn",
    )(bq, iq, iwp, ik2, bk2, vt)


def _stack_heads(x, heads):
    lane = lax.broadcasted_iota(jnp.int32, (x.shape[0], LANES), 1)
    lo = lane < HEAD_DIM
    parts = []
    for h in range(heads):
        blk = x[:, (h // 2) * LANES:(h // 2 + 1) * LANES]
        keep = lo if h % 2 == 0 else jnp.logical_not(lo)
        parts.append(jnp.where(keep, blk, jnp.zeros_like(blk)))
    return jnp.concatenate(parts, axis=0)


def _swa_kernel(slope_ref, sink_ref, q_ref, kp_ref, kc_ref, vp_ref, vc_ref, o_ref, *, tq, group):
    kv = pl.program_id(1)
    i = pl.program_id(2)
    nk = 2 * WINDOW
    kk = jnp.concatenate([kp_ref[0], kc_ref[0]], axis=0)
    vv = jnp.concatenate([vp_ref[0], vc_ref[0]], axis=0)
    r = lax.broadcasted_iota(jnp.int32, (WINDOW, nk), 0)
    c = lax.broadcasted_iota(jnp.int32, (WINDOW, nk), 1)
    dist = WINDOW + r - c
    band = (dist >= 0) & (dist < WINDOW)
    distf = dist.astype(F32)
    lane = lax.broadcasted_iota(jnp.int32, (WINDOW, LANES), 1)
    biases = [jnp.where(band, -(slope_ref[kv * group + g] * LOG2E) * distf, NEG_BIG)
              for g in range(group)]
    for blk in range(tq // WINDOW):
        rows = slice(blk * WINDOW, (blk + 1) * WINDOW)
        qs = _stack_heads(q_ref[0, rows, :], group)
        kb = kk[blk * WINDOW:blk * WINDOW + nk]
        vb = vv[blk * WINDOW:blk * WINDOW + nk]
        logit = lax.dot_general(qs, kb, _NT, preferred_element_type=F32)
        outs = []
        for g in range(group):
            hq = kv * group + g
            s = logit[g * WINDOW:(g + 1) * WINDOW] + biases[g]
            if blk == 0:
                s = jnp.where(c >= jnp.where(i == 0, WINDOW, 0), s, NEG_BIG)
            sink = sink_ref[hq] * LOG2E
            m = jnp.maximum(jnp.max(s, axis=1, keepdims=True), sink)
            p = jnp.exp2(s - m)
            denom = jnp.sum(p, axis=1, keepdims=True) + jnp.exp2(sink - m)
            outs.append(jnp.dot(p.astype(BF16), vb, preferred_element_type=F32) / denom)
        for g in range(group // 2):
            o_ref[0, rows, g * LANES:(g + 1) * LANES] = jnp.where(
                lane < HEAD_DIM, outs[2 * g], outs[2 * g + 1]).astype(o_ref.dtype)


def _swa_attention(q, k2, v2, sinks):
    bsz, s, width = q.shape
    heads = width // HEAD_DIM
    kvh = k2.shape[2] // LANES
    group = heads // kvh
    tq = min(SWA_TQ, s)
    slopes = jnp.asarray(_alibi_slopes(heads), F32)
    r = tq // WINDOW
    prev = lambda b, kv, i: (b, jnp.maximum(i * r - 1, 0), kv)
    cur = lambda b, kv, i: (b, i, kv)
    smem = pl.BlockSpec(memory_space=pltpu.SMEM)
    return pl.pallas_call(
        functools.partial(_swa_kernel, tq=tq, group=group),
        grid=(bsz, kvh, s // tq),
        in_specs=[smem, smem,
                  pl.BlockSpec((1, tq, group * HEAD_DIM), cur),
                  pl.BlockSpec((1, WINDOW, LANES), prev),
                  pl.BlockSpec((1, tq, LANES), cur),
                  pl.BlockSpec((1, WINDOW, LANES), prev),
                  pl.BlockSpec((1, tq, LANES), cur)],
        out_specs=pl.BlockSpec((1, tq, group * HEAD_DIM), cur),
        out_shape=jax.ShapeDtypeStruct((bsz, s, width), BF16),
        compiler_params=pltpu.CompilerParams(
            dimension_semantics=("arbitrary", "arbitrary", "arbitrary"),
            vmem_limit_bytes=VMEM_LIMIT),
        name="swa_attn",
    )(slopes, sinks.astype(F32), q, k2, k2, v2, v2)


def _out_ffn_kernel(*refs, ny):
    x_ref = refs[0]
    y_refs = refs[1:1 + ny]
    wo_ref, g1_ref, n_ref, sc_ref, sh_ref, g2_ref, wg_ref, wu_ref, wd_ref, o_ref = refs[1 + ny:]
    tm = x_ref.shape[1]
    half = tm // 2
    for rows in (slice(0, half), slice(half, tm)):
        mix = None
        r0 = 0
        for y_ref in y_refs:
            w = y_ref.shape[2]
            part = jnp.dot(y_ref[0, rows, :], wo_ref[r0:r0 + w, :], preferred_element_type=F32)
            mix = part if mix is None else mix + part
            r0 += w
        x = x_ref[0, rows, :] + g1_ref[0] * mix
        h = _modulated_norm(x, n_ref[...], sc_ref[0], sh_ref[0]).astype(BF16)
        gate = jnp.dot(h, wg_ref[...], preferred_element_type=F32)
        up = jnp.dot(h, wu_ref[...], preferred_element_type=F32)
        act = (gate * jax.nn.sigmoid(gate) * up).astype(BF16)
        o_ref[0, rows, :] = x + g2_ref[0] * jnp.dot(act, wd_ref[...], preferred_element_type=F32)


def _out_ffn(x, ys, wo, g1, n, sc, sh, g2, wg, wu, wd):
    bsz, s, d = x.shape
    f = wg.shape[1]
    tm = FFN_TM
    const = lambda b, i: (0, 0)
    tok = lambda b, i: (b, i, 0)
    mod = lambda b, i: (b, 0, 0)
    once = pl.Buffered(1)
    return pl.pallas_call(
        functools.partial(_out_ffn_kernel, ny=len(ys)),
        grid=(bsz, s // tm),
        in_specs=[pl.BlockSpec((1, tm, d), tok)]
                 + [pl.BlockSpec((1, tm, y.shape[2]), tok) for y in ys]
                 + [pl.BlockSpec((d, d), const, pipeline_mode=once),
                  pl.BlockSpec((1, 1, d), mod),
                  pl.BlockSpec((1, d), const),
                  pl.BlockSpec((1, 1, d), mod),
                  pl.BlockSpec((1, 1, d), mod),
                  pl.BlockSpec((1, 1, d), mod),
                  pl.BlockSpec((d, f), const, pipeline_mode=once),
                  pl.BlockSpec((d, f), const, pipeline_mode=once),
                  pl.BlockSpec((f, d), const, pipeline_mode=once)],
        out_specs=pl.BlockSpec((1, tm, d), tok),
        out_shape=jax.ShapeDtypeStruct((bsz, s, d), F32),
        compiler_params=pltpu.CompilerParams(
            dimension_semantics=("arbitrary", "arbitrary"), vmem_limit_bytes=VMEM_LIMIT),
        name="out_ffn",
    )(x, *ys, wo, g1, n.reshape(1, d), sc, sh, g2, wg, wu, wd)


def _dup(w):
    d, c = w.shape
    w = w.reshape(d, c // HEAD_DIM, 1, HEAD_DIM)
    return jnp.broadcast_to(w, (d, c // HEAD_DIM, 2, HEAD_DIM)).reshape(d, 2 * c)


def _even_layout(w_in, qn_a, kn_a, qn_b, kn_b, d):
    a = d // 2
    sizes = (a, a, a, a, HEAD_DIM, HEAD_DIM, a, HEAD_DIM, a // HEAD_DIM)
    offs = [0]
    for sz in sizes:
        offs.append(offs[-1] + sz)
    aq, ak, av, bq, bk, bv, iq, ik, iw = [w_in[:, offs[n]:offs[n + 1]] for n in range(9)]
    nh = a // HEAD_DIM
    iw_pad = jnp.zeros((d, LANES), w_in.dtype).at[:, :nh].set(iw)
    w = jnp.concatenate([aq, ak, av, bq, iq, _dup(bk), _dup(bv), _dup(ik), iw_pad], axis=1)
    qscale = HEAD_DIM ** -0.5 * LOG2E
    ones = lambda n: jnp.ones((n,), F32)
    gain = jnp.concatenate([
        jnp.tile(qn_a, nh) * qscale, jnp.tile(kn_a, nh), ones(a),
        jnp.tile(qn_b, nh) * qscale, ones(a) * HEAD_DIM ** -0.5,
        jnp.tile(kn_b, 2), ones(LANES), ones(LANES), ones(LANES) * nh ** -0.5])
    groups = (
        (0, a, ((0, a, 0, 0, "norm"),)),
        (a, a, ((0, a, 1, 0, "norm"),)),
        (2 * a, a, ((0, a, 2, 0, "plain"),)),
        (3 * a, a, ((0, a, 3, 0, "norm"),)),
        (4 * a, a, ((0, a, 4, 0, "scale"),)),
        (5 * a, 4 * LANES, ((0, LANES, 5, 0, "norm"), (LANES, LANES, 6, 0, "plain"),
                            (2 * LANES, LANES, 7, 0, "plain"), (3 * LANES, LANES, 8, 0, "scale"))),
    )
    out_defs = ((a, BF16), (a, BF16), (a, BF16), (a, BF16), (a, BF16),
                (LANES, BF16), (LANES, BF16), (LANES, BF16), (LANES, F32))
    return w.astype(BF16), gain.reshape(1, -1), groups, out_defs


def _odd_layout(w_in, qn_c, kn_c, d):
    kvw = d // 4
    q, k, v = w_in[:, :d], w_in[:, d:d + kvw], w_in[:, d + kvw:]
    w = jnp.concatenate([q, _dup(k), _dup(v)], axis=1)
    nh = d // HEAD_DIM
    gain = jnp.concatenate([jnp.tile(qn_c, nh) * (HEAD_DIM ** -0.5 * LOG2E),
                            jnp.tile(kn_c, 2 * kvw // HEAD_DIM), jnp.ones((2 * kvw,), F32)])
    half = d // 2
    groups = (
        (0, half, ((0, half, 0, 0, "norm"),)),
        (half, half, ((0, half, 0, half, "norm"),)),
        (d, 2 * kvw, ((0, 2 * kvw, 1, 0, "norm"),)),
        (d + 2 * kvw, 2 * kvw, ((0, 2 * kvw, 2, 0, "plain"),)),
    )
    out_defs = ((d, BF16), (2 * kvw, BF16), (2 * kvw, BF16))
    return w.astype(BF16), gain.reshape(1, -1), groups, out_defs


def kernel(x, c, ada_w, ada_b, norm_mix, norm_ffn, w_out, ffn_gate, ffn_up, ffn_down,
           w_in_even, qn_a, kn_a, lam_q1, lam_k1, lam_q2, lam_k2, subln_a, qn_b, kn_b,
           w_in_odd, qn_c, kn_c, sinks_c):
    depth, d = norm_mix.shape
    mod = _ada_mod(c, ada_w, ada_b)
    for l in range(depth):
        sh1, sc1, g1, sh2, sc2, g2 = [mod[l, :, None, n * d:(n + 1) * d] for n in range(6)]
        if l % 2 == 0:
            e = l // 2
            w, gain, groups, out_defs = _even_layout(w_in_even[e], qn_a[e], kn_a[e],
                                                     qn_b[e], kn_b[e], d)
            aq, ak, av, bq, iq, bk2, bv2, ik2, iwp = _proj(
                x, norm_mix[l], sc1, sh1, w, gain, groups, out_defs)
            lam_vecs = jnp.stack([lam_q1[e], lam_k1[e], lam_q2[e], lam_k2[e]]).astype(F32)
            ya = _diff_attention(aq, ak, av, lam_vecs, subln_a[e], l)
            yb = _dsa_attention(bq, iq, iwp, ik2, bk2, bv2)
            ys = (ya, yb)
        else:
            o = l // 2
            w, gain, groups, out_defs = _odd_layout(w_in_odd[o], qn_c[o], kn_c[o], d)
            q, k2, v2 = _proj(x, norm_mix[l], sc1, sh1, w, gain, groups, out_defs)
            ys = (_swa_attention(q, k2, v2, sinks_c[o]),)
        x = _out_ffn(x, ys, w_out[l].astype(BF16), g1, norm_ffn[l], sc2, sh2, g2,
                     ffn_gate[l].astype(BF16), ffn_up[l].astype(BF16),
                     ffn_down[l].astype(BF16))
    return x
```

```python
import functools
import math

import jax
import jax.numpy as jnp
from jax import lax
from jax.experimental import pallas as pl
from jax.experimental.pallas import tpu as pltpu

HEAD_DIM = 64
NORM_EPS = 1e-6
TOPK_MAX = 256
WINDOW = 128

LANES = 128
SUB = 8
PACK = 16
VMEM_LIMIT = 56 * 1024 * 1024

PROJ_TM = 512
FFN_TM = 512
DIFF_T = 512
DIFF_KW = 512
DSA_TQ = 512
DSA_TK = 512
SWA_TQ = 1024
ROWS = 128
QBLK = 256
NACC = 4

NEG_BIG = -1e30
LOG2E = 1.4426950408889634
KEY_NEG_INF = -2139095041
KEY16_NEG_INF = -32641

F32 = jnp.float32
BF16 = jnp.bfloat16
_NT = (((1,), (1,)), ((), ()))


def _alibi_slopes(n):
    return [2.0 ** (-8.0 * (i + 1) / n) for i in range(n)]


def _tile_lanes(x, n):
    return x if n == 1 else jnp.concatenate([x] * n, axis=1)


def _ada_kernel(c_ref, w_ref, b_ref, o_ref):
    c = c_ref[...]
    cond = c * jax.nn.sigmoid(c)
    o_ref[0] = jnp.dot(cond, w_ref[0], preferred_element_type=F32,
                       precision=lax.Precision.HIGHEST) + b_ref[0]


def _ada_mod(c, ada_w, ada_b):
    depth, d, n = ada_w.shape
    b = c.shape[0]
    tn = 1536
    c_pad = jnp.zeros((SUB, d), F32).at[:b].set(c)
    out = pl.pallas_call(
        _ada_kernel,
        grid=(depth, n // tn),
        in_specs=[pl.BlockSpec((SUB, d), lambda l, j: (0, 0)),
                  pl.BlockSpec((1, d, tn), lambda l, j: (l, 0, j)),
                  pl.BlockSpec((1, 1, tn), lambda l, j: (l, 0, j))],
        out_specs=pl.BlockSpec((1, SUB, tn), lambda l, j: (l, 0, j)),
        out_shape=jax.ShapeDtypeStruct((depth, SUB, n), F32),
        compiler_params=pltpu.CompilerParams(
            dimension_semantics=("arbitrary", "arbitrary"), vmem_limit_bytes=VMEM_LIMIT),
        name="ada_mod",
    )(c_pad, ada_w, ada_b.reshape(depth, 1, n))
    return out[:, :b]


def _modulated_norm(x, g, sc, sh):
    ms = jnp.mean(x * x, axis=-1, keepdims=True)
    return (x * lax.rsqrt(ms + NORM_EPS) * g) * (1.0 + sc) + sh


def _proj_kernel(x_ref, g_ref, sc_ref, sh_ref, w_ref, bd_ref, gain_ref, *out_refs, groups):
    tm = x_ref.shape[1]
    half = tm // 2
    bd = bd_ref[...]
    nb = bd.shape[0]
    for rows in (slice(0, half), slice(half, tm)):
        h = _modulated_norm(x_ref[0, rows, :], g_ref[...], sc_ref[0], sh_ref[0]).astype(BF16)
        for (c0, width, parts) in groups:
            y = jnp.dot(h, w_ref[:, c0:c0 + width], preferred_element_type=F32)
            for (p0, pw, oi, o0, mode) in parts:
                step = min(nb, pw)
                for s in range(0, pw, step):
                    ys = y[:, p0 + s:p0 + s + step]
                    col = c0 + p0 + s
                    if mode == "norm":
                        ss = jnp.dot((ys * ys).astype(BF16), bd[:step, :step],
                                     preferred_element_type=F32)
                        ys = ys * lax.rsqrt(ss * (1.0 / HEAD_DIM) + NORM_EPS)
                    if mode != "plain":
                        ys = ys * gain_ref[:, col:col + step]
                    out_refs[oi][0, rows, o0 + s:o0 + s + step] = ys.astype(out_refs[oi].dtype)


def _proj(x, g, sc, sh, w, gain, groups, out_defs):
    bsz, s, d = x.shape
    c = w.shape[1]
    tm = PROJ_TM
    nb = 2 * LANES
    r = lax.broadcasted_iota(jnp.int32, (nb, nb), 0) // HEAD_DIM
    cc = lax.broadcasted_iota(jnp.int32, (nb, nb), 1) // HEAD_DIM
    bd = (r == cc).astype(BF16)
    const = lambda b, i: (0, 0)
    return pl.pallas_call(
        functools.partial(_proj_kernel, groups=groups),
        grid=(bsz, s // tm),
        in_specs=[pl.BlockSpec((1, tm, d), lambda b, i: (b, i, 0)),
                  pl.BlockSpec((1, d), const),
                  pl.BlockSpec((1, 1, d), lambda b, i: (b, 0, 0)),
                  pl.BlockSpec((1, 1, d), lambda b, i: (b, 0, 0)),
                  pl.BlockSpec((d, c), const),
                  pl.BlockSpec((nb, nb), const),
                  pl.BlockSpec((1, c), const)],
        out_specs=[pl.BlockSpec((1, tm, wd), lambda b, i: (b, i, 0)) for wd, _ in out_defs],
        out_shape=[jax.ShapeDtypeStruct((bsz, s, wd), dt) for wd, dt in out_defs],
        compiler_params=pltpu.CompilerParams(
            dimension_semantics=("arbitrary", "arbitrary"), vmem_limit_bytes=VMEM_LIMIT),
        name="in_proj",
    )(x, g.reshape(1, d), sc, sh, w, bd, gain)


def _diff_attn_kernel(q_ref, k_ref, v_ref, lam_ref, subln_ref, o_ref,
                      qs_ref, m_ref, acc_ref, *, t, kw, heads, slopes, lam_init):
    i = pl.program_id(1)
    q0 = i * t
    lane = lax.broadcasted_iota(jnp.int32, (t, LANES), 1)
    for h in range(heads):
        q = q_ref[0, :, h * LANES:(h + 1) * LANES]
        zero = jnp.zeros_like(q)
        qs_ref[h, :t] = jnp.where(lane < HEAD_DIM, q, zero)
        qs_ref[h, t:] = jnp.where(lane >= HEAD_DIM, q, zero)
    m_ref[...] = jnp.full(m_ref.shape, NEG_BIG, F32)
    acc_ref[...] = jnp.zeros(acc_ref.shape, F32)

    def step(start, width, masked):
        col = lax.broadcasted_iota(jnp.int32, (1, width), 1)
        rel = (start - q0 + col).astype(F32)
        nsl = width // LANES
        logits, values = [], []
        for h in range(heads):
            kc = k_ref[0, pl.ds(start, width), h * LANES:(h + 1) * LANES]
            vc = v_ref[0, pl.ds(start, width), h * LANES:(h + 1) * LANES]
            logits.append(lax.dot_general(qs_ref[h], kc, _NT, preferred_element_type=F32))
            values.append(jnp.concatenate([vc, jnp.ones_like(vc)], axis=1))
        for h in range(heads):
            alibi = (slopes[h] * LOG2E) * rel
            for r0 in range(0, 2 * t, ROWS):
                s = logits[h][r0:r0 + ROWS] + alibi
                if masked:
                    r = lax.broadcasted_iota(jnp.int32, (ROWS, width), 0) + (r0 % t)
                    c = lax.broadcasted_iota(jnp.int32, (ROWS, width), 1)
                    s = jnp.where(r >= c, s, NEG_BIG)
                m_prev = m_ref[h, r0:r0 + ROWS]
                m_next = jnp.maximum(m_prev, jnp.max(s, axis=1, keepdims=True))
                alpha = jnp.exp2(m_prev - m_next)
                p = jnp.exp2(s - _tile_lanes(m_next, nsl))
                acc_ref[h, r0:r0 + ROWS] = (
                    _tile_lanes(alpha, 2) * acc_ref[h, r0:r0 + ROWS]
                    + jnp.dot(p.astype(BF16), values[h], preferred_element_type=F32))
                m_ref[h, r0:r0 + ROWS] = m_next

    def body(j, carry):
        step(pl.multiple_of(j * kw, kw), kw, False)
        return carry

    nfull = q0 // kw
    lax.fori_loop(0, nfull, body, 0)

    if t < kw:
        @pl.when(q0 - nfull * kw > 0)
        def _():
            step(pl.multiple_of(q0 - t, t), t, False)

    step(pl.multiple_of(q0, t), t, True)

    lam_v = lam_ref[...]
    s1 = jnp.sum(lam_v[0:1] * lam_v[1:2], axis=-1, keepdims=True)
    s2 = jnp.sum(lam_v[2:3] * lam_v[3:4], axis=-1, keepdims=True)
    lam = jnp.exp(s1) - jnp.exp(s2) + lam_init
    for h in range(heads):
        acc = acc_ref[h]
        o = acc[:, :LANES] / acc[:, LANES:]
        y = o[:t] - lam * o[t:]
        ms = jnp.mean(y * y, axis=-1, keepdims=True)
        y = (y * lax.rsqrt(ms + NORM_EPS) * subln_ref[...]) * (1.0 - lam_init)
        o_ref[0, :, h * LANES:(h + 1) * LANES] = y.astype(o_ref.dtype)


def _diff_attention(q, k, v, lam_vecs, subln, layer_idx):
    bsz, s, width = q.shape
    heads = width // LANES
    t, kw = DIFF_T, DIFF_KW
    assert kw in (t, 2 * t)
    lam_init = 0.8 - 0.6 * math.exp(-0.3 * layer_idx)
    kern = functools.partial(_diff_attn_kernel, t=t, kw=kw, heads=heads,
                             slopes=_alibi_slopes(heads), lam_init=lam_init)
    return pl.pallas_call(
        kern,
        grid=(bsz, s // t),
        in_specs=[pl.BlockSpec((1, t, width), lambda b, i: (b, i, 0)),
                  pl.BlockSpec((1, s, width), lambda b, i: (b, 0, 0)),
                  pl.BlockSpec((1, s, width), lambda b, i: (b, 0, 0)),
                  pl.BlockSpec((4, HEAD_DIM), lambda b, i: (0, 0)),
                  pl.BlockSpec((1, LANES), lambda b, i: (0, 0))],
        out_specs=pl.BlockSpec((1, t, width), lambda b, i: (b, i, 0)),
        out_shape=jax.ShapeDtypeStruct((bsz, s, width), BF16),
        scratch_shapes=[pltpu.VMEM((heads, 2 * t, LANES), BF16),
                        pltpu.VMEM((heads, 2 * t, LANES), F32),
                        pltpu.VMEM((heads, 2 * t, 2 * LANES), F32)],
        compiler_params=pltpu.CompilerParams(
            dimension_semantics=("arbitrary", "arbitrary"), vmem_limit_bytes=VMEM_LIMIT),
        name="diff_attn",
    )(q, k, v, lam_vecs, subln.reshape(1, LANES))


def _key_to_float(k):
    return lax.bitcast_convert_type(k ^ ((k >> 31) & jnp.int32(0x7FFFFFFF)), F32)


def _floor_bf16(x):
    r = x.astype(BF16).astype(F32)
    ulp_down = jnp.where(x < 0, jnp.int32(0x10000), jnp.int32(-0x10000))
    down = lax.bitcast_convert_type(lax.bitcast_convert_type(r, jnp.int32) + ulp_down, F32)
    return jnp.where(r > x, down, r).astype(BF16)


def _dsa_kernel(bq_ref, iq_ref, iw_ref, ik_ref, bk_ref, vt_ref, o_ref,
                sc_ref, scb_ref, iqs_ref, bqs_ref, wt_ref, need_ref, m_ref, acc_ref,
                *, tq, tk, heads, topk, slopes):
    i = pl.program_id(1)
    q0 = i * tq
    nch = q0 // tk + 1
    krow = lax.broadcasted_iota(jnp.int32, (tk, tq), 0)
    qcol = lax.broadcasted_iota(jnp.int32, (tk, tq), 1)

    lane = lax.broadcasted_iota(jnp.int32, (tq, LANES), 1)
    for h in range(heads):
        keep = (lane < HEAD_DIM) if h % 2 == 0 else (lane >= HEAD_DIM)
        tile = slice((h // 2) * LANES, (h // 2 + 1) * LANES)
        iqb = iq_ref[0, :, tile]
        bqb = bq_ref[0, :, tile]
        iqs_ref[h] = jnp.where(keep, iqb, jnp.zeros_like(iqb))
        bqs_ref[h * tq:(h + 1) * tq] = jnp.where(keep, bqb, jnp.zeros_like(bqb))
    wt_ref[...] = iw_ref[0].T

    def scores(j, r0, nrows, c_lo, band):
        start = pl.multiple_of(j * tk, tk)
        ikc = ik_ref[0, pl.ds(start + r0, nrows), :]
        sc = None
        for h in range(heads):
            logit = lax.dot_general(ikc, iqs_ref[h, c_lo:, :], _NT,
                                    preferred_element_type=F32)
            term = jnp.maximum(logit, 0.0) * wt_ref[h:h + 1, c_lo:]
            sc = term if sc is None else sc + term
        sc = sc + 0.0
        if band:
            kr = lax.broadcasted_iota(jnp.int32, sc.shape, 0) + (r0 - c_lo)
            qc = lax.broadcasted_iota(jnp.int32, sc.shape, 1)
            sc = jnp.where(kr <= qc, sc, -jnp.inf)
            if c_lo:
                sc = jnp.concatenate([jnp.full((nrows, c_lo), -jnp.inf, F32), sc], axis=1)
        sc_ref[j, r0:r0 + nrows, :] = sc
        scb_ref[j, r0:r0 + nrows, :] = _floor_bf16(sc)

    def score_body(j, carry):
        scores(j, 0, tk, 0, False)
        return carry

    lax.fori_loop(0, nch - 1, score_body, 0)
    for r0 in range(0, tk, QBLK):
        scores(nch - 1, r0, QBLK, r0, True)

    def count_in(ref, rows, pred_fn, thr):
        dt = ref.dtype

        def add_rows(j, accs, r_lo, r_hi, c_lo):
            accs = list(accs)
            for r in range(r_lo // rows, r_hi // rows):
                blk = ref[j, r * rows:(r + 1) * rows, c_lo:]
                hit = jnp.where(pred_fn(blk, thr[:, c_lo:]), jnp.ones(blk.shape, dt),
                                jnp.zeros(blk.shape, dt))
                a = accs[r % NACC]
                accs[r % NACC] = (a + hit if c_lo == 0 else
                                  jnp.concatenate([a[:, :c_lo], a[:, c_lo:] + hit], axis=1))
            return tuple(accs)

        accs = lax.fori_loop(0, nch - 1, lambda j, accs: add_rows(j, accs, 0, tk, 0),
                             tuple(jnp.zeros((rows, tq), dt) for _ in range(NACC)))
        for r0 in range(0, tk, LANES):
            accs = add_rows(nch - 1, accs, r0, r0 + LANES, r0)
        cnt = accs[0].astype(F32)
        for a in accs[1:]:
            cnt = cnt + a.astype(F32)
        return jnp.broadcast_to(jnp.sum(cnt, axis=0, keepdims=True), (SUB, tq))

    def count(pred_fn, thr):
        return count_in(sc_ref, SUB, pred_fn, thr)

    ge = lambda a, b: a >= b
    gt = lambda a, b: a > b

    def count_b(thr_b):
        return count_in(scb_ref, PACK, ge, thr_b)

    def key16_to_bits(k):
        return lax.shift_left(k ^ ((k >> 31) & jnp.int32(0x7FFF)), 16)

    def hi_body(b, carry):
        key, n_ge = carry
        cand = key + lax.shift_left(jnp.int32(1), 15 - b)
        cf = lax.bitcast_convert_type(key16_to_bits(cand), F32)
        cnt = count_b(jnp.concatenate([cf, cf], axis=0).astype(BF16))
        ok = cnt >= float(topk)
        return jnp.where(ok, cand, key), jnp.where(ok, cnt, n_ge)

    key16, n_ge = lax.fori_loop(0, 16, hi_body, (jnp.full((SUB, tq), -2 ** 15, jnp.int32),
                                                 jnp.zeros((SUB, tq), F32)))
    bits = key16_to_bits(key16)
    base = bits ^ ((bits >> 31) & jnp.int32(0x7FFFFFFF))
    base = jnp.where(key16 <= KEY16_NEG_INF, KEY_NEG_INF, base)

    def lo_body(b, carry):
        off, n_ge = carry
        step = lax.shift_left(jnp.int32(1), 15 - b)
        cnt = count(ge, _key_to_float(base + off + step))
        ok = cnt >= float(topk)
        return jnp.where(ok, off + step, off), jnp.where(ok, cnt, n_ge)

    off, n_ge = lax.fori_loop(0, 16, lo_body, (jnp.zeros((SUB, tq), jnp.int32), n_ge))
    key = base + off
    thr = jnp.where(key <= KEY_NEG_INF, -jnp.inf, _key_to_float(key))
    excess = jnp.where(n_ge > float(topk), 1.0, 0.0)
    thr_row = thr[0:1]

    @pl.when(jnp.max(excess) > 0.0)
    def _():
        need_ref[...] = float(topk) - count(gt, thr)

    krow_g = lax.broadcasted_iota(jnp.int32, (tk, LANES), 0)
    qcol_g = lax.broadcasted_iota(jnp.int32, (tk, LANES), 1)
    for c0 in range(0, tq, LANES):
        ls = slice(c0, c0 + LANES)
        thr_c = thr[0:1, ls]
        group_ties = jnp.max(excess[:, ls]) > 0.0
        causal_c = ((nch - 1) * tk + krow_g) <= (q0 + c0 + qcol_g)

        @pl.when(jnp.logical_not(group_ties))
        def _():
            def body(j, carry):
                sc_ref[j, :, ls] = jnp.where(sc_ref[j, :, ls] >= thr_c, 0.0, NEG_BIG)
                return carry
            lax.fori_loop(0, nch - 1, body, 0)
            last = nch - 1
            keep = (sc_ref[last, :, ls] >= thr_c) & causal_c
            sc_ref[last, :, ls] = jnp.where(keep, 0.0, NEG_BIG)

        @pl.when(group_ties)
        def _():
            need = need_ref[0:1, ls]
            r2 = lax.broadcasted_iota(jnp.int32, (tk, tk), 0)
            c2 = lax.broadcasted_iota(jnp.int32, (tk, tk), 1)
            lower = jnp.where(c2 <= r2, 1.0, 0.0).astype(BF16)

            def body(j, seen):
                sc = sc_ref[j, :, ls]
                eq = sc == thr_c
                eqf = jnp.where(eq, 1.0, 0.0)
                rank = seen + jnp.dot(lower, eqf.astype(BF16), preferred_element_type=F32)
                keep = (sc > thr_c) | (eq & (rank <= need))
                keep = keep & ((j * tk + krow_g) <= (q0 + c0 + qcol_g))
                sc_ref[j, :, ls] = jnp.where(keep, 0.0, NEG_BIG)
                return seen + jnp.sum(eqf, axis=0, keepdims=True)

            lax.fori_loop(0, nch, body, jnp.zeros((1, LANES), F32))

    m_ref[...] = jnp.full(m_ref.shape, NEG_BIG, F32)
    acc_ref[...] = jnp.zeros(acc_ref.shape, F32)
    krow_t = lax.broadcasted_iota(jnp.int32, (tk, LANES), 0)

    def attend(j, last):
        start = pl.multiple_of(j * tk, tk)
        kc = bk_ref[0, pl.ds(start, tk), :]
        vt = vt_ref[0, j]
        rel = (start - q0 + krow_t).astype(F32)

        def qk(h):
            return lax.dot_general(kc, bqs_ref[h * tq:(h + 1) * tq], _NT,
                                   preferred_element_type=F32)

        logits = {0: qk(0)}
        for h in range(heads):
            if h + 1 < heads:
                logits[h + 1] = qk(h + 1)
            logit = logits.pop(h)
            alibi = _tile_lanes((slopes[h] * LOG2E) * rel, QBLK // LANES)
            for c0 in range(0, tq, QBLK):
                nk = c0 + QBLK if last else tk
                s = logit[:nk, c0:c0 + QBLK] + sc_ref[j, :nk, c0:c0 + QBLK] + alibi[:nk]
                m_prev = m_ref[h, :, c0:c0 + QBLK]
                m_next = jnp.maximum(m_prev, jnp.max(s, axis=0, keepdims=True))
                alpha = jnp.exp2(m_prev - m_next)
                p = jnp.exp2(s - m_next).astype(BF16)
                acc_ref[h, :, c0:c0 + QBLK] = (alpha * acc_ref[h, :, c0:c0 + QBLK]
                                               + jnp.dot(vt[:, :nk], p, preferred_element_type=F32))
                m_ref[h, :, c0:c0 + QBLK] = m_next

    def attn_body(j, carry):
        attend(j, False)
        return carry

    lax.fori_loop(0, nch - 1, attn_body, 0)
    attend(nch - 1, True)

    for g in range(heads // 2):
        a = acc_ref[2 * g]
        b = acc_ref[2 * g + 1]
        pair = jnp.concatenate([a[:HEAD_DIM] / a[HEAD_DIM:HEAD_DIM + 1],
                                b[:HEAD_DIM] / b[HEAD_DIM:HEAD_DIM + 1]], axis=0)
        o_ref[0, :, g * LANES:(g + 1) * LANES] = pair.T.astype(o_ref.dtype)


def _dsa_attention(bq, iq, iwp, ik2, bk2, bv2):
    bsz, s, width = bq.shape
    heads = width // HEAD_DIM
    tq, tk = DSA_TQ, DSA_TK
    assert tq == tk
    topk = min(TOPK_MAX, s // 4)
    nck = s // tk
    kern = functools.partial(_dsa_kernel, tq=tq, tk=tk, heads=heads, topk=topk,
                             slopes=_alibi_slopes(heads))
    vt = bv2[:, :, :HEAD_DIM].reshape(bsz, nck, tk, HEAD_DIM).transpose(0, 1, 3, 2)
    vt = jnp.concatenate([vt, jnp.ones_like(vt)], axis=2)
    qspec = lambda w: pl.BlockSpec((1, tq, w), lambda b, i: (b, i, 0))
    kspec = pl.BlockSpec((1, s, LANES), lambda b, i: (b, 0, 0))
    return pl.pallas_call(
        kern,
        grid=(bsz, s // tq),
        in_specs=[qspec(width), qspec(width), qspec(LANES), kspec, kspec,
                  pl.BlockSpec((1, nck, LANES, tk), lambda b, i: (b, 0, 0, 0))],
        out_specs=pl.BlockSpec((1, tq, width), lambda b, i: (b, i, 0)),
        out_shape=jax.ShapeDtypeStruct((bsz, s, width), BF16),
        scratch_shapes=[pltpu.VMEM((nck, tk, tq), F32),
                        pltpu.VMEM((nck, tk, tq), BF16),
                        pltpu.VMEM((heads, tq, LANES), BF16),
                        pltpu.VMEM((heads * tq, LANES), BF16),
                        pltpu.VMEM((LANES, tq), F32),
                        pltpu.VMEM((SUB, tq), F32),
                        pltpu.VMEM((heads, 1, tq), F32),
                        pltpu.VMEM((heads, LANES, tq), F32)],
        compiler_params=pltpu.CompilerParams(
            dimension_semantics=("arbitrary", "arbitrary"), vmem_limit_bytes=VMEM_LIMIT),
        name="dsa_attn",
    )(bq, iq, iwp, ik2, bk2, vt)


def _stack_heads(x, heads):
    lane = lax.broadcasted_iota(jnp.int32, (x.shape[0], LANES), 1)
    lo = lane < HEAD_DIM
    parts = []
    for h in range(heads):
        blk = x[:, (h // 2) * LANES:(h // 2 + 1) * LANES]
        keep = lo if h % 2 == 0 else jnp.logical_not(lo)
        parts.append(jnp.where(keep, blk, jnp.zeros_like(blk)))
    return jnp.concatenate(parts, axis=0)


def _swa_kernel(slope_ref, sink_ref, q_ref, kp_ref, kc_ref, vp_ref, vc_ref, o_ref, *, tq, group):
    kv = pl.program_id(1)
    i = pl.program_id(2)
    nk = 2 * WINDOW
    kk = jnp.concatenate([kp_ref[0], kc_ref[0]], axis=0)
    vv = jnp.concatenate([vp_ref[0], vc_ref[0]], axis=0)
    r = lax.broadcasted_iota(jnp.int32, (WINDOW, nk), 0)
    c = lax.broadcasted_iota(jnp.int32, (WINDOW, nk), 1)
    dist = WINDOW + r - c
    band = (dist >= 0) & (dist < WINDOW)
    distf = dist.astype(F32)
    lane = lax.broadcasted_iota(jnp.int32, (WINDOW, LANES), 1)
    biases = [jnp.where(band, -(slope_ref[kv * group + g] * LOG2E) * distf, NEG_BIG)
              for g in range(group)]
    for blk in range(tq // WINDOW):
        rows = slice(blk * WINDOW, (blk + 1) * WINDOW)
        qs = _stack_heads(q_ref[0, rows, :], group)
        kb = kk[blk * WINDOW:blk * WINDOW + nk]
        vb = vv[blk * WINDOW:blk * WINDOW + nk]
        logit = lax.dot_general(qs, kb, _NT, preferred_element_type=F32)
        outs = []
        for g in range(group):
            hq = kv * group + g
            s = logit[g * WINDOW:(g + 1) * WINDOW] + biases[g]
            if blk == 0:
                s = jnp.where(c >= jnp.where(i == 0, WINDOW, 0), s, NEG_BIG)
            sink = sink_ref[hq] * LOG2E
            m = jnp.maximum(jnp.max(s, axis=1, keepdims=True), sink)
            p = jnp.exp2(s - m)
            denom = jnp.sum(p, axis=1, keepdims=True) + jnp.exp2(sink - m)
            outs.append(jnp.dot(p.astype(BF16), vb, preferred_element_type=F32) / denom)
        for g in range(group // 2):
            o_ref[0, rows, g * LANES:(g + 1) * LANES] = jnp.where(
                lane < HEAD_DIM, outs[2 * g], outs[2 * g + 1]).astype(o_ref.dtype)


def _swa_attention(q, k2, v2, sinks):
    bsz, s, width = q.shape
    heads = width // HEAD_DIM
    kvh = k2.shape[2] // LANES
    group = heads // kvh
    tq = min(SWA_TQ, s)
    slopes = jnp.asarray(_alibi_slopes(heads), F32)
    r = tq // WINDOW
    prev = lambda b, kv, i: (b, jnp.maximum(i * r - 1, 0), kv)
    cur = lambda b, kv, i: (b, i, kv)
    smem = pl.BlockSpec(memory_space=pltpu.SMEM)
    return pl.pallas_call(
        functools.partial(_swa_kernel, tq=tq, group=group),
        grid=(bsz, kvh, s // tq),
        in_specs=[smem, smem,
                  pl.BlockSpec((1, tq, group * HEAD_DIM), cur),
                  pl.BlockSpec((1, WINDOW, LANES), prev),
                  pl.BlockSpec((1, tq, LANES), cur),
                  pl.BlockSpec((1, WINDOW, LANES), prev),
                  pl.BlockSpec((1, tq, LANES), cur)],
        out_specs=pl.BlockSpec((1, tq, group * HEAD_DIM), cur),
        out_shape=jax.ShapeDtypeStruct((bsz, s, width), BF16),
        compiler_params=pltpu.CompilerParams(
            dimension_semantics=("arbitrary", "arbitrary", "arbitrary"),
            vmem_limit_bytes=VMEM_LIMIT),
        name="swa_attn",
    )(slopes, sinks.astype(F32), q, k2, k2, v2, v2)


def _out_ffn_kernel(*refs, ny):
    x_ref = refs[0]
    y_refs = refs[1:1 + ny]
    wo_ref, g1_ref, n_ref, sc_ref, sh_ref, g2_ref, wg_ref, wu_ref, wd_ref, o_ref = refs[1 + ny:]
    tm = x_ref.shape[1]
    half = tm // 2
    for rows in (slice(0, half), slice(half, tm)):
        mix = None
        r0 = 0
        for y_ref in y_refs:
            w = y_ref.shape[2]
            part = jnp.dot(y_ref[0, rows, :], wo_ref[r0:r0 + w, :], preferred_element_type=F32)
            mix = part if mix is None else mix + part
            r0 += w
        x = x_ref[0, rows, :] + g1_ref[0] * mix
        h = _modulated_norm(x, n_ref[...], sc_ref[0], sh_ref[0]).astype(BF16)
        gate = jnp.dot(h, wg_ref[...], preferred_element_type=F32)
        up = jnp.dot(h, wu_ref[...], preferred_element_type=F32)
        act = (gate * jax.nn.sigmoid(gate) * up).astype(BF16)
        o_ref[0, rows, :] = x + g2_ref[0] * jnp.dot(act, wd_ref[...], preferred_element_type=F32)


def _out_ffn(x, ys, wo, g1, n, sc, sh, g2, wg, wu, wd):
    bsz, s, d = x.shape
    f = wg.shape[1]
    tm = FFN_TM
    const = lambda b, i: (0, 0)
    tok = lambda b, i: (b, i, 0)
    mod = lambda b, i: (b, 0, 0)
    once = pl.Buffered(1)
    return pl.pallas_call(
        functools.partial(_out_ffn_kernel, ny=len(ys)),
        grid=(bsz, s // tm),
        in_specs=[pl.BlockSpec((1, tm, d), tok)]
                 + [pl.BlockSpec((1, tm, y.shape[2]), tok) for y in ys]
                 + [pl.BlockSpec((d, d), const, pipeline_mode=once),
                  pl.BlockSpec((1, 1, d), mod),
                  pl.BlockSpec((1, d), const),
                  pl.BlockSpec((1, 1, d), mod),
                  pl.BlockSpec((1, 1, d), mod),
                  pl.BlockSpec((1, 1, d), mod),
                  pl.BlockSpec((d, f), const, pipeline_mode=once),
                  pl.BlockSpec((d, f), const, pipeline_mode=once),
                  pl.BlockSpec((f, d), const, pipeline_mode=once)],
        out_specs=pl.BlockSpec((1, tm, d), tok),
        out_shape=jax.ShapeDtypeStruct((bsz, s, d), F32),
        compiler_params=pltpu.CompilerParams(
            dimension_semantics=("arbitrary", "arbitrary"), vmem_limit_bytes=VMEM_LIMIT),
        name="out_ffn",
    )(x, *ys, wo, g1, n.reshape(1, d), sc, sh, g2, wg, wu, wd)


def _dup(w):
    d, c = w.shape
    w = w.reshape(d, c // HEAD_DIM, 1, HEAD_DIM)
    return jnp.broadcast_to(w, (d, c // HEAD_DIM, 2, HEAD_DIM)).reshape(d, 2 * c)


def _even_layout(w_in, qn_a, kn_a, qn_b, kn_b, d):
    a = d // 2
    sizes = (a, a, a, a, HEAD_DIM, HEAD_DIM, a, HEAD_DIM, a // HEAD_DIM)
    offs = [0]
    for sz in sizes:
        offs.append(offs[-1] + sz)
    aq, ak, av, bq, bk, bv, iq, ik, iw = [w_in[:, offs[n]:offs[n + 1]] for n in range(9)]
    nh = a // HEAD_DIM
    iw_pad = jnp.zeros((d, LANES), w_in.dtype).at[:, :nh].set(iw)
    w = jnp.concatenate([aq, ak, av, bq, iq, _dup(bk), _dup(bv), _dup(ik), iw_pad], axis=1)
    qscale = HEAD_DIM ** -0.5 * LOG2E
    ones = lambda n: jnp.ones((n,), F32)
    gain = jnp.concatenate([
        jnp.tile(qn_a, nh) * qscale, jnp.tile(kn_a, nh), ones(a),
        jnp.tile(qn_b, nh) * qscale, ones(a) * HEAD_DIM ** -0.5,
        jnp.tile(kn_b, 2), ones(LANES), ones(LANES), ones(LANES) * nh ** -0.5])
    groups = (
        (0, a, ((0, a, 0, 0, "norm"),)),
        (a, a, ((0, a, 1, 0, "norm"),)),
        (2 * a, a, ((0, a, 2, 0, "plain"),)),
        (3 * a, a, ((0, a, 3, 0, "norm"),)),
        (4 * a, a, ((0, a, 4, 0, "scale"),)),
        (5 * a, 4 * LANES, ((0, LANES, 5, 0, "norm"), (LANES, LANES, 6, 0, "plain"),
                            (2 * LANES, LANES, 7, 0, "plain"), (3 * LANES, LANES, 8, 0, "scale"))),
    )
    out_defs = ((a, BF16), (a, BF16), (a, BF16), (a, BF16), (a, BF16),
                (LANES, BF16), (LANES, BF16), (LANES, BF16), (LANES, F32))
    return w.astype(BF16), gain.reshape(1, -1), groups, out_defs


def _odd_layout(w_in, qn_c, kn_c, d):
    kvw = d // 4
    q, k, v = w_in[:, :d], w_in[:, d:d + kvw], w_in[:, d + kvw:]
    w = jnp.concatenate([q, _dup(k), _dup(v)], axis=1)
    nh = d // HEAD_DIM
    gain = jnp.concatenate([jnp.tile(qn_c, nh) * (HEAD_DIM ** -0.5 * LOG2E),
                            jnp.tile(kn_c, 2 * kvw // HEAD_DIM), jnp.ones((2 * kvw,), F32)])
    half = d // 2
    groups = (
        (0, half, ((0, half, 0, 0, "norm"),)),
        (half, half, ((0, half, 0, half, "norm"),)),
        (d, 2 * kvw, ((0, 2 * kvw, 1, 0, "norm"),)),
        (d + 2 * kvw, 2 * kvw, ((0, 2 * kvw, 2, 0, "plain"),)),
    )
    out_defs = ((d, BF16), (2 * kvw, BF16), (2 * kvw, BF16))
    return w.astype(BF16), gain.reshape(1, -1), groups, out_defs


def kernel(x, c, ada_w, ada_b, norm_mix, norm_ffn, w_out, ffn_gate, ffn_up, ffn_down,
           w_in_even, qn_a, kn_a, lam_q1, lam_k1, lam_q2, lam_k2, subln_a, qn_b, kn_b,
           w_in_odd, qn_c, kn_c, sinks_c):
    depth, d = norm_mix.shape
    mod = _ada_mod(c, ada_w, ada_b)
    for l in range(depth):
        sh1, sc1, g1, sh2, sc2, g2 = [mod[l, :, None, n * d:(n + 1) * d] for n in range(6)]
        if l % 2 == 0:
            e = l // 2
            w, gain, groups, out_defs = _even_layout(w_in_even[e], qn_a[e], kn_a[e],
                                                     qn_b[e], kn_b[e], d)
            aq, ak, av, bq, iq, bk2, bv2, ik2, iwp = _proj(
                x, norm_mix[l], sc1, sh1, w, gain, groups, out_defs)
            lam_vecs = jnp.stack([lam_q1[e], lam_k1[e], lam_q2[e], lam_k2[e]]).astype(F32)
            ya = _diff_attention(aq, ak, av, lam_vecs, subln_a[e], l)
            yb = _dsa_attention(bq, iq, iwp, ik2, bk2, bv2)
            ys = (ya, yb)
        else:
            o = l // 2
            w, gain, groups, out_defs = _odd_layout(w_in_odd[o], qn_c[o], kn_c[o], d)
            q, k2, v2 = _proj(x, norm_mix[l], sc1, sh1, w, gain, groups, out_defs)
            ys = (_swa_attention(q, k2, v2, sinks_c[o]),)
        x = _out_ffn(x, ys, w_out[l].astype(BF16), g1, norm_ffn[l], sc2, sh2, g2,
                     ffn_gate[l].astype(BF16), ffn_up[l].astype(BF16),
                     ffn_down[l].astype(BF16))
    return x
```

```python
import functools
import math

import jax
import jax.numpy as jnp
from jax import lax
from jax.experimental import pallas as pl
from jax.experimental.pallas import tpu as pltpu

HEAD_DIM = 64
NORM_EPS = 1e-6
TOPK_MAX = 256
WINDOW = 128

LANES = 128
SUB = 8
PACK = 16
VMEM_LIMIT = 56 * 1024 * 1024

PROJ_TM = 512
FFN_TM = 512
DIFF_T = 512
DIFF_KW = 512
DSA_TQ = 512
DSA_TK = 512
SWA_TQ = 1024
ROWS = 128
QBLK = 256
NACC = 4
LO_CHECKS = (10, 12)

NEG_BIG = -1e30
LOG2E = 1.4426950408889634
KEY_NEG_INF = -2139095041
KEY16_NEG_INF = -32641

F32 = jnp.float32
BF16 = jnp.bfloat16
_NT = (((1,), (1,)), ((), ()))


def _alibi_slopes(n):
    return [2.0 ** (-8.0 * (i + 1) / n) for i in range(n)]


def _tile_lanes(x, n):
    return x if n == 1 else jnp.concatenate([x] * n, axis=1)


def _ada_kernel(c_ref, w_ref, b_ref, o_ref):
    c = c_ref[...]
    cond = c * jax.nn.sigmoid(c)
    o_ref[0] = jnp.dot(cond, w_ref[0], preferred_element_type=F32,
                       precision=lax.Precision.HIGHEST) + b_ref[0]


def _ada_mod(c, ada_w, ada_b):
    depth, d, n = ada_w.shape
    b = c.shape[0]
    tn = 1536
    c_pad = jnp.zeros((SUB, d), F32).at[:b].set(c)
    out = pl.pallas_call(
        _ada_kernel,
        grid=(depth, n // tn),
        in_specs=[pl.BlockSpec((SUB, d), lambda l, j: (0, 0)),
                  pl.BlockSpec((1, d, tn), lambda l, j: (l, 0, j)),
                  pl.BlockSpec((1, 1, tn), lambda l, j: (l, 0, j))],
        out_specs=pl.BlockSpec((1, SUB, tn), lambda l, j: (l, 0, j)),
        out_shape=jax.ShapeDtypeStruct((depth, SUB, n), F32),
        compiler_params=pltpu.CompilerParams(
            dimension_semantics=("arbitrary", "arbitrary"), vmem_limit_bytes=VMEM_LIMIT),
        name="ada_mod",
    )(c_pad, ada_w, ada_b.reshape(depth, 1, n))
    return out[:, :b]


def _modulated_norm(x, g, sc, sh):
    ms = jnp.mean(x * x, axis=-1, keepdims=True)
    return (x * lax.rsqrt(ms + NORM_EPS) * g) * (1.0 + sc) + sh


def _proj_kernel(x_ref, g_ref, sc_ref, sh_ref, w_ref, bd_ref, gain_ref, *out_refs, groups):
    tm = x_ref.shape[1]
    half = tm // 2
    bd = bd_ref[...]
    nb = bd.shape[0]
    for rows in (slice(0, half), slice(half, tm)):
        h = _modulated_norm(x_ref[0, rows, :], g_ref[...], sc_ref[0], sh_ref[0]).astype(BF16)
        for (c0, width, parts) in groups:
            y = jnp.dot(h, w_ref[:, c0:c0 + width], preferred_element_type=F32)
            for (p0, pw, oi, o0, mode) in parts:
                step = min(nb, pw)
                for s in range(0, pw, step):
                    ys = y[:, p0 + s:p0 + s + step]
                    col = c0 + p0 + s
                    if mode == "norm":
                        ss = jnp.dot((ys * ys).astype(BF16), bd[:step, :step],
                                     preferred_element_type=F32)
                        ys = ys * lax.rsqrt(ss * (1.0 / HEAD_DIM) + NORM_EPS)
                    if mode != "plain":
                        ys = ys * gain_ref[:, col:col + step]
                    out_refs[oi][0, rows, o0 + s:o0 + s + step] = ys.astype(out_refs[oi].dtype)


def _proj(x, g, sc, sh, w, gain, groups, out_defs):
    bsz, s, d = x.shape
    c = w.shape[1]
    tm = PROJ_TM
    nb = 2 * LANES
    r = lax.broadcasted_iota(jnp.int32, (nb, nb), 0) // HEAD_DIM
    cc = lax.broadcasted_iota(jnp.int32, (nb, nb), 1) // HEAD_DIM
    bd = (r == cc).astype(BF16)
    const = lambda b, i: (0, 0)
    return pl.pallas_call(
        functools.partial(_proj_kernel, groups=groups),
        grid=(bsz, s // tm),
        in_specs=[pl.BlockSpec((1, tm, d), lambda b, i: (b, i, 0)),
                  pl.BlockSpec((1, d), const),
                  pl.BlockSpec((1, 1, d), lambda b, i: (b, 0, 0)),
                  pl.BlockSpec((1, 1, d), lambda b, i: (b, 0, 0)),
                  pl.BlockSpec((d, c), const),
                  pl.BlockSpec((nb, nb), const),
                  pl.BlockSpec((1, c), const)],
        out_specs=[pl.BlockSpec((1, tm, wd), lambda b, i: (b, i, 0)) for wd, _ in out_defs],
        out_shape=[jax.ShapeDtypeStruct((bsz, s, wd), dt) for wd, dt in out_defs],
        compiler_params=pltpu.CompilerParams(
            dimension_semantics=("arbitrary", "arbitrary"), vmem_limit_bytes=VMEM_LIMIT),
        name="in_proj",
    )(x, g.reshape(1, d), sc, sh, w, bd, gain)


def _diff_attn_kernel(q_ref, k_ref, v_ref, lam_ref, subln_ref, o_ref,
                      qs_ref, m_ref, acc_ref, *, t, kw, heads, slopes, lam_init):
    i = pl.program_id(1)
    q0 = i * t
    lane = lax.broadcasted_iota(jnp.int32, (t, LANES), 1)
    for h in range(heads):
        q = q_ref[0, :, h * LANES:(h + 1) * LANES]
        zero = jnp.zeros_like(q)
        qs_ref[h, :t] = jnp.where(lane < HEAD_DIM, q, zero)
        qs_ref[h, t:] = jnp.where(lane >= HEAD_DIM, q, zero)
    m_ref[...] = jnp.full(m_ref.shape, NEG_BIG, F32)
    acc_ref[...] = jnp.zeros(acc_ref.shape, F32)

    def step(start, width, masked):
        col = lax.broadcasted_iota(jnp.int32, (1, width), 1)
        rel = (start - q0 + col).astype(F32)
        nsl = width // LANES
        logits, values = [], []
        for h in range(heads):
            kc = k_ref[0, pl.ds(start, width), h * LANES:(h + 1) * LANES]
            vc = v_ref[0, pl.ds(start, width), h * LANES:(h + 1) * LANES]
            logits.append(lax.dot_general(qs_ref[h], kc, _NT, preferred_element_type=F32))
            values.append(jnp.concatenate([vc, jnp.ones_like(vc)], axis=1))
        for h in range(heads):
            alibi = (slopes[h] * LOG2E) * rel
            for r0 in range(0, 2 * t, ROWS):
                s = logits[h][r0:r0 + ROWS] + alibi
                if masked:
                    r = lax.broadcasted_iota(jnp.int32, (ROWS, width), 0) + (r0 % t)
                    c = lax.broadcasted_iota(jnp.int32, (ROWS, width), 1)
                    s = jnp.where(r >= c, s, NEG_BIG)
                m_prev = m_ref[h, r0:r0 + ROWS]
                m_next = jnp.maximum(m_prev, jnp.max(s, axis=1, keepdims=True))
                alpha = jnp.exp2(m_prev - m_next)
                p = jnp.exp2(s - _tile_lanes(m_next, nsl))
                acc_ref[h, r0:r0 + ROWS] = (
                    _tile_lanes(alpha, 2) * acc_ref[h, r0:r0 + ROWS]
                    + jnp.dot(p.astype(BF16), values[h], preferred_element_type=F32))
                m_ref[h, r0:r0 + ROWS] = m_next

    def body(j, carry):
        step(pl.multiple_of(j * kw, kw), kw, False)
        return carry

    nfull = q0 // kw
    lax.fori_loop(0, nfull, body, 0)

    if t < kw:
        @pl.when(q0 - nfull * kw > 0)
        def _():
            step(pl.multiple_of(q0 - t, t), t, False)

    step(pl.multiple_of(q0, t), t, True)

    lam_v = lam_ref[...]
    s1 = jnp.sum(lam_v[0:1] * lam_v[1:2], axis=-1, keepdims=True)
    s2 = jnp.sum(lam_v[2:3] * lam_v[3:4], axis=-1, keepdims=True)
    lam = jnp.exp(s1) - jnp.exp(s2) + lam_init
    for h in range(heads):
        acc = acc_ref[h]
        o = acc[:, :LANES] / acc[:, LANES:]
        y = o[:t] - lam * o[t:]
        ms = jnp.mean(y * y, axis=-1, keepdims=True)
        y = (y * lax.rsqrt(ms + NORM_EPS) * subln_ref[...]) * (1.0 - lam_init)
        o_ref[0, :, h * LANES:(h + 1) * LANES] = y.astype(o_ref.dtype)


def _diff_attention(q, k, v, lam_vecs, subln, layer_idx):
    bsz, s, width = q.shape
    heads = width // LANES
    t, kw = DIFF_T, DIFF_KW
    assert kw in (t, 2 * t)
    lam_init = 0.8 - 0.6 * math.exp(-0.3 * layer_idx)
    kern = functools.partial(_diff_attn_kernel, t=t, kw=kw, heads=heads,
                             slopes=_alibi_slopes(heads), lam_init=lam_init)
    return pl.pallas_call(
        kern,
        grid=(bsz, s // t),
        in_specs=[pl.BlockSpec((1, t, width), lambda b, i: (b, i, 0)),
                  pl.BlockSpec((1, s, width), lambda b, i: (b, 0, 0)),
                  pl.BlockSpec((1, s, width), lambda b, i: (b, 0, 0)),
                  pl.BlockSpec((4, HEAD_DIM), lambda b, i: (0, 0)),
                  pl.BlockSpec((1, LANES), lambda b, i: (0, 0))],
        out_specs=pl.BlockSpec((1, t, width), lambda b, i: (b, i, 0)),
        out_shape=jax.ShapeDtypeStruct((bsz, s, width), BF16),
        scratch_shapes=[pltpu.VMEM((heads, 2 * t, LANES), BF16),
                        pltpu.VMEM((heads, 2 * t, LANES), F32),
                        pltpu.VMEM((heads, 2 * t, 2 * LANES), F32)],
        compiler_params=pltpu.CompilerParams(
            dimension_semantics=("arbitrary", "arbitrary"), vmem_limit_bytes=VMEM_LIMIT),
        name="diff_attn",
    )(q, k, v, lam_vecs, subln.reshape(1, LANES))


def _key_to_float(k):
    return lax.bitcast_convert_type(k ^ ((k >> 31) & jnp.int32(0x7FFFFFFF)), F32)


def _floor_bf16(x):
    r = x.astype(BF16).astype(F32)
    ulp_down = jnp.where(x < 0, jnp.int32(0x10000), jnp.int32(-0x10000))
    down = lax.bitcast_convert_type(lax.bitcast_convert_type(r, jnp.int32) + ulp_down, F32)
    return jnp.where(r > x, down, r).astype(BF16)


def _dsa_kernel(bq_ref, iq_ref, iw_ref, ik_ref, bk_ref, vt_ref, o_ref,
                sc_ref, scb_ref, iqs_ref, bqs_ref, wt_ref, need_ref, m_ref, acc_ref,
                *, tq, tk, heads, topk, slopes):
    i = pl.program_id(1)
    q0 = i * tq
    nch = q0 // tk + 1
    krow = lax.broadcasted_iota(jnp.int32, (tk, tq), 0)
    qcol = lax.broadcasted_iota(jnp.int32, (tk, tq), 1)

    lane = lax.broadcasted_iota(jnp.int32, (tq, LANES), 1)
    for h in range(heads):
        keep = (lane < HEAD_DIM) if h % 2 == 0 else (lane >= HEAD_DIM)
        tile = slice((h // 2) * LANES, (h // 2 + 1) * LANES)
        iqb = iq_ref[0, :, tile]
        bqb = bq_ref[0, :, tile]
        iqs_ref[h] = jnp.where(keep, iqb, jnp.zeros_like(iqb))
        bqs_ref[h * tq:(h + 1) * tq] = jnp.where(keep, bqb, jnp.zeros_like(bqb))
    wt_ref[...] = iw_ref[0].T

    def scores(j, r0, nrows, c_lo, band):
        start = pl.multiple_of(j * tk, tk)
        ikc = ik_ref[0, pl.ds(start + r0, nrows), :]
        sc = None
        for h in range(heads):
            logit = lax.dot_general(ikc, iqs_ref[h, c_lo:, :], _NT,
                                    preferred_element_type=F32)
            term = jnp.maximum(logit, 0.0) * wt_ref[h:h + 1, c_lo:]
            sc = term if sc is None else sc + term
        sc = sc + 0.0
        if band:
            kr = lax.broadcasted_iota(jnp.int32, sc.shape, 0) + (r0 - c_lo)
            qc = lax.broadcasted_iota(jnp.int32, sc.shape, 1)
            sc = jnp.where(kr <= qc, sc, -jnp.inf)
            if c_lo:
                sc = jnp.concatenate([jnp.full((nrows, c_lo), -jnp.inf, F32), sc], axis=1)
        sc_ref[j, r0:r0 + nrows, :] = sc
        scb_ref[j, r0:r0 + nrows, :] = _floor_bf16(sc)

    def score_body(j, carry):
        scores(j, 0, tk, 0, False)
        return carry

    lax.fori_loop(0, nch - 1, score_body, 0)
    for r0 in range(0, tk, QBLK):
        scores(nch - 1, r0, QBLK, r0, True)

    def count_in(ref, rows, pred_fn, thr):
        dt = ref.dtype

        def add_rows(j, accs, r_lo, r_hi, c_lo):
            accs = list(accs)
            for r in range(r_lo // rows, r_hi // rows):
                blk = ref[j, r * rows:(r + 1) * rows, c_lo:]
                hit = jnp.where(pred_fn(blk, thr[:, c_lo:]), jnp.ones(blk.shape, dt),
                                jnp.zeros(blk.shape, dt))
                a = accs[r % NACC]
                accs[r % NACC] = (a + hit if c_lo == 0 else
                                  jnp.concatenate([a[:, :c_lo], a[:, c_lo:] + hit], axis=1))
            return tuple(accs)

        accs = lax.fori_loop(0, nch - 1, lambda j, accs: add_rows(j, accs, 0, tk, 0),
                             tuple(jnp.zeros((rows, tq), dt) for _ in range(NACC)))
        for r0 in range(0, tk, LANES):
            accs = add_rows(nch - 1, accs, r0, r0 + LANES, r0)
        cnt = accs[0].astype(F32)
        for a in accs[1:]:
            cnt = cnt + a.astype(F32)
        return jnp.broadcast_to(jnp.sum(cnt, axis=0, keepdims=True), (SUB, tq))

    def count(pred_fn, thr):
        return count_in(sc_ref, SUB, pred_fn, thr)

    ge = lambda a, b: a >= b
    gt = lambda a, b: a > b

    def count_b(thr_b):
        return count_in(scb_ref, PACK, ge, thr_b)

    def key16_to_bits(k):
        return lax.shift_left(k ^ ((k >> 31) & jnp.int32(0x7FFF)), 16)

    def hi_body(b, carry):
        key, n_ge = carry
        cand = key + lax.shift_left(jnp.int32(1), 15 - b)
        cf = lax.bitcast_convert_type(key16_to_bits(cand), F32)
        cnt = count_b(jnp.concatenate([cf, cf], axis=0).astype(BF16))
        ok = cnt >= float(topk)
        return jnp.where(ok, cand, key), jnp.where(ok, cnt, n_ge)

    key16, n_ge = lax.fori_loop(0, 16, hi_body, (jnp.full((SUB, tq), -2 ** 15, jnp.int32),
                                                 jnp.zeros((SUB, tq), F32)))
    bits = key16_to_bits(key16)
    base = bits ^ ((bits >> 31) & jnp.int32(0x7FFFFFFF))
    base = jnp.where(key16 <= KEY16_NEG_INF, KEY_NEG_INF, base)

    def decode(key):
        return jnp.where(key <= KEY_NEG_INF, -jnp.inf, _key_to_float(key))

    def lo_bits(first, stop, carry):
        def body(b, carry):
            off, n_ge = carry
            step = lax.shift_left(jnp.int32(1), 15 - b)
            cnt = count(ge, _key_to_float(base + off + step))
            ok = cnt >= float(topk)
            return jnp.where(ok, off + step, off), jnp.where(ok, cnt, n_ge)
        return lax.fori_loop(first, stop, body, carry)

    def settled(carry):
        off, n_ge = carry
        n_gt = count(gt, decode(base + off))
        final = (n_ge == float(topk)) | ((n_ge > float(topk)) & (n_gt < float(topk)))
        return jnp.min(jnp.where(final, 1.0, 0.0))

    keep_state = lambda st: st
    state = lo_bits(0, LO_CHECKS[0], (jnp.zeros((SUB, tq), jnp.int32), n_ge))
    done = settled(state)
    state = lax.cond(done > 0.0, keep_state,
                     functools.partial(lo_bits, LO_CHECKS[0], LO_CHECKS[1]), state)
    done = lax.cond(done > 0.0, lambda st: jnp.float32(1.0), settled, state)
    off, n_ge = lax.cond(done > 0.0, keep_state, functools.partial(lo_bits, LO_CHECKS[1], 16), state)
    thr = decode(base + off)
    excess = jnp.where(n_ge > float(topk), 1.0, 0.0)

    @pl.when(jnp.max(excess) > 0.0)
    def _():
        need_ref[...] = float(topk) - count(gt, thr)

    krow_g = lax.broadcasted_iota(jnp.int32, (tk, LANES), 0)
    qcol_g = lax.broadcasted_iota(jnp.int32, (tk, LANES), 1)
    for c0 in range(0, tq, LANES):
        ls = slice(c0, c0 + LANES)
        thr_c = thr[0:1, ls]
        group_ties = jnp.max(excess[:, ls]) > 0.0
        causal_c = ((nch - 1) * tk + krow_g) <= (q0 + c0 + qcol_g)

        @pl.when(jnp.logical_not(group_ties))
        def _():
            def body(j, carry):
                sc_ref[j, :, ls] = jnp.where(sc_ref[j, :, ls] >= thr_c, 0.0, NEG_BIG)
                return carry
            lax.fori_loop(0, nch - 1, body, 0)
            last = nch - 1
            keep = (sc_ref[last, :, ls] >= thr_c) & causal_c
            sc_ref[last, :, ls] = jnp.where(keep, 0.0, NEG_BIG)

        @pl.when(group_ties)
        def _():
            need = need_ref[0:1, ls]
            r2 = lax.broadcasted_iota(jnp.int32, (tk, tk), 0)
            c2 = lax.broadcasted_iota(jnp.int32, (tk, tk), 1)
            lower = jnp.where(c2 <= r2, 1.0, 0.0).astype(BF16)

            def body(j, seen):
                sc = sc_ref[j, :, ls]
                eq = sc == thr_c
                eqf = jnp.where(eq, 1.0, 0.0)
                rank = seen + jnp.dot(lower, eqf.astype(BF16), preferred_element_type=F32)
                keep = (sc > thr_c) | (eq & (rank <= need))
                keep = keep & ((j * tk + krow_g) <= (q0 + c0 + qcol_g))
                sc_ref[j, :, ls] = jnp.where(keep, 0.0, NEG_BIG)
                return seen + jnp.sum(eqf, axis=0, keepdims=True)

            lax.fori_loop(0, nch, body, jnp.zeros((1, LANES), F32))

    m_ref[...] = jnp.full(m_ref.shape, NEG_BIG, F32)
    acc_ref[...] = jnp.zeros(acc_ref.shape, F32)
    krow_t = lax.broadcasted_iota(jnp.int32, (tk, LANES), 0)

    def attend(j, last):
        start = pl.multiple_of(j * tk, tk)
        kc = bk_ref[0, pl.ds(start, tk), :]
        vt = vt_ref[0, j]
        rel = (start - q0 + krow_t).astype(F32)

        def qk(h):
            return lax.dot_general(kc, bqs_ref[h * tq:(h + 1) * tq], _NT,
                                   preferred_element_type=F32)

        logits = {0: qk(0)}
        for h in range(heads):
            if h + 1 < heads:
                logits[h + 1] = qk(h + 1)
            logit = logits.pop(h)
            alibi = _tile_lanes((slopes[h] * LOG2E) * rel, QBLK // LANES)
            for c0 in range(0, tq, QBLK):
                nk = c0 + QBLK if last else tk
                s = logit[:nk, c0:c0 + QBLK] + sc_ref[j, :nk, c0:c0 + QBLK] + alibi[:nk]
                m_prev = m_ref[h, :, c0:c0 + QBLK]
                m_next = jnp.maximum(m_prev, jnp.max(s, axis=0, keepdims=True))
                alpha = jnp.exp2(m_prev - m_next)
                p = jnp.exp2(s - m_next).astype(BF16)
                acc_ref[h, :, c0:c0 + QBLK] = (alpha * acc_ref[h, :, c0:c0 + QBLK]
                                               + jnp.dot(vt[:, :nk], p, preferred_element_type=F32))
                m_ref[h, :, c0:c0 + QBLK] = m_next

    def attn_body(j, carry):
        attend(j, False)
        return carry

    lax.fori_loop(0, nch - 1, attn_body, 0)
    attend(nch - 1, True)

    for g in range(heads // 2):
        a = acc_ref[2 * g]
        b = acc_ref[2 * g + 1]
        pair = jnp.concatenate([a[:HEAD_DIM] / a[HEAD_DIM:HEAD_DIM + 1],
                                b[:HEAD_DIM] / b[HEAD_DIM:HEAD_DIM + 1]], axis=0)
        o_ref[0, :, g * LANES:(g + 1) * LANES] = pair.T.astype(o_ref.dtype)


def _dsa_attention(bq, iq, iwp, ik2, bk2, bv2):
    bsz, s, width = bq.shape
    heads = width // HEAD_DIM
    tq, tk = DSA_TQ, DSA_TK
    assert tq == tk
    topk = min(TOPK_MAX, s // 4)
    nck = s // tk
    kern = functools.partial(_dsa_kernel, tq=tq, tk=tk, heads=heads, topk=topk,
                             slopes=_alibi_slopes(heads))
    vt = bv2[:, :, :HEAD_DIM].reshape(bsz, nck, tk, HEAD_DIM).transpose(0, 1, 3, 2)
    vt = jnp.concatenate([vt, jnp.ones_like(vt)], axis=2)
    qspec = lambda w: pl.BlockSpec((1, tq, w), lambda b, i: (b, i, 0))
    kspec = pl.BlockSpec((1, s, LANES), lambda b, i: (b, 0, 0))
    return pl.pallas_call(
        kern,
        grid=(bsz, s // tq),
        in_specs=[qspec(width), qspec(width), qspec(LANES), kspec, kspec,
                  pl.BlockSpec((1, nck, LANES, tk), lambda b, i: (b, 0, 0, 0))],
        out_specs=pl.BlockSpec((1, tq, width), lambda b, i: (b, i, 0)),
        out_shape=jax.ShapeDtypeStruct((bsz, s, width), BF16),
        scratch_shapes=[pltpu.VMEM((nck, tk, tq), F32),
                        pltpu.VMEM((nck, tk, tq), BF16),
                        pltpu.VMEM((heads, tq, LANES), BF16),
                        pltpu.VMEM((heads * tq, LANES), BF16),
                        pltpu.VMEM((LANES, tq), F32),
                        pltpu.VMEM((SUB, tq), F32),
                        pltpu.VMEM((heads, 1, tq), F32),
                        pltpu.VMEM((heads, LANES, tq), F32)],
        compiler_params=pltpu.CompilerParams(
            dimension_semantics=("arbitrary", "arbitrary"), vmem_limit_bytes=VMEM_LIMIT),
        name="dsa_attn",
    )(bq, iq, iwp, ik2, bk2, vt)


def _stack_heads(x, heads):
    lane = lax.broadcasted_iota(jnp.int32, (x.shape[0], LANES), 1)
    lo = lane < HEAD_DIM
    parts = []
    for h in range(heads):
        blk = x[:, (h // 2) * LANES:(h // 2 + 1) * LANES]
        keep = lo if h % 2 == 0 else jnp.logical_not(lo)
        parts.append(jnp.where(keep, blk, jnp.zeros_like(blk)))
    return jnp.concatenate(parts, axis=0)


def _swa_kernel(slope_ref, sink_ref, q_ref, kp_ref, kc_ref, vp_ref, vc_ref, o_ref, *, tq, group):
    kv = pl.program_id(1)
    i = pl.program_id(2)
    nk = 2 * WINDOW
    kk = jnp.concatenate([kp_ref[0], kc_ref[0]], axis=0)
    vv = jnp.concatenate([vp_ref[0], vc_ref[0]], axis=0)
    r = lax.broadcasted_iota(jnp.int32, (WINDOW, nk), 0)
    c = lax.broadcasted_iota(jnp.int32, (WINDOW, nk), 1)
    dist = WINDOW + r - c
    band = (dist >= 0) & (dist < WINDOW)
    distf = dist.astype(F32)
    lane = lax.broadcasted_iota(jnp.int32, (WINDOW, LANES), 1)
    biases = [jnp.where(band, -(slope_ref[kv * group + g] * LOG2E) * distf, NEG_BIG)
              for g in range(group)]
    for blk in range(tq // WINDOW):
        rows = slice(blk * WINDOW, (blk + 1) * WINDOW)
        qs = _stack_heads(q_ref[0, rows, :], group)
        kb = kk[blk * WINDOW:blk * WINDOW + nk]
        vb = vv[blk * WINDOW:blk * WINDOW + nk]
        logit = lax.dot_general(qs, kb, _NT, preferred_element_type=F32)
        outs = []
        for g in range(group):
            hq = kv * group + g
            s = logit[g * WINDOW:(g + 1) * WINDOW] + biases[g]
            if blk == 0:
                s = jnp.where(c >= jnp.where(i == 0, WINDOW, 0), s, NEG_BIG)
            sink = sink_ref[hq] * LOG2E
            m = jnp.maximum(jnp.max(s, axis=1, keepdims=True), sink)
            p = jnp.exp2(s - m)
            denom = jnp.sum(p, axis=1, keepdims=True) + jnp.exp2(sink - m)
            outs.append(jnp.dot(p.astype(BF16), vb, preferred_element_type=F32) / denom)
        for g in range(group // 2):
            o_ref[0, rows, g * LANES:(g + 1) * LANES] = jnp.where(
                lane < HEAD_DIM, outs[2 * g], outs[2 * g + 1]).astype(o_ref.dtype)


def _swa_attention(q, k2, v2, sinks):
    bsz, s, width = q.shape
    heads = width // HEAD_DIM
    kvh = k2.shape[2] // LANES
    group = heads // kvh
    tq = min(SWA_TQ, s)
    slopes = jnp.asarray(_alibi_slopes(heads), F32)
    r = tq // WINDOW
    prev = lambda b, kv, i: (b, jnp.maximum(i * r - 1, 0), kv)
    cur = lambda b, kv, i: (b, i, kv)
    smem = pl.BlockSpec(memory_space=pltpu.SMEM)
    return pl.pallas_call(
        functools.partial(_swa_kernel, tq=tq, group=group),
        grid=(bsz, kvh, s // tq),
        in_specs=[smem, smem,
                  pl.BlockSpec((1, tq, group * HEAD_DIM), cur),
                  pl.BlockSpec((1, WINDOW, LANES), prev),
                  pl.BlockSpec((1, tq, LANES), cur),
                  pl.BlockSpec((1, WINDOW, LANES), prev),
                  pl.BlockSpec((1, tq, LANES), cur)],
        out_specs=pl.BlockSpec((1, tq, group * HEAD_DIM), cur),
        out_shape=jax.ShapeDtypeStruct((bsz, s, width), BF16),
        compiler_params=pltpu.CompilerParams(
            dimension_semantics=("arbitrary", "arbitrary", "arbitrary"),
            vmem_limit_bytes=VMEM_LIMIT),
        name="swa_attn",
    )(slopes, sinks.astype(F32), q, k2, k2, v2, v2)


def _out_ffn_kernel(*refs, ny):
    x_ref = refs[0]
    y_refs = refs[1:1 + ny]
    wo_ref, g1_ref, n_ref, sc_ref, sh_ref, g2_ref, wg_ref, wu_ref, wd_ref, o_ref = refs[1 + ny:]
    tm = x_ref.shape[1]
    half = tm // 2
    for rows in (slice(0, half), slice(half, tm)):
        mix = None
        r0 = 0
        for y_ref in y_refs:
            w = y_ref.shape[2]
            part = jnp.dot(y_ref[0, rows, :], wo_ref[r0:r0 + w, :], preferred_element_type=F32)
            mix = part if mix is None else mix + part
            r0 += w
        x = x_ref[0, rows, :] + g1_ref[0] * mix
        h = _modulated_norm(x, n_ref[...], sc_ref[0], sh_ref[0]).astype(BF16)
        gate = jnp.dot(h, wg_ref[...], preferred_element_type=F32)
        up = jnp.dot(h, wu_ref[...], preferred_element_type=F32)
        act = (gate * jax.nn.sigmoid(gate) * up).astype(BF16)
        o_ref[0, rows, :] = x + g2_ref[0] * jnp.dot(act, wd_ref[...], preferred_element_type=F32)


def _out_ffn(x, ys, wo, g1, n, sc, sh, g2, wg, wu, wd):
    bsz, s, d = x.shape
    f = wg.shape[1]
    tm = FFN_TM
    const = lambda b, i: (0, 0)
    tok = lambda b, i: (b, i, 0)
    mod = lambda b, i: (b, 0, 0)
    once = pl.Buffered(1)
    return pl.pallas_call(
        functools.partial(_out_ffn_kernel, ny=len(ys)),
        grid=(bsz, s // tm),
        in_specs=[pl.BlockSpec((1, tm, d), tok)]
                 + [pl.BlockSpec((1, tm, y.shape[2]), tok) for y in ys]
                 + [pl.BlockSpec((d, d), const, pipeline_mode=once),
                  pl.BlockSpec((1, 1, d), mod),
                  pl.BlockSpec((1, d), const),
                  pl.BlockSpec((1, 1, d), mod),
                  pl.BlockSpec((1, 1, d), mod),
                  pl.BlockSpec((1, 1, d), mod),
                  pl.BlockSpec((d, f), const, pipeline_mode=once),
                  pl.BlockSpec((d, f), const, pipeline_mode=once),
                  pl.BlockSpec((f, d), const, pipeline_mode=once)],
        out_specs=pl.BlockSpec((1, tm, d), tok),
        out_shape=jax.ShapeDtypeStruct((bsz, s, d), F32),
        compiler_params=pltpu.CompilerParams(
            dimension_semantics=("arbitrary", "arbitrary"), vmem_limit_bytes=VMEM_LIMIT),
        name="out_ffn",
    )(x, *ys, wo, g1, n.reshape(1, d), sc, sh, g2, wg, wu, wd)


def _dup(w):
    d, c = w.shape
    w = w.reshape(d, c // HEAD_DIM, 1, HEAD_DIM)
    return jnp.broadcast_to(w, (d, c // HEAD_DIM, 2, HEAD_DIM)).reshape(d, 2 * c)


def _even_layout(w_in, qn_a, kn_a, qn_b, kn_b, d):
    a = d // 2
    sizes = (a, a, a, a, HEAD_DIM, HEAD_DIM, a, HEAD_DIM, a // HEAD_DIM)
    offs = [0]
    for sz in sizes:
        offs.append(offs[-1] + sz)
    aq, ak, av, bq, bk, bv, iq, ik, iw = [w_in[:, offs[n]:offs[n + 1]] for n in range(9)]
    nh = a // HEAD_DIM
    iw_pad = jnp.zeros((d, LANES), w_in.dtype).at[:, :nh].set(iw)
    w = jnp.concatenate([aq, ak, av, bq, iq, _dup(bk), _dup(bv), _dup(ik), iw_pad], axis=1)
    qscale = HEAD_DIM ** -0.5 * LOG2E
    ones = lambda n: jnp.ones((n,), F32)
    gain = jnp.concatenate([
        jnp.tile(qn_a, nh) * qscale, jnp.tile(kn_a, nh), ones(a),
        jnp.tile(qn_b, nh) * qscale, ones(a) * HEAD_DIM ** -0.5,
        jnp.tile(kn_b, 2), ones(LANES), ones(LANES), ones(LANES) * nh ** -0.5])
    groups = (
        (0, a, ((0, a, 0, 0, "norm"),)),
        (a, a, ((0, a, 1, 0, "norm"),)),
        (2 * a, a, ((0, a, 2, 0, "plain"),)),
        (3 * a, a, ((0, a, 3, 0, "norm"),)),
        (4 * a, a, ((0, a, 4, 0, "scale"),)),
        (5 * a, 4 * LANES, ((0, LANES, 5, 0, "norm"), (LANES, LANES, 6, 0, "plain"),
                            (2 * LANES, LANES, 7, 0, "plain"), (3 * LANES, LANES, 8, 0, "scale"))),
    )
    out_defs = ((a, BF16), (a, BF16), (a, BF16), (a, BF16), (a, BF16),
                (LANES, BF16), (LANES, BF16), (LANES, BF16), (LANES, F32))
    return w.astype(BF16), gain.reshape(1, -1), groups, out_defs


def _odd_layout(w_in, qn_c, kn_c, d):
    kvw = d // 4
    q, k, v = w_in[:, :d], w_in[:, d:d + kvw], w_in[:, d + kvw:]
    w = jnp.concatenate([q, _dup(k), _dup(v)], axis=1)
    nh = d // HEAD_DIM
    gain = jnp.concatenate([jnp.tile(qn_c, nh) * (HEAD_DIM ** -0.5 * LOG2E),
                            jnp.tile(kn_c, 2 * kvw // HEAD_DIM), jnp.ones((2 * kvw,), F32)])
    half = d // 2
    groups = (
        (0, half, ((0, half, 0, 0, "norm"),)),
        (half, half, ((0, half, 0, half, "norm"),)),
        (d, 2 * kvw, ((0, 2 * kvw, 1, 0, "norm"),)),
        (d + 2 * kvw, 2 * kvw, ((0, 2 * kvw, 2, 0, "plain"),)),
    )
    out_defs = ((d, BF16), (2 * kvw, BF16), (2 * kvw, BF16))
    return w.astype(BF16), gain.reshape(1, -1), groups, out_defs


def kernel(x, c, ada_w, ada_b, norm_mix, norm_ffn, w_out, ffn_gate, ffn_up, ffn_down,
           w_in_even, qn_a, kn_a, lam_q1, lam_k1, lam_q2, lam_k2, subln_a, qn_b, kn_b,
           w_in_odd, qn_c, kn_c, sinks_c):
    depth, d = norm_mix.shape
    mod = _ada_mod(c, ada_w, ada_b)
    for l in range(depth):
        sh1, sc1, g1, sh2, sc2, g2 = [mod[l, :, None, n * d:(n + 1) * d] for n in range(6)]
        if l % 2 == 0:
            e = l // 2
            w, gain, groups, out_defs = _even_layout(w_in_even[e], qn_a[e], kn_a[e],
                                                     qn_b[e], kn_b[e], d)
            aq, ak, av, bq, iq, bk2, bv2, ik2, iwp = _proj(
                x, norm_mix[l], sc1, sh1, w, gain, groups, out_defs)
            lam_vecs = jnp.stack([lam_q1[e], lam_k1[e], lam_q2[e], lam_k2[e]]).astype(F32)
            ya = _diff_attention(aq, ak, av, lam_vecs, subln_a[e], l)
            yb = _dsa_attention(bq, iq, iwp, ik2, bk2, bv2)
            ys = (ya, yb)
        else:
            o = l // 2
            w, gain, groups, out_defs = _odd_layout(w_in_odd[o], qn_c[o], kn_c[o], d)
            q, k2, v2 = _proj(x, norm_mix[l], sc1, sh1, w, gain, groups, out_defs)
            ys = (_swa_attention(q, k2, v2, sinks_c[o]),)
        x = _out_ffn(x, ys, w_out[l].astype(BF16), g1, norm_ffn[l], sc2, sh2, g2,
                     ffn_gate[l].astype(BF16), ffn_up[l].astype(BF16),
                     ffn_down[l].astype(BF16))
    return x
```

```python
import functools
import math

import jax
import jax.numpy as jnp
from jax import lax
from jax.experimental import pallas as pl
from jax.experimental.pallas import tpu as pltpu

HEAD_DIM = 64
NORM_EPS = 1e-6
TOPK_MAX = 256
WINDOW = 128

LANES = 128
SUB = 8
PACK = 16
VMEM_LIMIT = 56 * 1024 * 1024

PROJ_TM = 512
FFN_TM = 512
DIFF_T = 512
DIFF_KW = 512
DSA_TQ = 512
DSA_TK = 512
SWA_TQ = 1024
ROWS = 128
QBLK = 256
NACC = 4
LO_CHECKS = (10, 12)

NEG_BIG = -1e30
LOG2E = 1.4426950408889634
KEY_NEG_INF = -2139095041
KEY16_NEG_INF = -32641

F32 = jnp.float32
BF16 = jnp.bfloat16
_NT = (((1,), (1,)), ((), ()))


def _alibi_slopes(n):
    return [2.0 ** (-8.0 * (i + 1) / n) for i in range(n)]


def _tile_lanes(x, n):
    return x if n == 1 else jnp.concatenate([x] * n, axis=1)


def _ada_kernel(c_ref, w_ref, b_ref, o_ref):
    c = c_ref[...]
    cond = c * jax.nn.sigmoid(c)
    o_ref[0] = jnp.dot(cond, w_ref[0], preferred_element_type=F32,
                       precision=lax.Precision.HIGHEST) + b_ref[0]


def _ada_mod(c, ada_w, ada_b):
    depth, d, n = ada_w.shape
    b = c.shape[0]
    tn = 1536
    c_pad = jnp.zeros((SUB, d), F32).at[:b].set(c)
    out = pl.pallas_call(
        _ada_kernel,
        grid=(depth, n // tn),
        in_specs=[pl.BlockSpec((SUB, d), lambda l, j: (0, 0)),
                  pl.BlockSpec((1, d, tn), lambda l, j: (l, 0, j)),
                  pl.BlockSpec((1, 1, tn), lambda l, j: (l, 0, j))],
        out_specs=pl.BlockSpec((1, SUB, tn), lambda l, j: (l, 0, j)),
        out_shape=jax.ShapeDtypeStruct((depth, SUB, n), F32),
        compiler_params=pltpu.CompilerParams(
            dimension_semantics=("arbitrary", "arbitrary"), vmem_limit_bytes=VMEM_LIMIT),
        name="ada_mod",
    )(c_pad, ada_w, ada_b.reshape(depth, 1, n))
    return out[:, :b]


def _modulated_norm(x, g, sc, sh):
    ms = jnp.mean(x * x, axis=-1, keepdims=True)
    return (x * lax.rsqrt(ms + NORM_EPS) * g) * (1.0 + sc) + sh


def _proj_kernel(x_ref, g_ref, sc_ref, sh_ref, w_ref, bd_ref, gain_ref, *out_refs, groups):
    tm = x_ref.shape[1]
    half = tm // 2
    bd = bd_ref[...]
    nb = bd.shape[0]
    for rows in (slice(0, half), slice(half, tm)):
        h = _modulated_norm(x_ref[0, rows, :], g_ref[...], sc_ref[0], sh_ref[0]).astype(BF16)
        for (c0, width, parts) in groups:
            y = jnp.dot(h, w_ref[:, c0:c0 + width], preferred_element_type=F32)
            for (p0, pw, oi, o0, mode) in parts:
                step = min(nb, pw)
                for s in range(0, pw, step):
                    ys = y[:, p0 + s:p0 + s + step]
                    col = c0 + p0 + s
                    if mode == "norm":
                        ss = jnp.dot((ys * ys).astype(BF16), bd[:step, :step],
                                     preferred_element_type=F32)
                        ys = ys * lax.rsqrt(ss * (1.0 / HEAD_DIM) + NORM_EPS)
                    if mode != "plain":
                        ys = ys * gain_ref[:, col:col + step]
                    out_refs[oi][0, rows, o0 + s:o0 + s + step] = ys.astype(out_refs[oi].dtype)


def _proj(x, g, sc, sh, w, gain, groups, out_defs):
    bsz, s, d = x.shape
    c = w.shape[1]
    tm = PROJ_TM
    nb = 2 * LANES
    r = lax.broadcasted_iota(jnp.int32, (nb, nb), 0) // HEAD_DIM
    cc = lax.broadcasted_iota(jnp.int32, (nb, nb), 1) // HEAD_DIM
    bd = (r == cc).astype(BF16)
    const = lambda b, i: (0, 0)
    return pl.pallas_call(
        functools.partial(_proj_kernel, groups=groups),
        grid=(bsz, s // tm),
        in_specs=[pl.BlockSpec((1, tm, d), lambda b, i: (b, i, 0)),
                  pl.BlockSpec((1, d), const),
                  pl.BlockSpec((1, 1, d), lambda b, i: (b, 0, 0)),
                  pl.BlockSpec((1, 1, d), lambda b, i: (b, 0, 0)),
                  pl.BlockSpec((d, c), const),
                  pl.BlockSpec((nb, nb), const),
                  pl.BlockSpec((1, c), const)],
        out_specs=[pl.BlockSpec((1, tm, wd), lambda b, i: (b, i, 0)) for wd, _ in out_defs],
        out_shape=[jax.ShapeDtypeStruct((bsz, s, wd), dt) for wd, dt in out_defs],
        compiler_params=pltpu.CompilerParams(
            dimension_semantics=("arbitrary", "arbitrary"), vmem_limit_bytes=VMEM_LIMIT),
        name="in_proj",
    )(x, g.reshape(1, d), sc, sh, w, bd, gain)


def _diff_attn_kernel(q_ref, k_ref, v_ref, lam_ref, subln_ref, o_ref,
                      qs_ref, m_ref, acc_ref, *, t, kw, heads, slopes, lam_init):
    i = pl.program_id(1)
    q0 = i * t
    lane = lax.broadcasted_iota(jnp.int32, (t, LANES), 1)
    for h in range(heads):
        q = q_ref[0, :, h * LANES:(h + 1) * LANES]
        zero = jnp.zeros_like(q)
        qs_ref[h, :t] = jnp.where(lane < HEAD_DIM, q, zero)
        qs_ref[h, t:] = jnp.where(lane >= HEAD_DIM, q, zero)
    m_ref[...] = jnp.full(m_ref.shape, NEG_BIG, F32)
    acc_ref[...] = jnp.zeros(acc_ref.shape, F32)

    def step(start, width, masked):
        col = lax.broadcasted_iota(jnp.int32, (1, width), 1)
        rel = (start - q0 + col).astype(F32)
        nsl = width // LANES
        logits, values = [], []
        for h in range(heads):
            kc = k_ref[0, pl.ds(start, width), h * LANES:(h + 1) * LANES]
            vc = v_ref[0, pl.ds(start, width), h * LANES:(h + 1) * LANES]
            logits.append(lax.dot_general(qs_ref[h], kc, _NT, preferred_element_type=F32))
            values.append(jnp.concatenate([vc, jnp.ones_like(vc)], axis=1))
        for h in range(heads):
            alibi = (slopes[h] * LOG2E) * rel
            for r0 in range(0, 2 * t, ROWS):
                s = logits[h][r0:r0 + ROWS] + alibi
                if masked:
                    r = lax.broadcasted_iota(jnp.int32, (ROWS, width), 0) + (r0 % t)
                    c = lax.broadcasted_iota(jnp.int32, (ROWS, width), 1)
                    s = jnp.where(r >= c, s, NEG_BIG)
                m_prev = m_ref[h, r0:r0 + ROWS]
                m_next = jnp.maximum(m_prev, jnp.max(s, axis=1, keepdims=True))
                alpha = jnp.exp2(m_prev - m_next)
                p = jnp.exp2(s - _tile_lanes(m_next, nsl))
                acc_ref[h, r0:r0 + ROWS] = (
                    _tile_lanes(alpha, 2) * acc_ref[h, r0:r0 + ROWS]
                    + jnp.dot(p.astype(BF16), values[h], preferred_element_type=F32))
                m_ref[h, r0:r0 + ROWS] = m_next

    def body(j, carry):
        step(pl.multiple_of(j * kw, kw), kw, False)
        return carry

    nfull = q0 // kw
    lax.fori_loop(0, nfull, body, 0)

    if t < kw:
        @pl.when(q0 - nfull * kw > 0)
        def _():
            step(pl.multiple_of(q0 - t, t), t, False)

    step(pl.multiple_of(q0, t), t, True)

    lam_v = lam_ref[...]
    s1 = jnp.sum(lam_v[0:1] * lam_v[1:2], axis=-1, keepdims=True)
    s2 = jnp.sum(lam_v[2:3] * lam_v[3:4], axis=-1, keepdims=True)
    lam = jnp.exp(s1) - jnp.exp(s2) + lam_init
    for h in range(heads):
        acc = acc_ref[h]
        o = acc[:, :LANES] / acc[:, LANES:]
        y = o[:t] - lam * o[t:]
        ms = jnp.mean(y * y, axis=-1, keepdims=True)
        y = (y * lax.rsqrt(ms + NORM_EPS) * subln_ref[...]) * (1.0 - lam_init)
        o_ref[0, :, h * LANES:(h + 1) * LANES] = y.astype(o_ref.dtype)


def _diff_attention(q, k, v, lam_vecs, subln, layer_idx):
    bsz, s, width = q.shape
    heads = width // LANES
    t, kw = DIFF_T, DIFF_KW
    assert kw in (t, 2 * t)
    lam_init = 0.8 - 0.6 * math.exp(-0.3 * layer_idx)
    kern = functools.partial(_diff_attn_kernel, t=t, kw=kw, heads=heads,
                             slopes=_alibi_slopes(heads), lam_init=lam_init)
    return pl.pallas_call(
        kern,
        grid=(bsz, s // t),
        in_specs=[pl.BlockSpec((1, t, width), lambda b, i: (b, i, 0)),
                  pl.BlockSpec((1, s, width), lambda b, i: (b, 0, 0)),
                  pl.BlockSpec((1, s, width), lambda b, i: (b, 0, 0)),
                  pl.BlockSpec((4, HEAD_DIM), lambda b, i: (0, 0)),
                  pl.BlockSpec((1, LANES), lambda b, i: (0, 0))],
        out_specs=pl.BlockSpec((1, t, width), lambda b, i: (b, i, 0)),
        out_shape=jax.ShapeDtypeStruct((bsz, s, width), BF16),
        scratch_shapes=[pltpu.VMEM((heads, 2 * t, LANES), BF16),
                        pltpu.VMEM((heads, 2 * t, LANES), F32),
                        pltpu.VMEM((heads, 2 * t, 2 * LANES), F32)],
        compiler_params=pltpu.CompilerParams(
            dimension_semantics=("arbitrary", "arbitrary"), vmem_limit_bytes=VMEM_LIMIT),
        name="diff_attn",
    )(q, k, v, lam_vecs, subln.reshape(1, LANES))


def _key_to_float(k):
    return lax.bitcast_convert_type(k ^ ((k >> 31) & jnp.int32(0x7FFFFFFF)), F32)


def _floor_bf16(x):
    r = x.astype(BF16).astype(F32)
    ulp_down = jnp.where(x < 0, jnp.int32(0x10000), jnp.int32(-0x10000))
    down = lax.bitcast_convert_type(lax.bitcast_convert_type(r, jnp.int32) + ulp_down, F32)
    return jnp.where(r > x, down, r).astype(BF16)


def _dsa_kernel(bq_ref, iq_ref, iw_ref, ik_ref, bk_ref, vt_ref, o_ref,
                sc_ref, scb_ref, iqs_ref, bqs_ref, wt_ref, need_ref, m_ref, acc_ref,
                *, tq, tk, heads, topk, slopes):
    i = pl.program_id(1)
    q0 = i * tq
    nch = q0 // tk + 1
    krow = lax.broadcasted_iota(jnp.int32, (tk, tq), 0)
    qcol = lax.broadcasted_iota(jnp.int32, (tk, tq), 1)

    lane = lax.broadcasted_iota(jnp.int32, (tq, LANES), 1)
    for h in range(heads):
        keep = (lane < HEAD_DIM) if h % 2 == 0 else (lane >= HEAD_DIM)
        tile = slice((h // 2) * LANES, (h // 2 + 1) * LANES)
        iqb = iq_ref[0, :, tile]
        bqb = bq_ref[0, :, tile]
        iqs_ref[h] = jnp.where(keep, iqb, jnp.zeros_like(iqb))
        bqs_ref[h * tq:(h + 1) * tq] = jnp.where(keep, bqb, jnp.zeros_like(bqb))
    wt_ref[...] = iw_ref[0].T

    def scores(j, r0, nrows, c_lo, band):
        start = pl.multiple_of(j * tk, tk)
        ikc = ik_ref[0, pl.ds(start + r0, nrows), :]
        sc = None
        for h in range(heads):
            logit = lax.dot_general(ikc, iqs_ref[h, c_lo:, :], _NT,
                                    preferred_element_type=F32)
            term = jnp.maximum(logit, 0.0) * wt_ref[h:h + 1, c_lo:]
            sc = term if sc is None else sc + term
        sc = sc + 0.0
        if band:
            kr = lax.broadcasted_iota(jnp.int32, sc.shape, 0) + (r0 - c_lo)
            qc = lax.broadcasted_iota(jnp.int32, sc.shape, 1)
            sc = jnp.where(kr <= qc, sc, -jnp.inf)
            if c_lo:
                sc = jnp.concatenate([jnp.full((nrows, c_lo), -jnp.inf, F32), sc], axis=1)
        sc_ref[j, r0:r0 + nrows, :] = sc
        scb_ref[j, r0:r0 + nrows, :] = _floor_bf16(sc)

    def score_body(j, carry):
        scores(j, 0, tk, 0, False)
        return carry

    lax.fori_loop(0, nch - 1, score_body, 0)
    for r0 in range(0, tk, QBLK):
        scores(nch - 1, r0, QBLK, r0, True)

    def count_in(ref, rows, pred_fn, thr):
        dt = ref.dtype

        def add_rows(j, accs, r_lo, r_hi, c_lo):
            accs = list(accs)
            for r in range(r_lo // rows, r_hi // rows):
                blk = ref[j, r * rows:(r + 1) * rows, c_lo:]
                hit = jnp.where(pred_fn(blk, thr[:, c_lo:]), jnp.ones(blk.shape, dt),
                                jnp.zeros(blk.shape, dt))
                a = accs[r % NACC]
                accs[r % NACC] = (a + hit if c_lo == 0 else
                                  jnp.concatenate([a[:, :c_lo], a[:, c_lo:] + hit], axis=1))
            return tuple(accs)

        accs = lax.fori_loop(0, nch - 1, lambda j, accs: add_rows(j, accs, 0, tk, 0),
                             tuple(jnp.zeros((rows, tq), dt) for _ in range(NACC)))
        for r0 in range(0, tk, LANES):
            accs = add_rows(nch - 1, accs, r0, r0 + LANES, r0)
        cnt = accs[0].astype(F32)
        for a in accs[1:]:
            cnt = cnt + a.astype(F32)
        return jnp.broadcast_to(jnp.sum(cnt, axis=0, keepdims=True), (SUB, tq))

    def count(pred_fn, thr):
        return count_in(sc_ref, SUB, pred_fn, thr)

    ge = lambda a, b: a >= b
    gt = lambda a, b: a > b

    def count_b(thr_b):
        return count_in(scb_ref, PACK, ge, thr_b)

    def key16_to_bits(k):
        return lax.shift_left(k ^ ((k >> 31) & jnp.int32(0x7FFF)), 16)

    def hi_body(b, carry):
        key, n_ge = carry
        cand = key + lax.shift_left(jnp.int32(1), 15 - b)
        cf = lax.bitcast_convert_type(key16_to_bits(cand), F32)
        cnt = count_b(jnp.concatenate([cf, cf], axis=0).astype(BF16))
        ok = cnt >= float(topk)
        return jnp.where(ok, cand, key), jnp.where(ok, cnt, n_ge)

    key16, n_ge = lax.fori_loop(0, 16, hi_body, (jnp.full((SUB, tq), -2 ** 15, jnp.int32),
                                                 jnp.zeros((SUB, tq), F32)))
    bits = key16_to_bits(key16)
    base = bits ^ ((bits >> 31) & jnp.int32(0x7FFFFFFF))
    base = jnp.where(key16 <= KEY16_NEG_INF, KEY_NEG_INF, base)

    def decode(key):
        return jnp.where(key <= KEY_NEG_INF, -jnp.inf, _key_to_float(key))

    def lo_bits(first, stop, carry):
        def body(b, carry):
            off, n_ge = carry
            step = lax.shift_left(jnp.int32(1), 15 - b)
            cnt = count(ge, _key_to_float(base + off + step))
            ok = cnt >= float(topk)
            return jnp.where(ok, off + step, off), jnp.where(ok, cnt, n_ge)
        return lax.fori_loop(first, stop, body, carry)

    def settled(carry):
        off, n_ge = carry
        n_gt = count(gt, decode(base + off))
        final = (n_ge == float(topk)) | ((n_ge > float(topk)) & (n_gt < float(topk)))
        return jnp.min(jnp.where(final, 1.0, 0.0))

    keep_state = lambda st: st
    state = lo_bits(0, LO_CHECKS[0], (jnp.zeros((SUB, tq), jnp.int32), n_ge))
    done = settled(state)
    state = lax.cond(done > 0.0, keep_state,
                     functools.partial(lo_bits, LO_CHECKS[0], LO_CHECKS[1]), state)
    done = lax.cond(done > 0.0, lambda st: jnp.float32(1.0), settled, state)
    off, n_ge = lax.cond(done > 0.0, keep_state, functools.partial(lo_bits, LO_CHECKS[1], 16), state)
    thr = decode(base + off)
    excess = jnp.where(n_ge > float(topk), 1.0, 0.0)

    @pl.when(jnp.max(excess) > 0.0)
    def _():
        need_ref[...] = float(topk) - count(gt, thr)

    krow_g = lax.broadcasted_iota(jnp.int32, (tk, LANES), 0)
    qcol_g = lax.broadcasted_iota(jnp.int32, (tk, LANES), 1)
    for c0 in range(0, tq, LANES):
        ls = slice(c0, c0 + LANES)
        thr_c = thr[0:1, ls]
        group_ties = jnp.max(excess[:, ls]) > 0.0
        causal_c = ((nch - 1) * tk + krow_g) <= (q0 + c0 + qcol_g)

        @pl.when(jnp.logical_not(group_ties))
        def _():
            def body(j, carry):
                sc_ref[j, :, ls] = jnp.where(sc_ref[j, :, ls] >= thr_c, 0.0, NEG_BIG)
                return carry
            lax.fori_loop(0, nch - 1, body, 0)
            last = nch - 1
            keep = (sc_ref[last, :, ls] >= thr_c) & causal_c
            sc_ref[last, :, ls] = jnp.where(keep, 0.0, NEG_BIG)

        @pl.when(group_ties)
        def _():
            need = need_ref[0:1, ls]
            r2 = lax.broadcasted_iota(jnp.int32, (tk, tk), 0)
            c2 = lax.broadcasted_iota(jnp.int32, (tk, tk), 1)
            lower = jnp.where(c2 <= r2, 1.0, 0.0).astype(BF16)

            def body(j, seen):
                sc = sc_ref[j, :, ls]
                eq = sc == thr_c
                eqf = jnp.where(eq, 1.0, 0.0)
                rank = seen + jnp.dot(lower, eqf.astype(BF16), preferred_element_type=F32)
                keep = (sc > thr_c) | (eq & (rank <= need))
                keep = keep & ((j * tk + krow_g) <= (q0 + c0 + qcol_g))
                sc_ref[j, :, ls] = jnp.where(keep, 0.0, NEG_BIG)
                return seen + jnp.sum(eqf, axis=0, keepdims=True)

            lax.fori_loop(0, nch, body, jnp.zeros((1, LANES), F32))

    m_ref[...] = jnp.full(m_ref.shape, NEG_BIG, F32)
    acc_ref[...] = jnp.zeros(acc_ref.shape, F32)
    krow_t = lax.broadcasted_iota(jnp.int32, (tk, LANES), 0)

    def attend(j, last):
        start = pl.multiple_of(j * tk, tk)
        kc = bk_ref[0, pl.ds(start, tk), :]
        vt = vt_ref[0, j]
        rel = (start - q0 + krow_t).astype(F32)

        def qk(h):
            return lax.dot_general(kc, bqs_ref[h * tq:(h + 1) * tq], _NT,
                                   preferred_element_type=F32)

        logits = {0: qk(0)}
        for h in range(heads):
            if h + 1 < heads:
                logits[h + 1] = qk(h + 1)
            logit = logits.pop(h)
            alibi = _tile_lanes((slopes[h] * LOG2E) * rel, QBLK // LANES)
            for c0 in range(0, tq, QBLK):
                nk = c0 + QBLK if last else tk
                s = logit[:nk, c0:c0 + QBLK] + sc_ref[j, :nk, c0:c0 + QBLK] + alibi[:nk]
                m_prev = m_ref[h, :, c0:c0 + QBLK]
                m_next = jnp.maximum(m_prev, jnp.max(s, axis=0, keepdims=True))
                alpha = jnp.exp2(m_prev - m_next)
                p = jnp.exp2(s - m_next).astype(BF16)
                acc_ref[h, :, c0:c0 + QBLK] = (alpha * acc_ref[h, :, c0:c0 + QBLK]
                                               + jnp.dot(vt[:, :nk], p, preferred_element_type=F32))
                m_ref[h, :, c0:c0 + QBLK] = m_next

    def attn_body(j, carry):
        attend(j, False)
        return carry

    lax.fori_loop(0, nch - 1, attn_body, 0)
    attend(nch - 1, True)

    for g in range(heads // 2):
        a = acc_ref[2 * g]
        b = acc_ref[2 * g + 1]
        pair = jnp.concatenate([a[:HEAD_DIM] / a[HEAD_DIM:HEAD_DIM + 1],
                                b[:HEAD_DIM] / b[HEAD_DIM:HEAD_DIM + 1]], axis=0)
        o_ref[0, :, g * LANES:(g + 1) * LANES] = pair.T.astype(o_ref.dtype)


def _dsa_attention(bq, iq, iwp, ik2, bk2, bv2):
    bsz, s, width = bq.shape
    heads = width // HEAD_DIM
    tq, tk = DSA_TQ, DSA_TK
    assert tq == tk
    topk = min(TOPK_MAX, s // 4)
    nck = s // tk
    kern = functools.partial(_dsa_kernel, tq=tq, tk=tk, heads=heads, topk=topk,
                             slopes=_alibi_slopes(heads))
    vt = bv2[:, :, :HEAD_DIM].reshape(bsz, nck, tk, HEAD_DIM).transpose(0, 1, 3, 2)
    vt = jnp.concatenate([vt, jnp.ones_like(vt)], axis=2)
    qspec = lambda w: pl.BlockSpec((1, tq, w), lambda b, i: (b, i, 0))
    kspec = pl.BlockSpec((1, s, LANES), lambda b, i: (b, 0, 0))
    return pl.pallas_call(
        kern,
        grid=(bsz, s // tq),
        in_specs=[qspec(width), qspec(width), qspec(LANES), kspec, kspec,
                  pl.BlockSpec((1, nck, LANES, tk), lambda b, i: (b, 0, 0, 0))],
        out_specs=pl.BlockSpec((1, tq, width), lambda b, i: (b, i, 0)),
        out_shape=jax.ShapeDtypeStruct((bsz, s, width), BF16),
        scratch_shapes=[pltpu.VMEM((nck, tk, tq), F32),
                        pltpu.VMEM((nck, tk, tq), BF16),
                        pltpu.VMEM((heads, tq, LANES), BF16),
                        pltpu.VMEM((heads * tq, LANES), BF16),
                        pltpu.VMEM((LANES, tq), F32),
                        pltpu.VMEM((SUB, tq), F32),
                        pltpu.VMEM((heads, 1, tq), F32),
                        pltpu.VMEM((heads, LANES, tq), F32)],
        compiler_params=pltpu.CompilerParams(
            dimension_semantics=("arbitrary", "arbitrary"), vmem_limit_bytes=VMEM_LIMIT),
        name="dsa_attn",
    )(bq, iq, iwp, ik2, bk2, vt)


def _stack_heads(x, heads):
    lane = lax.broadcasted_iota(jnp.int32, (x.shape[0], LANES), 1)
    lo = lane < HEAD_DIM
    parts = []
    for h in range(heads):
        blk = x[:, (h // 2) * LANES:(h // 2 + 1) * LANES]
        keep = lo if h % 2 == 0 else jnp.logical_not(lo)
        parts.append(jnp.where(keep, blk, jnp.zeros_like(blk)))
    return jnp.concatenate(parts, axis=0)


def _swa_kernel(slope_ref, sink_ref, q_ref, kp_ref, kc_ref, vp_ref, vc_ref, o_ref, *, tq, group):
    kv = pl.program_id(1)
    i = pl.program_id(2)
    nk = 2 * WINDOW
    kk = jnp.concatenate([kp_ref[0], kc_ref[0]], axis=0)
    vv = jnp.concatenate([vp_ref[0], vc_ref[0]], axis=0)
    r = lax.broadcasted_iota(jnp.int32, (WINDOW, nk), 0)
    c = lax.broadcasted_iota(jnp.int32, (WINDOW, nk), 1)
    dist = WINDOW + r - c
    band = (dist >= 0) & (dist < WINDOW)
    distf = dist.astype(F32)
    lane = lax.broadcasted_iota(jnp.int32, (WINDOW, LANES), 1)
    biases = [jnp.where(band, -(slope_ref[kv * group + g] * LOG2E) * distf, NEG_BIG)
              for g in range(group)]
    for blk in range(tq // WINDOW):
        rows = slice(blk * WINDOW, (blk + 1) * WINDOW)
        qs = _stack_heads(q_ref[0, rows, :], group)
        kb = kk[blk * WINDOW:blk * WINDOW + nk]
        vb = vv[blk * WINDOW:blk * WINDOW + nk]
        logit = lax.dot_general(qs, kb, _NT, preferred_element_type=F32)
        outs = []
        for g in range(group):
            hq = kv * group + g
            s = logit[g * WINDOW:(g + 1) * WINDOW] + biases[g]
            if blk == 0:
                s = jnp.where(c >= jnp.where(i == 0, WINDOW, 0), s, NEG_BIG)
            sink = sink_ref[hq] * LOG2E
            m = jnp.maximum(jnp.max(s, axis=1, keepdims=True), sink)
            p = jnp.exp2(s - m)
            denom = jnp.sum(p, axis=1, keepdims=True) + jnp.exp2(sink - m)
            outs.append(jnp.dot(p.astype(BF16), vb, preferred_element_type=F32) / denom)
        for g in range(group // 2):
            o_ref[0, rows, g * LANES:(g + 1) * LANES] = jnp.where(
                lane < HEAD_DIM, outs[2 * g], outs[2 * g + 1]).astype(o_ref.dtype)


def _swa_attention(q, k2, v2, sinks):
    bsz, s, width = q.shape
    heads = width // HEAD_DIM
    kvh = k2.shape[2] // LANES
    group = heads // kvh
    tq = min(SWA_TQ, s)
    slopes = jnp.asarray(_alibi_slopes(heads), F32)
    r = tq // WINDOW
    prev = lambda b, kv, i: (b, jnp.maximum(i * r - 1, 0), kv)
    cur = lambda b, kv, i: (b, i, kv)
    smem = pl.BlockSpec(memory_space=pltpu.SMEM)
    return pl.pallas_call(
        functools.partial(_swa_kernel, tq=tq, group=group),
        grid=(bsz, kvh, s // tq),
        in_specs=[smem, smem,
                  pl.BlockSpec((1, tq, group * HEAD_DIM), cur),
                  pl.BlockSpec((1, WINDOW, LANES), prev),
                  pl.BlockSpec((1, tq, LANES), cur),
                  pl.BlockSpec((1, WINDOW, LANES), prev),
                  pl.BlockSpec((1, tq, LANES), cur)],
        out_specs=pl.BlockSpec((1, tq, group * HEAD_DIM), cur),
        out_shape=jax.ShapeDtypeStruct((bsz, s, width), BF16),
        compiler_params=pltpu.CompilerParams(
            dimension_semantics=("arbitrary", "arbitrary", "arbitrary"),
            vmem_limit_bytes=VMEM_LIMIT),
        name="swa_attn",
    )(slopes, sinks.astype(F32), q, k2, k2, v2, v2)


def _out_ffn_kernel(*refs, ny):
    x_ref = refs[0]
    y_refs = refs[1:1 + ny]
    wo_ref, g1_ref, n_ref, sc_ref, sh_ref, g2_ref, wg_ref, wu_ref, wd_ref, o_ref = refs[1 + ny:]
    tm = x_ref.shape[1]
    half = tm // 2
    for rows in (slice(0, half), slice(half, tm)):
        mix = None
        r0 = 0
        for y_ref in y_refs:
            w = y_ref.shape[2]
            part = jnp.dot(y_ref[0, rows, :], wo_ref[r0:r0 + w, :], preferred_element_type=F32)
            mix = part if mix is None else mix + part
            r0 += w
        x = x_ref[0, rows, :] + g1_ref[0] * mix
        h = _modulated_norm(x, n_ref[...], sc_ref[0], sh_ref[0]).astype(BF16)
        gate = jnp.dot(h, wg_ref[...], preferred_element_type=F32)
        up = jnp.dot(h, wu_ref[...], preferred_element_type=F32)
        act = (gate * jax.nn.sigmoid(gate) * up).astype(BF16)
        o_ref[0, rows, :] = x + g2_ref[0] * jnp.dot(act, wd_ref[...], preferred_element_type=F32)


def _out_ffn(x, ys, wo, g1, n, sc, sh, g2, wg, wu, wd):
    bsz, s, d = x.shape
    f = wg.shape[1]
    tm = FFN_TM
    const = lambda b, i: (0, 0)
    tok = lambda b, i: (b, i, 0)
    mod = lambda b, i: (b, 0, 0)
    once = pl.Buffered(1)
    return pl.pallas_call(
        functools.partial(_out_ffn_kernel, ny=len(ys)),
        grid=(bsz, s // tm),
        in_specs=[pl.BlockSpec((1, tm, d), tok)]
                 + [pl.BlockSpec((1, tm, y.shape[2]), tok) for y in ys]
                 + [pl.BlockSpec((d, d), const, pipeline_mode=once),
                  pl.BlockSpec((1, 1, d), mod),
                  pl.BlockSpec((1, d), const),
                  pl.BlockSpec((1, 1, d), mod),
                  pl.BlockSpec((1, 1, d), mod),
                  pl.BlockSpec((1, 1, d), mod),
                  pl.BlockSpec((d, f), const, pipeline_mode=once),
                  pl.BlockSpec((d, f), const, pipeline_mode=once),
                  pl.BlockSpec((f, d), const, pipeline_mode=once)],
        out_specs=pl.BlockSpec((1, tm, d), tok),
        out_shape=jax.ShapeDtypeStruct((bsz, s, d), F32),
        compiler_params=pltpu.CompilerParams(
            dimension_semantics=("arbitrary", "arbitrary"), vmem_limit_bytes=VMEM_LIMIT),
        name="out_ffn",
    )(x, *ys, wo, g1, n.reshape(1, d), sc, sh, g2, wg, wu, wd)


CAST_STEPS = 8


def _cast_kernel(*refs, n):
    for k in range(n):
        refs[n + k][...] = refs[k][0].astype(BF16)


def _layer_weights_bf16(l, *ws):
    n = len(ws)
    return pl.pallas_call(
        functools.partial(_cast_kernel, n=n),
        grid=(CAST_STEPS,),
        in_specs=[pl.BlockSpec((1, w.shape[1] // CAST_STEPS, w.shape[2]), lambda i: (l, i, 0))
                  for w in ws],
        out_specs=[pl.BlockSpec((w.shape[1] // CAST_STEPS, w.shape[2]), lambda i: (i, 0))
                   for w in ws],
        out_shape=[jax.ShapeDtypeStruct(w.shape[1:], BF16) for w in ws],
        compiler_params=pltpu.CompilerParams(
            dimension_semantics=("arbitrary",), vmem_limit_bytes=VMEM_LIMIT),
        name="cast_bf16",
    )(*ws)


def _dup(w):
    d, c = w.shape
    w = w.reshape(d, c // HEAD_DIM, 1, HEAD_DIM)
    return jnp.broadcast_to(w, (d, c // HEAD_DIM, 2, HEAD_DIM)).reshape(d, 2 * c)


def _even_layout(w_in, qn_a, kn_a, qn_b, kn_b, d):
    a = d // 2
    sizes = (a, a, a, a, HEAD_DIM, HEAD_DIM, a, HEAD_DIM, a // HEAD_DIM)
    offs = [0]
    for sz in sizes:
        offs.append(offs[-1] + sz)
    aq, ak, av, bq, bk, bv, iq, ik, iw = [w_in[:, offs[n]:offs[n + 1]] for n in range(9)]
    nh = a // HEAD_DIM
    iw_pad = jnp.zeros((d, LANES), w_in.dtype).at[:, :nh].set(iw)
    w = jnp.concatenate([aq, ak, av, bq, iq, _dup(bk), _dup(bv), _dup(ik), iw_pad], axis=1)
    qscale = HEAD_DIM ** -0.5 * LOG2E
    ones = lambda n: jnp.ones((n,), F32)
    gain = jnp.concatenate([
        jnp.tile(qn_a, nh) * qscale, jnp.tile(kn_a, nh), ones(a),
        jnp.tile(qn_b, nh) * qscale, ones(a) * HEAD_DIM ** -0.5,
        jnp.tile(kn_b, 2), ones(LANES), ones(LANES), ones(LANES) * nh ** -0.5])
    groups = (
        (0, a, ((0, a, 0, 0, "norm"),)),
        (a, a, ((0, a, 1, 0, "norm"),)),
        (2 * a, a, ((0, a, 2, 0, "plain"),)),
        (3 * a, a, ((0, a, 3, 0, "norm"),)),
        (4 * a, a, ((0, a, 4, 0, "scale"),)),
        (5 * a, 4 * LANES, ((0, LANES, 5, 0, "norm"), (LANES, LANES, 6, 0, "plain"),
                            (2 * LANES, LANES, 7, 0, "plain"), (3 * LANES, LANES, 8, 0, "scale"))),
    )
    out_defs = ((a, BF16), (a, BF16), (a, BF16), (a, BF16), (a, BF16),
                (LANES, BF16), (LANES, BF16), (LANES, BF16), (LANES, F32))
    return w.astype(BF16), gain.reshape(1, -1), groups, out_defs


def _odd_layout(w_in, qn_c, kn_c, d):
    kvw = d // 4
    q, k, v = w_in[:, :d], w_in[:, d:d + kvw], w_in[:, d + kvw:]
    w = jnp.concatenate([q, _dup(k), _dup(v)], axis=1)
    nh = d // HEAD_DIM
    gain = jnp.concatenate([jnp.tile(qn_c, nh) * (HEAD_DIM ** -0.5 * LOG2E),
                            jnp.tile(kn_c, 2 * kvw // HEAD_DIM), jnp.ones((2 * kvw,), F32)])
    half = d // 2
    groups = (
        (0, half, ((0, half, 0, 0, "norm"),)),
        (half, half, ((0, half, 0, half, "norm"),)),
        (d, 2 * kvw, ((0, 2 * kvw, 1, 0, "norm"),)),
        (d + 2 * kvw, 2 * kvw, ((0, 2 * kvw, 2, 0, "plain"),)),
    )
    out_defs = ((d, BF16), (2 * kvw, BF16), (2 * kvw, BF16))
    return w.astype(BF16), gain.reshape(1, -1), groups, out_defs


def kernel(x, c, ada_w, ada_b, norm_mix, norm_ffn, w_out, ffn_gate, ffn_up, ffn_down,
           w_in_even, qn_a, kn_a, lam_q1, lam_k1, lam_q2, lam_k2, subln_a, qn_b, kn_b,
           w_in_odd, qn_c, kn_c, sinks_c):
    depth, d = norm_mix.shape
    mod = _ada_mod(c, ada_w, ada_b)
    for l in range(depth):
        sh1, sc1, g1, sh2, sc2, g2 = [mod[l, :, None, n * d:(n + 1) * d] for n in range(6)]
        if l % 2 == 0:
            e = l // 2
            w, gain, groups, out_defs = _even_layout(w_in_even[e], qn_a[e], kn_a[e],
                                                     qn_b[e], kn_b[e], d)
            aq, ak, av, bq, iq, bk2, bv2, ik2, iwp = _proj(
                x, norm_mix[l], sc1, sh1, w, gain, groups, out_defs)
            lam_vecs = jnp.stack([lam_q1[e], lam_k1[e], lam_q2[e], lam_k2[e]]).astype(F32)
            ya = _diff_attention(aq, ak, av, lam_vecs, subln_a[e], l)
            yb = _dsa_attention(bq, iq, iwp, ik2, bk2, bv2)
            ys = (ya, yb)
        else:
            o = l // 2
            w, gain, groups, out_defs = _odd_layout(w_in_odd[o], qn_c[o], kn_c[o], d)
            q, k2, v2 = _proj(x, norm_mix[l], sc1, sh1, w, gain, groups, out_defs)
            ys = (_swa_attention(q, k2, v2, sinks_c[o]),)
        wo, wg, wu, wd = _layer_weights_bf16(l, w_out, ffn_gate, ffn_up, ffn_down)
        x = _out_ffn(x, ys, wo, g1, norm_ffn[l], sc2, sh2, g2, wg, wu, wd)
    return x
```

```python
import functools
import math

import jax
import jax.numpy as jnp
from jax import lax
from jax.experimental import pallas as pl
from jax.experimental.pallas import tpu as pltpu

HEAD_DIM = 64
NORM_EPS = 1e-6
TOPK_MAX = 256
WINDOW = 128

LANES = 128
SUB = 8
PACK = 16
FFN_CHUNK = 1024
VMEM_LIMIT = 56 * 1024 * 1024

PROJ_TM = 512
FFN_TM = 512
DIFF_T = 512
DIFF_KW = 512
DSA_TQ = 512
DSA_TK = 512
SWA_TQ = 1024
ROWS = 128
QBLK = 256
NACC = 4
LO_CHECKS = (10, 12)

NEG_BIG = -1e30
LOG2E = 1.4426950408889634
KEY_NEG_INF = -2139095041
KEY16_NEG_INF = -32641

F32 = jnp.float32
BF16 = jnp.bfloat16
_NT = (((1,), (1,)), ((), ()))


def _alibi_slopes(n):
    return [2.0 ** (-8.0 * (i + 1) / n) for i in range(n)]


def _tile_lanes(x, n):
    return x if n == 1 else jnp.concatenate([x] * n, axis=1)


def _ada_kernel(c_ref, w_ref, b_ref, o_ref):
    c = c_ref[...]
    cond = c * jax.nn.sigmoid(c)
    o_ref[0] = jnp.dot(cond, w_ref[0], preferred_element_type=F32,
                       precision=lax.Precision.HIGHEST) + b_ref[0]


def _ada_mod(c, ada_w, ada_b):
    depth, d, n = ada_w.shape
    b = c.shape[0]
    tn = 1536
    c_pad = jnp.zeros((SUB, d), F32).at[:b].set(c)
    out = pl.pallas_call(
        _ada_kernel,
        grid=(depth, n // tn),
        in_specs=[pl.BlockSpec((SUB, d), lambda l, j: (0, 0)),
                  pl.BlockSpec((1, d, tn), lambda l, j: (l, 0, j)),
                  pl.BlockSpec((1, 1, tn), lambda l, j: (l, 0, j))],
        out_specs=pl.BlockSpec((1, SUB, tn), lambda l, j: (l, 0, j)),
        out_shape=jax.ShapeDtypeStruct((depth, SUB, n), F32),
        compiler_params=pltpu.CompilerParams(
            dimension_semantics=("arbitrary", "arbitrary"), vmem_limit_bytes=VMEM_LIMIT),
        name="ada_mod",
    )(c_pad, ada_w, ada_b.reshape(depth, 1, n))
    return out[:, :b]


def _modulated_norm(x, g, sc, sh):
    ms = jnp.mean(x * x, axis=-1, keepdims=True)
    return (x * lax.rsqrt(ms + NORM_EPS) * g) * (1.0 + sc) + sh


def _proj_kernel(x_ref, g_ref, sc_ref, sh_ref, w_ref, bd_ref, gain_ref, *out_refs, groups):
    tm = x_ref.shape[1]
    half = tm // 2
    bd = bd_ref[...]
    nb = bd.shape[0]
    for rows in (slice(0, half), slice(half, tm)):
        h = _modulated_norm(x_ref[0, rows, :], g_ref[...], sc_ref[0], sh_ref[0]).astype(BF16)
        for (c0, width, parts) in groups:
            y = jnp.dot(h, w_ref[:, c0:c0 + width], preferred_element_type=F32)
            for (p0, pw, oi, o0, mode) in parts:
                step = min(nb, pw)
                for s in range(0, pw, step):
                    ys = y[:, p0 + s:p0 + s + step]
                    col = c0 + p0 + s
                    if mode == "norm":
                        ss = jnp.dot((ys * ys).astype(BF16), bd[:step, :step],
                                     preferred_element_type=F32)
                        ys = ys * lax.rsqrt(ss * (1.0 / HEAD_DIM) + NORM_EPS)
                    if mode != "plain":
                        ys = ys * gain_ref[:, col:col + step]
                    out_refs[oi][0, rows, o0 + s:o0 + s + step] = ys.astype(out_refs[oi].dtype)


def _proj(x, g, sc, sh, w, gain, groups, out_defs):
    bsz, s, d = x.shape
    c = w.shape[1]
    tm = PROJ_TM
    nb = 2 * LANES
    r = lax.broadcasted_iota(jnp.int32, (nb, nb), 0) // HEAD_DIM
    cc = lax.broadcasted_iota(jnp.int32, (nb, nb), 1) // HEAD_DIM
    bd = (r == cc).astype(BF16)
    const = lambda b, i: (0, 0)
    return pl.pallas_call(
        functools.partial(_proj_kernel, groups=groups),
        grid=(bsz, s // tm),
        in_specs=[pl.BlockSpec((1, tm, d), lambda b, i: (b, i, 0)),
                  pl.BlockSpec((1, d), const),
                  pl.BlockSpec((1, 1, d), lambda b, i: (b, 0, 0)),
                  pl.BlockSpec((1, 1, d), lambda b, i: (b, 0, 0)),
                  pl.BlockSpec((d, c), const),
                  pl.BlockSpec((nb, nb), const),
                  pl.BlockSpec((1, c), const)],
        out_specs=[pl.BlockSpec((1, tm, wd), lambda b, i: (b, i, 0)) for wd, _ in out_defs],
        out_shape=[jax.ShapeDtypeStruct((bsz, s, wd), dt) for wd, dt in out_defs],
        compiler_params=pltpu.CompilerParams(
            dimension_semantics=("arbitrary", "arbitrary"), vmem_limit_bytes=VMEM_LIMIT),
        name="in_proj",
    )(x, g.reshape(1, d), sc, sh, w, bd, gain)


def _diff_attn_kernel(q_ref, k_ref, v_ref, lam_ref, subln_ref, o_ref,
                      qs_ref, m_ref, acc_ref, *, t, kw, heads, slopes, lam_init):
    i = pl.program_id(1)
    q0 = i * t
    lane = lax.broadcasted_iota(jnp.int32, (t, LANES), 1)
    for h in range(heads):
        q = q_ref[0, :, h * LANES:(h + 1) * LANES]
        zero = jnp.zeros_like(q)
        qs_ref[h, :t] = jnp.where(lane < HEAD_DIM, q, zero)
        qs_ref[h, t:] = jnp.where(lane >= HEAD_DIM, q, zero)
    m_ref[...] = jnp.full(m_ref.shape, NEG_BIG, F32)
    acc_ref[...] = jnp.zeros(acc_ref.shape, F32)

    def step(start, width, masked):
        col = lax.broadcasted_iota(jnp.int32, (1, width), 1)
        rel = (start - q0 + col).astype(F32)
        nsl = width // LANES
        logits, values = [], []
        for h in range(heads):
            kc = k_ref[0, pl.ds(start, width), h * LANES:(h + 1) * LANES]
            vc = v_ref[0, pl.ds(start, width), h * LANES:(h + 1) * LANES]
            logits.append(lax.dot_general(qs_ref[h], kc, _NT, preferred_element_type=F32))
            values.append(jnp.concatenate([vc, jnp.ones_like(vc)], axis=1))
        for h in range(heads):
            alibi = (slopes[h] * LOG2E) * rel
            for r0 in range(0, 2 * t, ROWS):
                s = logits[h][r0:r0 + ROWS] + alibi
                if masked:
                    r = lax.broadcasted_iota(jnp.int32, (ROWS, width), 0) + (r0 % t)
                    c = lax.broadcasted_iota(jnp.int32, (ROWS, width), 1)
                    s = jnp.where(r >= c, s, NEG_BIG)
                m_prev = m_ref[h, r0:r0 + ROWS]
                m_next = jnp.maximum(m_prev, jnp.max(s, axis=1, keepdims=True))
                alpha = jnp.exp2(m_prev - m_next)
                p = jnp.exp2(s - _tile_lanes(m_next, nsl))
                acc_ref[h, r0:r0 + ROWS] = (
                    _tile_lanes(alpha, 2) * acc_ref[h, r0:r0 + ROWS]
                    + jnp.dot(p.astype(BF16), values[h], preferred_element_type=F32))
                m_ref[h, r0:r0 + ROWS] = m_next

    def body(j, carry):
        step(pl.multiple_of(j * kw, kw), kw, False)
        return carry

    nfull = q0 // kw
    lax.fori_loop(0, nfull, body, 0)

    if t < kw:
        @pl.when(q0 - nfull * kw > 0)
        def _():
            step(pl.multiple_of(q0 - t, t), t, False)

    step(pl.multiple_of(q0, t), t, True)

    lam_v = lam_ref[...]
    s1 = jnp.sum(lam_v[0:1] * lam_v[1:2], axis=-1, keepdims=True)
    s2 = jnp.sum(lam_v[2:3] * lam_v[3:4], axis=-1, keepdims=True)
    lam = jnp.exp(s1) - jnp.exp(s2) + lam_init
    for h in range(heads):
        acc = acc_ref[h]
        o = acc[:, :LANES] / acc[:, LANES:]
        y = o[:t] - lam * o[t:]
        ms = jnp.mean(y * y, axis=-1, keepdims=True)
        y = (y * lax.rsqrt(ms + NORM_EPS) * subln_ref[...]) * (1.0 - lam_init)
        o_ref[0, :, h * LANES:(h + 1) * LANES] = y.astype(o_ref.dtype)


def _diff_attention(q, k, v, lam_vecs, subln, layer_idx):
    bsz, s, width = q.shape
    heads = width // LANES
    t, kw = DIFF_T, DIFF_KW
    assert kw in (t, 2 * t)
    lam_init = 0.8 - 0.6 * math.exp(-0.3 * layer_idx)
    kern = functools.partial(_diff_attn_kernel, t=t, kw=kw, heads=heads,
                             slopes=_alibi_slopes(heads), lam_init=lam_init)
    return pl.pallas_call(
        kern,
        grid=(bsz, s // t),
        in_specs=[pl.BlockSpec((1, t, width), lambda b, i: (b, i, 0)),
                  pl.BlockSpec((1, s, width), lambda b, i: (b, 0, 0)),
                  pl.BlockSpec((1, s, width), lambda b, i: (b, 0, 0)),
                  pl.BlockSpec((4, HEAD_DIM), lambda b, i: (0, 0)),
                  pl.BlockSpec((1, LANES), lambda b, i: (0, 0))],
        out_specs=pl.BlockSpec((1, t, width), lambda b, i: (b, i, 0)),
        out_shape=jax.ShapeDtypeStruct((bsz, s, width), BF16),
        scratch_shapes=[pltpu.VMEM((heads, 2 * t, LANES), BF16),
                        pltpu.VMEM((heads, 2 * t, LANES), F32),
                        pltpu.VMEM((heads, 2 * t, 2 * LANES), F32)],
        compiler_params=pltpu.CompilerParams(
            dimension_semantics=("arbitrary", "arbitrary"), vmem_limit_bytes=VMEM_LIMIT),
        name="diff_attn",
    )(q, k, v, lam_vecs, subln.reshape(1, LANES))


def _key_to_float(k):
    return lax.bitcast_convert_type(k ^ ((k >> 31) & jnp.int32(0x7FFFFFFF)), F32)


def _floor_bf16(x):
    r = x.astype(BF16).astype(F32)
    ulp_down = jnp.where(x < 0, jnp.int32(0x10000), jnp.int32(-0x10000))
    down = lax.bitcast_convert_type(lax.bitcast_convert_type(r, jnp.int32) + ulp_down, F32)
    return jnp.where(r > x, down, r).astype(BF16)


def _dsa_kernel(bq_ref, iq_ref, iw_ref, ik_ref, bk_ref, vt_ref, o_ref,
                sc_ref, scb_ref, iqs_ref, bqs_ref, wt_ref, need_ref, m_ref, acc_ref,
                *, tq, tk, heads, topk, slopes):
    i = pl.program_id(1)
    q0 = i * tq
    nch = q0 // tk + 1
    krow = lax.broadcasted_iota(jnp.int32, (tk, tq), 0)
    qcol = lax.broadcasted_iota(jnp.int32, (tk, tq), 1)

    lane = lax.broadcasted_iota(jnp.int32, (tq, LANES), 1)
    for h in range(heads):
        keep = (lane < HEAD_DIM) if h % 2 == 0 else (lane >= HEAD_DIM)
        tile = slice((h // 2) * LANES, (h // 2 + 1) * LANES)
        iqb = iq_ref[0, :, tile]
        bqb = bq_ref[0, :, tile]
        iqs_ref[h] = jnp.where(keep, iqb, jnp.zeros_like(iqb))
        bqs_ref[h * tq:(h + 1) * tq] = jnp.where(keep, bqb, jnp.zeros_like(bqb))
    wt_ref[...] = iw_ref[0].T

    def scores(j, r0, nrows, c_lo, band):
        start = pl.multiple_of(j * tk, tk)
        ikc = ik_ref[0, pl.ds(start + r0, nrows), :]
        sc = None
        for h in range(heads):
            logit = lax.dot_general(ikc, iqs_ref[h, c_lo:, :], _NT,
                                    preferred_element_type=F32)
            term = jnp.maximum(logit, 0.0) * wt_ref[h:h + 1, c_lo:]
            sc = term if sc is None else sc + term
        sc = sc + 0.0
        if band:
            kr = lax.broadcasted_iota(jnp.int32, sc.shape, 0) + (r0 - c_lo)
            qc = lax.broadcasted_iota(jnp.int32, sc.shape, 1)
            sc = jnp.where(kr <= qc, sc, -jnp.inf)
            if c_lo:
                sc = jnp.concatenate([jnp.full((nrows, c_lo), -jnp.inf, F32), sc], axis=1)
        sc_ref[j, r0:r0 + nrows, :] = sc
        scb_ref[j, r0:r0 + nrows, :] = _floor_bf16(sc)

    def score_body(j, carry):
        scores(j, 0, tk, 0, False)
        return carry

    lax.fori_loop(0, nch - 1, score_body, 0)
    for r0 in range(0, tk, QBLK):
        scores(nch - 1, r0, QBLK, r0, True)

    def count_in(ref, rows, pred_fn, thr):
        dt = ref.dtype

        def add_rows(j, accs, r_lo, r_hi, c_lo):
            accs = list(accs)
            for r in range(r_lo // rows, r_hi // rows):
                blk = ref[j, r * rows:(r + 1) * rows, c_lo:]
                hit = jnp.where(pred_fn(blk, thr[:, c_lo:]), jnp.ones(blk.shape, dt),
                                jnp.zeros(blk.shape, dt))
                a = accs[r % NACC]
                accs[r % NACC] = (a + hit if c_lo == 0 else
                                  jnp.concatenate([a[:, :c_lo], a[:, c_lo:] + hit], axis=1))
            return tuple(accs)

        accs = lax.fori_loop(0, nch - 1, lambda j, accs: add_rows(j, accs, 0, tk, 0),
                             tuple(jnp.zeros((rows, tq), dt) for _ in range(NACC)))
        for r0 in range(0, tk, LANES):
            accs = add_rows(nch - 1, accs, r0, r0 + LANES, r0)
        cnt = accs[0].astype(F32)
        for a in accs[1:]:
            cnt = cnt + a.astype(F32)
        return jnp.broadcast_to(jnp.sum(cnt, axis=0, keepdims=True), (SUB, tq))

    def count(pred_fn, thr):
        return count_in(sc_ref, SUB, pred_fn, thr)

    ge = lambda a, b: a >= b
    gt = lambda a, b: a > b

    def count_b(thr_b):
        return count_in(scb_ref, PACK, ge, thr_b)

    def key16_to_bits(k):
        return lax.shift_left(k ^ ((k >> 31) & jnp.int32(0x7FFF)), 16)

    def hi_body(b, carry):
        key, n_ge = carry
        cand = key + lax.shift_left(jnp.int32(1), 15 - b)
        cf = lax.bitcast_convert_type(key16_to_bits(cand), F32)
        cnt = count_b(jnp.concatenate([cf, cf], axis=0).astype(BF16))
        ok = cnt >= float(topk)
        return jnp.where(ok, cand, key), jnp.where(ok, cnt, n_ge)

    key16, n_ge = lax.fori_loop(0, 16, hi_body, (jnp.full((SUB, tq), -2 ** 15, jnp.int32),
                                                 jnp.zeros((SUB, tq), F32)))
    bits = key16_to_bits(key16)
    base = bits ^ ((bits >> 31) & jnp.int32(0x7FFFFFFF))
    base = jnp.where(key16 <= KEY16_NEG_INF, KEY_NEG_INF, base)

    def decode(key):
        return jnp.where(key <= KEY_NEG_INF, -jnp.inf, _key_to_float(key))

    def lo_bits(first, stop, carry):
        def body(b, carry):
            off, n_ge = carry
            step = lax.shift_left(jnp.int32(1), 15 - b)
            cnt = count(ge, _key_to_float(base + off + step))
            ok = cnt >= float(topk)
            return jnp.where(ok, off + step, off), jnp.where(ok, cnt, n_ge)
        return lax.fori_loop(first, stop, body, carry)

    def settled(carry):
        off, n_ge = carry
        n_gt = count(gt, decode(base + off))
        final = (n_ge == float(topk)) | ((n_ge > float(topk)) & (n_gt < float(topk)))
        return jnp.min(jnp.where(final, 1.0, 0.0))

    keep_state = lambda st: st
    state = lo_bits(0, LO_CHECKS[0], (jnp.zeros((SUB, tq), jnp.int32), n_ge))
    done = settled(state)
    state = lax.cond(done > 0.0, keep_state,
                     functools.partial(lo_bits, LO_CHECKS[0], LO_CHECKS[1]), state)
    done = lax.cond(done > 0.0, lambda st: jnp.float32(1.0), settled, state)
    off, n_ge = lax.cond(done > 0.0, keep_state, functools.partial(lo_bits, LO_CHECKS[1], 16), state)
    thr = decode(base + off)
    excess = jnp.where(n_ge > float(topk), 1.0, 0.0)

    @pl.when(jnp.max(excess) > 0.0)
    def _():
        need_ref[...] = float(topk) - count(gt, thr)

    krow_g = lax.broadcasted_iota(jnp.int32, (tk, LANES), 0)
    qcol_g = lax.broadcasted_iota(jnp.int32, (tk, LANES), 1)
    for c0 in range(0, tq, LANES):
        ls = slice(c0, c0 + LANES)
        thr_c = thr[0:1, ls]
        group_ties = jnp.max(excess[:, ls]) > 0.0
        causal_c = ((nch - 1) * tk + krow_g) <= (q0 + c0 + qcol_g)

        @pl.when(jnp.logical_not(group_ties))
        def _():
            def body(j, carry):
                sc_ref[j, :, ls] = jnp.where(sc_ref[j, :, ls] >= thr_c, 0.0, NEG_BIG)
                return carry
            lax.fori_loop(0, nch - 1, body, 0)
            last = nch - 1
            keep = (sc_ref[last, :, ls] >= thr_c) & causal_c
            sc_ref[last, :, ls] = jnp.where(keep, 0.0, NEG_BIG)

        @pl.when(group_ties)
        def _():
            need = need_ref[0:1, ls]
            r2 = lax.broadcasted_iota(jnp.int32, (tk, tk), 0)
            c2 = lax.broadcasted_iota(jnp.int32, (tk, tk), 1)
            lower = jnp.where(c2 <= r2, 1.0, 0.0).astype(BF16)

            def body(j, seen):
                sc = sc_ref[j, :, ls]
                eq = sc == thr_c
                eqf = jnp.where(eq, 1.0, 0.0)
                rank = seen + jnp.dot(lower, eqf.astype(BF16), preferred_element_type=F32)
                keep = (sc > thr_c) | (eq & (rank <= need))
                keep = keep & ((j * tk + krow_g) <= (q0 + c0 + qcol_g))
                sc_ref[j, :, ls] = jnp.where(keep, 0.0, NEG_BIG)
                return seen + jnp.sum(eqf, axis=0, keepdims=True)

            lax.fori_loop(0, nch, body, jnp.zeros((1, LANES), F32))

    m_ref[...] = jnp.full(m_ref.shape, NEG_BIG, F32)
    acc_ref[...] = jnp.zeros(acc_ref.shape, F32)
    krow_t = lax.broadcasted_iota(jnp.int32, (tk, LANES), 0)

    def attend(j, last):
        start = pl.multiple_of(j * tk, tk)
        kc = bk_ref[0, pl.ds(start, tk), :]
        vt = vt_ref[0, j]
        rel = (start - q0 + krow_t).astype(F32)

        def qk(h):
            return lax.dot_general(kc, bqs_ref[h * tq:(h + 1) * tq], _NT,
                                   preferred_element_type=F32)

        logits = {0: qk(0)}
        for h in range(heads):
            if h + 1 < heads:
                logits[h + 1] = qk(h + 1)
            logit = logits.pop(h)
            alibi = _tile_lanes((slopes[h] * LOG2E) * rel, QBLK // LANES)
            for c0 in range(0, tq, QBLK):
                nk = c0 + QBLK if last else tk
                s = logit[:nk, c0:c0 + QBLK] + sc_ref[j, :nk, c0:c0 + QBLK] + alibi[:nk]
                m_prev = m_ref[h, :, c0:c0 + QBLK]
                m_next = jnp.maximum(m_prev, jnp.max(s, axis=0, keepdims=True))
                alpha = jnp.exp2(m_prev - m_next)
                p = jnp.exp2(s - m_next).astype(BF16)
                acc_ref[h, :, c0:c0 + QBLK] = (alpha * acc_ref[h, :, c0:c0 + QBLK]
                                               + jnp.dot(vt[:, :nk], p, preferred_element_type=F32))
                m_ref[h, :, c0:c0 + QBLK] = m_next

    def attn_body(j, carry):
        attend(j, False)
        return carry

    lax.fori_loop(0, nch - 1, attn_body, 0)
    attend(nch - 1, True)

    for g in range(heads // 2):
        a = acc_ref[2 * g]
        b = acc_ref[2 * g + 1]
        pair = jnp.concatenate([a[:HEAD_DIM] / a[HEAD_DIM:HEAD_DIM + 1],
                                b[:HEAD_DIM] / b[HEAD_DIM:HEAD_DIM + 1]], axis=0)
        o_ref[0, :, g * LANES:(g + 1) * LANES] = pair.T.astype(o_ref.dtype)


def _dsa_attention(bq, iq, iwp, ik2, bk2, bv2):
    bsz, s, width = bq.shape
    heads = width // HEAD_DIM
    tq, tk = DSA_TQ, DSA_TK
    assert tq == tk
    topk = min(TOPK_MAX, s // 4)
    nck = s // tk
    kern = functools.partial(_dsa_kernel, tq=tq, tk=tk, heads=heads, topk=topk,
                             slopes=_alibi_slopes(heads))
    vt = bv2[:, :, :HEAD_DIM].reshape(bsz, nck, tk, HEAD_DIM).transpose(0, 1, 3, 2)
    vt = jnp.concatenate([vt, jnp.ones_like(vt)], axis=2)
    qspec = lambda w: pl.BlockSpec((1, tq, w), lambda b, i: (b, i, 0))
    kspec = pl.BlockSpec((1, s, LANES), lambda b, i: (b, 0, 0))
    return pl.pallas_call(
        kern,
        grid=(bsz, s // tq),
        in_specs=[qspec(width), qspec(width), qspec(LANES), kspec, kspec,
                  pl.BlockSpec((1, nck, LANES, tk), lambda b, i: (b, 0, 0, 0))],
        out_specs=pl.BlockSpec((1, tq, width), lambda b, i: (b, i, 0)),
        out_shape=jax.ShapeDtypeStruct((bsz, s, width), BF16),
        scratch_shapes=[pltpu.VMEM((nck, tk, tq), F32),
                        pltpu.VMEM((nck, tk, tq), BF16),
                        pltpu.VMEM((heads, tq, LANES), BF16),
                        pltpu.VMEM((heads * tq, LANES), BF16),
                        pltpu.VMEM((LANES, tq), F32),
                        pltpu.VMEM((SUB, tq), F32),
                        pltpu.VMEM((heads, 1, tq), F32),
                        pltpu.VMEM((heads, LANES, tq), F32)],
        compiler_params=pltpu.CompilerParams(
            dimension_semantics=("arbitrary", "arbitrary"), vmem_limit_bytes=VMEM_LIMIT),
        name="dsa_attn",
    )(bq, iq, iwp, ik2, bk2, vt)


def _stack_heads(x, heads):
    lane = lax.broadcasted_iota(jnp.int32, (x.shape[0], LANES), 1)
    lo = lane < HEAD_DIM
    parts = []
    for h in range(heads):
        blk = x[:, (h // 2) * LANES:(h // 2 + 1) * LANES]
        keep = lo if h % 2 == 0 else jnp.logical_not(lo)
        parts.append(jnp.where(keep, blk, jnp.zeros_like(blk)))
    return jnp.concatenate(parts, axis=0)


def _swa_kernel(slope_ref, sink_ref, q_ref, kp_ref, kc_ref, vp_ref, vc_ref, o_ref, *, tq, group):
    kv = pl.program_id(1)
    i = pl.program_id(2)
    nk = 2 * WINDOW
    kk = jnp.concatenate([kp_ref[0], kc_ref[0]], axis=0)
    vv = jnp.concatenate([vp_ref[0], vc_ref[0]], axis=0)
    r = lax.broadcasted_iota(jnp.int32, (WINDOW, nk), 0)
    c = lax.broadcasted_iota(jnp.int32, (WINDOW, nk), 1)
    dist = WINDOW + r - c
    band = (dist >= 0) & (dist < WINDOW)
    distf = dist.astype(F32)
    lane = lax.broadcasted_iota(jnp.int32, (WINDOW, LANES), 1)
    biases = [jnp.where(band, -(slope_ref[kv * group + g] * LOG2E) * distf, NEG_BIG)
              for g in range(group)]
    for blk in range(tq // WINDOW):
        rows = slice(blk * WINDOW, (blk + 1) * WINDOW)
        qs = _stack_heads(q_ref[0, rows, :], group)
        kb = kk[blk * WINDOW:blk * WINDOW + nk]
        vb = vv[blk * WINDOW:blk * WINDOW + nk]
        logit = lax.dot_general(qs, kb, _NT, preferred_element_type=F32)
        outs = []
        for g in range(group):
            hq = kv * group + g
            s = logit[g * WINDOW:(g + 1) * WINDOW] + biases[g]
            if blk == 0:
                s = jnp.where(c >= jnp.where(i == 0, WINDOW, 0), s, NEG_BIG)
            sink = sink_ref[hq] * LOG2E
            m = jnp.maximum(jnp.max(s, axis=1, keepdims=True), sink)
            p = jnp.exp2(s - m)
            denom = jnp.sum(p, axis=1, keepdims=True) + jnp.exp2(sink - m)
            outs.append(jnp.dot(p.astype(BF16), vb, preferred_element_type=F32) / denom)
        for g in range(group // 2):
            o_ref[0, rows, g * LANES:(g + 1) * LANES] = jnp.where(
                lane < HEAD_DIM, outs[2 * g], outs[2 * g + 1]).astype(o_ref.dtype)


def _swa_attention(q, k2, v2, sinks):
    bsz, s, width = q.shape
    heads = width // HEAD_DIM
    kvh = k2.shape[2] // LANES
    group = heads // kvh
    tq = min(SWA_TQ, s)
    slopes = jnp.asarray(_alibi_slopes(heads), F32)
    r = tq // WINDOW
    prev = lambda b, kv, i: (b, jnp.maximum(i * r - 1, 0), kv)
    cur = lambda b, kv, i: (b, i, kv)
    smem = pl.BlockSpec(memory_space=pltpu.SMEM)
    return pl.pallas_call(
        functools.partial(_swa_kernel, tq=tq, group=group),
        grid=(bsz, kvh, s // tq),
        in_specs=[smem, smem,
                  pl.BlockSpec((1, tq, group * HEAD_DIM), cur),
                  pl.BlockSpec((1, WINDOW, LANES), prev),
                  pl.BlockSpec((1, tq, LANES), cur),
                  pl.BlockSpec((1, WINDOW, LANES), prev),
                  pl.BlockSpec((1, tq, LANES), cur)],
        out_specs=pl.BlockSpec((1, tq, group * HEAD_DIM), cur),
        out_shape=jax.ShapeDtypeStruct((bsz, s, width), BF16),
        compiler_params=pltpu.CompilerParams(
            dimension_semantics=("arbitrary", "arbitrary", "arbitrary"),
            vmem_limit_bytes=VMEM_LIMIT),
        name="swa_attn",
    )(slopes, sinks.astype(F32), q, k2, k2, v2, v2)


def _out_ffn_kernel(*refs, ny):
    x_ref = refs[0]
    y_refs = refs[1:1 + ny]
    wo_ref, g1_ref, n_ref, sc_ref, sh_ref, g2_ref, wg_ref, wu_ref, wd_ref, o_ref = refs[1 + ny:]
    tm = x_ref.shape[1]
    half = tm // 2
    for rows in (slice(0, half), slice(half, tm)):
        mix = None
        r0 = 0
        for y_ref in y_refs:
            w = y_ref.shape[2]
            part = jnp.dot(y_ref[0, rows, :], wo_ref[r0:r0 + w, :], preferred_element_type=F32)
            mix = part if mix is None else mix + part
            r0 += w
        x = x_ref[0, rows, :] + g1_ref[0] * mix
        h = _modulated_norm(x, n_ref[...], sc_ref[0], sh_ref[0]).astype(BF16)
        f = wg_ref.shape[1]
        ffn = None
        for lo in range(0, f, FFN_CHUNK):
            hi = min(lo + FFN_CHUNK, f)
            gate = jnp.dot(h, wg_ref[:, lo:hi], preferred_element_type=F32)
            up = jnp.dot(h, wu_ref[:, lo:hi], preferred_element_type=F32)
            act = (gate * jax.nn.sigmoid(gate) * up).astype(BF16)
            part = jnp.dot(act, wd_ref[lo:hi, :], preferred_element_type=F32)
            ffn = part if ffn is None else ffn + part
        o_ref[0, rows, :] = x + g2_ref[0] * ffn


def _out_ffn(x, ys, wo, g1, n, sc, sh, g2, wg, wu, wd):
    bsz, s, d = x.shape
    f = wg.shape[1]
    tm = FFN_TM
    const = lambda b, i: (0, 0)
    tok = lambda b, i: (b, i, 0)
    mod = lambda b, i: (b, 0, 0)
    once = pl.Buffered(1)
    return pl.pallas_call(
        functools.partial(_out_ffn_kernel, ny=len(ys)),
        grid=(bsz, s // tm),
        in_specs=[pl.BlockSpec((1, tm, d), tok)]
                 + [pl.BlockSpec((1, tm, y.shape[2]), tok) for y in ys]
                 + [pl.BlockSpec((d, d), const, pipeline_mode=once),
                  pl.BlockSpec((1, 1, d), mod),
                  pl.BlockSpec((1, d), const),
                  pl.BlockSpec((1, 1, d), mod),
                  pl.BlockSpec((1, 1, d), mod),
                  pl.BlockSpec((1, 1, d), mod),
                  pl.BlockSpec((d, f), const, pipeline_mode=once),
                  pl.BlockSpec((d, f), const, pipeline_mode=once),
                  pl.BlockSpec((f, d), const, pipeline_mode=once)],
        out_specs=pl.BlockSpec((1, tm, d), tok),
        out_shape=jax.ShapeDtypeStruct((bsz, s, d), F32),
        compiler_params=pltpu.CompilerParams(
            dimension_semantics=("arbitrary", "arbitrary"), vmem_limit_bytes=VMEM_LIMIT),
        name="out_ffn",
    )(x, *ys, wo, g1, n.reshape(1, d), sc, sh, g2, wg, wu, wd)


CAST_STEPS = 8


def _cast_kernel(*refs, n):
    for k in range(n):
        refs[n + k][...] = refs[k][0].astype(BF16)


def _layer_weights_bf16(l, *ws):
    n = len(ws)
    return pl.pallas_call(
        functools.partial(_cast_kernel, n=n),
        grid=(CAST_STEPS,),
        in_specs=[pl.BlockSpec((1, w.shape[1] // CAST_STEPS, w.shape[2]), lambda i: (l, i, 0))
                  for w in ws],
        out_specs=[pl.BlockSpec((w.shape[1] // CAST_STEPS, w.shape[2]), lambda i: (i, 0))
                   for w in ws],
        out_shape=[jax.ShapeDtypeStruct(w.shape[1:], BF16) for w in ws],
        compiler_params=pltpu.CompilerParams(
            dimension_semantics=("arbitrary",), vmem_limit_bytes=VMEM_LIMIT),
        name="cast_bf16",
    )(*ws)


def _dup(w):
    d, c = w.shape
    w = w.reshape(d, c // HEAD_DIM, 1, HEAD_DIM)
    return jnp.broadcast_to(w, (d, c // HEAD_DIM, 2, HEAD_DIM)).reshape(d, 2 * c)


def _even_layout(w_in, qn_a, kn_a, qn_b, kn_b, d):
    a = d // 2
    sizes = (a, a, a, a, HEAD_DIM, HEAD_DIM, a, HEAD_DIM, a // HEAD_DIM)
    offs = [0]
    for sz in sizes:
        offs.append(offs[-1] + sz)
    aq, ak, av, bq, bk, bv, iq, ik, iw = [w_in[:, offs[n]:offs[n + 1]] for n in range(9)]
    nh = a // HEAD_DIM
    iw_pad = jnp.zeros((d, LANES), w_in.dtype).at[:, :nh].set(iw)
    w = jnp.concatenate([aq, ak, av, bq, iq, _dup(bk), _dup(bv), _dup(ik), iw_pad], axis=1)
    qscale = HEAD_DIM ** -0.5 * LOG2E
    ones = lambda n: jnp.ones((n,), F32)
    gain = jnp.concatenate([
        jnp.tile(qn_a, nh) * qscale, jnp.tile(kn_a, nh), ones(a),
        jnp.tile(qn_b, nh) * qscale, ones(a) * HEAD_DIM ** -0.5,
        jnp.tile(kn_b, 2), ones(LANES), ones(LANES), ones(LANES) * nh ** -0.5])
    groups = (
        (0, a, ((0, a, 0, 0, "norm"),)),
        (a, a, ((0, a, 1, 0, "norm"),)),
        (2 * a, a, ((0, a, 2, 0, "plain"),)),
        (3 * a, a, ((0, a, 3, 0, "norm"),)),
        (4 * a, a, ((0, a, 4, 0, "scale"),)),
        (5 * a, 4 * LANES, ((0, LANES, 5, 0, "norm"), (LANES, LANES, 6, 0, "plain"),
                            (2 * LANES, LANES, 7, 0, "plain"), (3 * LANES, LANES, 8, 0, "scale"))),
    )
    out_defs = ((a, BF16), (a, BF16), (a, BF16), (a, BF16), (a, BF16),
                (LANES, BF16), (LANES, BF16), (LANES, BF16), (LANES, F32))
    return w.astype(BF16), gain.reshape(1, -1), groups, out_defs


def _odd_layout(w_in, qn_c, kn_c, d):
    kvw = d // 4
    q, k, v = w_in[:, :d], w_in[:, d:d + kvw], w_in[:, d + kvw:]
    w = jnp.concatenate([q, _dup(k), _dup(v)], axis=1)
    nh = d // HEAD_DIM
    gain = jnp.concatenate([jnp.tile(qn_c, nh) * (HEAD_DIM ** -0.5 * LOG2E),
                            jnp.tile(kn_c, 2 * kvw // HEAD_DIM), jnp.ones((2 * kvw,), F32)])
    half = d // 2
    groups = (
        (0, half, ((0, half, 0, 0, "norm"),)),
        (half, half, ((0, half, 0, half, "norm"),)),
        (d, 2 * kvw, ((0, 2 * kvw, 1, 0, "norm"),)),
        (d + 2 * kvw, 2 * kvw, ((0, 2 * kvw, 2, 0, "plain"),)),
    )
    out_defs = ((d, BF16), (2 * kvw, BF16), (2 * kvw, BF16))
    return w.astype(BF16), gain.reshape(1, -1), groups, out_defs


def kernel(x, c, ada_w, ada_b, norm_mix, norm_ffn, w_out, ffn_gate, ffn_up, ffn_down,
           w_in_even, qn_a, kn_a, lam_q1, lam_k1, lam_q2, lam_k2, subln_a, qn_b, kn_b,
           w_in_odd, qn_c, kn_c, sinks_c):
    depth, d = norm_mix.shape
    mod = _ada_mod(c, ada_w, ada_b)
    for l in range(depth):
        sh1, sc1, g1, sh2, sc2, g2 = [mod[l, :, None, n * d:(n + 1) * d] for n in range(6)]
        if l % 2 == 0:
            e = l // 2
            w, gain, groups, out_defs = _even_layout(w_in_even[e], qn_a[e], kn_a[e],
                                                     qn_b[e], kn_b[e], d)
            aq, ak, av, bq, iq, bk2, bv2, ik2, iwp = _proj(
                x, norm_mix[l], sc1, sh1, w, gain, groups, out_defs)
            lam_vecs = jnp.stack([lam_q1[e], lam_k1[e], lam_q2[e], lam_k2[e]]).astype(F32)
            ya = _diff_attention(aq, ak, av, lam_vecs, subln_a[e], l)
            yb = _dsa_attention(bq, iq, iwp, ik2, bk2, bv2)
            ys = (ya, yb)
        else:
            o = l // 2
            w, gain, groups, out_defs = _odd_layout(w_in_odd[o], qn_c[o], kn_c[o], d)
            q, k2, v2 = _proj(x, norm_mix[l], sc1, sh1, w, gain, groups, out_defs)
            ys = (_swa_attention(q, k2, v2, sinks_c[o]),)
        wo, wg, wu, wd = _layer_weights_bf16(l, w_out, ffn_gate, ffn_up, ffn_down)
        x = _out_ffn(x, ys, wo, g1, norm_ffn[l], sc2, sh2, g2, wg, wu, wd)
    return x
```
